```python
import functools
import jax, jax.numpy as jnp
from jax import lax
import numpy as np

D_MODEL = 2048
BATCH = 4
SEQ = 4096
DEPTH = 1
DEC_BATCH = 128
DEC_SEQ = 4
PAST_LEN = 16384
PAGE_SIZE = 128

D_RNN = D_MODEL // 2
RNN_BLOCKS = 8
RNN_BLOCK = D_RNN // RNN_BLOCKS
CONV_W = 4
LRU_C = 8.0
HEAD_DIM = 64
N_Q_HEADS = D_MODEL // 2 // HEAD_DIM
N_KV_HEADS = 4
GROUP = N_Q_HEADS // N_KV_HEADS
WINDOW = 128
ROT_DIM = HEAD_DIM // 4
ROPE_THETA = 500000.0
N_KEYS = 128
N_EXPERTS = N_KEYS * N_KEYS
PEER_HEADS = 8
PEER_TOPK = 16
D_KEY = 256
PEER_CHUNK = 128
D_PLE = 256
EPS = 1e-6
NEG_INF = -1e30
IN_SIZES = [D_RNN, D_RNN, N_Q_HEADS * HEAD_DIM, N_KV_HEADS * HEAD_DIM, N_KV_HEADS * HEAD_DIM, D_MODEL, D_MODEL]
D_IN = sum(IN_SIZES)

kernel_name = "hawk_swa_sink_peer_hybrid_step"


def rmsnorm(x, g):
    xf = x.astype(jnp.float32)
    y = xf * lax.rsqrt(jnp.mean(xf * xf, axis=-1, keepdims=True) + EPS) * g.astype(jnp.float32)
    return y.astype(x.dtype)


def rope_partial(x, pos):
    half = ROT_DIM // 2
    inv = ROPE_THETA ** (-jnp.arange(0, ROT_DIM, 2, dtype=jnp.float32) / ROT_DIM)
    ang = pos.astype(jnp.float32)[:, None] * inv[None, :]
    cos = jnp.cos(ang)[:, None, :]
    sin = jnp.sin(ang)[:, None, :]
    xf = x.astype(jnp.float32)
    x1, x2 = xf[..., :half], xf[..., half:ROT_DIM]
    y = jnp.concatenate([x1 * cos - x2 * sin, x2 * cos + x1 * sin, xf[..., ROT_DIM:]], axis=-1)
    return y.astype(x.dtype)


def causal_conv(x, buf, w, b):
    T = x.shape[1]
    xp = jnp.concatenate([buf.astype(x.dtype), x], axis=1)
    y = b + sum(w[k] * xp[:, k:k + T] for k in range(CONV_W))
    return y, xp[:, xp.shape[1] - (CONV_W - 1):]


def linear_recurrence(a, b, h0):
    b = b.at[:, 0].add(a[:, 0] * h0)

    def combine(c1, c2):
        a1, b1 = c1
        a2, b2 = c2
        return a1 * a2, a2 * b1 + b2

    _, h = lax.associative_scan(combine, (a, b), axis=1)
    return h


def rglru(x, h0, pos, w_r, b_r, w_i, b_i, lam):
    B, T, _ = x.shape
    xb = x.reshape(B, T, RNN_BLOCKS, RNN_BLOCK)
    r = jax.nn.sigmoid((jnp.einsum('btnc,ncd->btnd', xb, w_r).reshape(B, T, D_RNN) + b_r).astype(jnp.float32))
    i = jax.nn.sigmoid((jnp.einsum('btnc,ncd->btnd', xb, w_i).reshape(B, T, D_RNN) + b_i).astype(jnp.float32))
    log_a = -LRU_C * r * jax.nn.softplus(-lam.astype(jnp.float32))
    a = jnp.exp(log_a)
    mult = jnp.where(pos[None, :, None] == 0, 1.0, jnp.sqrt(-jnp.expm1(2.0 * log_a)))
    h = linear_recurrence(a, mult * i * x.astype(jnp.float32), h0.astype(jnp.float32))
    return h.astype(x.dtype), h[:, -1]


def sink_attention(q, k, v, mask, sinks):
    s = jnp.einsum('bnqkgd,bnskd->bnkgqs', q.astype(jnp.float32), k.astype(jnp.float32)) * (HEAD_DIM ** -0.5)
    s = jnp.where(mask[None, :, None, None], s, NEG_INF)
    sk = sinks.astype(jnp.float32).reshape(N_KV_HEADS, GROUP)[None, None, :, :, None, None]
    m = jnp.maximum(jnp.max(s, axis=-1, keepdims=True), sk)
    p = jnp.exp(s - m)
    denom = jnp.sum(p, axis=-1, keepdims=True) + jnp.exp(sk - m)
    return jnp.einsum('bnkgqs,bnskd->bnqkgd', p / denom, v.astype(jnp.float32))


def swa_prompt(q, k, v, sinks):
    B, S = q.shape[:2]
    nb = S // WINDOW
    qb = q.reshape(B, nb, WINDOW, N_KV_HEADS, GROUP, HEAD_DIM)
    kb = k.reshape(B, nb, WINDOW, N_KV_HEADS, HEAD_DIM)
    vb = v.reshape(B, nb, WINDOW, N_KV_HEADS, HEAD_DIM)
    kk = jnp.concatenate([jnp.concatenate([jnp.zeros_like(kb[:, :1]), kb[:, :-1]], axis=1), kb], axis=2)
    vv = jnp.concatenate([jnp.concatenate([jnp.zeros_like(vb[:, :1]), vb[:, :-1]], axis=1), vb], axis=2)
    i = jnp.arange(WINDOW)[:, None]
    j = jnp.arange(2 * WINDOW)[None, :]
    d = WINDOW + i - j
    valid = (d >= 0) & (d < WINDOW)
    mask = jnp.where(jnp.arange(nb)[:, None, None] == 0, valid & (j >= WINDOW), valid)
    o = sink_attention(qb, kk, vv, mask, sinks).reshape(B, S, N_Q_HEADS * HEAD_DIM)
    keep = min(WINDOW, S)
    return o, k[:, S - keep:], v[:, S - keep:]


def swa_sample(q, k, v, sinks, cache_k, cache_v, past_len):
    B, T = q.shape[:2]
    w_buf = cache_k.shape[1]
    kk = jnp.concatenate([cache_k.astype(k.dtype), k], axis=1)
    vv = jnp.concatenate([cache_v.astype(v.dtype), v], axis=1)
    qpos = past_len + jnp.arange(T)
    kpos = past_len - w_buf + jnp.arange(w_buf + T)
    d = qpos[:, None] - kpos[None, :]
    mask = (d >= 0) & (d < WINDOW)
    o = sink_attention(q.reshape(B, 1, T, N_KV_HEADS, GROUP, HEAD_DIM), kk[:, None], vv[:, None], mask[None], sinks)
    o = o.reshape(B, T, N_Q_HEADS * HEAD_DIM)
    return o, kk[:, kk.shape[1] - w_buf:], vv[:, vv.shape[1] - w_buf:]


def peer(x, w_q, sub_keys, u, v):
    B, T, D = x.shape
    n = B * T
    xt = x.reshape(n, D)
    q = (xt @ w_q).reshape(n, PEER_HEADS, 2, D_KEY // 2)
    s = jnp.einsum('nhpd,hpkd->nhpk', q.astype(jnp.float32), sub_keys.astype(jnp.float32))
    s1, i1 = lax.top_k(s[:, :, 0], PEER_TOPK)
    s2, i2 = lax.top_k(s[:, :, 1], PEER_TOPK)
    cand = (s1[..., :, None] + s2[..., None, :]).reshape(n, PEER_HEADS, PEER_TOPK * PEER_TOPK)
    cidx = (i1[..., :, None] * N_KEYS + i2[..., None, :]).reshape(n, PEER_HEADS, PEER_TOPK * PEER_TOPK)
    top_s, sel = lax.top_k(cand, PEER_TOPK)
    idx = jnp.take_along_axis(cidx, sel, axis=-1)
    g = jax.nn.softmax(top_s, axis=-1)
    n_pad = -(-n // PEER_CHUNK) * PEER_CHUNK
    pad = n_pad - n
    nc = n_pad // PEER_CHUNK
    xc = jnp.pad(xt, ((0, pad), (0, 0))).reshape(nc, PEER_CHUNK, D)
    ic = jnp.pad(idx, ((0, pad), (0, 0), (0, 0))).reshape(nc, PEER_CHUNK, PEER_HEADS, PEER_TOPK)
    gc = jnp.pad(g, ((0, pad), (0, 0), (0, 0))).reshape(nc, PEER_CHUNK, PEER_HEADS, PEER_TOPK)

    def chunk(args):
        xb, ib, gb = args
        hb = jax.nn.gelu(jnp.einsum('chkd,cd->chk', u[ib], xb).astype(jnp.float32))
        return jnp.einsum('chk,chkd->cd', (hb * gb).astype(v.dtype), v[ib])

    out = lax.map(chunk, (xc, ic, gc))
    return out.reshape(n_pad, D)[:n].reshape(B, T, D).astype(x.dtype)


def decoder_layer(x, ple, pos, conv_buf, h0, attn_fn, norm_mix, w_in, conv_w, conv_b, w_rgate, b_rgate,
                  w_igate, b_igate, lru_lambda, w_proj_rnn, q_norm, k_norm, attn_sinks, w_proj_attn, w_out,
                  norm_ffn, w_peer_q, peer_sub_keys, peer_u, peer_v, w_ple, norm_ple, w_ple_gate):
    B, T, _ = x.shape
    n1 = rmsnorm(x, norm_mix)
    z = n1 @ w_in
    offs = np.cumsum(IN_SIZES)[:-1].tolist()
    xr, gr, q, k, v, ga, gb = jnp.split(z, offs, axis=-1)
    xr, conv_new = causal_conv(xr, conv_buf, conv_w, conv_b)
    hr, h_last = rglru(xr, h0, pos, w_rgate, b_rgate, w_igate, b_igate, lru_lambda)
    branch_a = (hr * jax.nn.gelu(gr)) @ w_proj_rnn
    q = rope_partial(rmsnorm(q.reshape(B, T, N_Q_HEADS, HEAD_DIM), q_norm), pos)
    k = rope_partial(rmsnorm(k.reshape(B, T, N_KV_HEADS, HEAD_DIM), k_norm), pos)
    v = v.reshape(B, T, N_KV_HEADS, HEAD_DIM)
    o, k_buf, v_buf = attn_fn(q, k, v, attn_sinks)
    branch_b = o.astype(x.dtype) @ w_proj_attn
    merged = jax.nn.sigmoid(ga) * branch_a + jax.nn.sigmoid(gb) * branch_b
    x = x + merged @ w_out
    x = x + peer(rmsnorm(x, norm_ffn), w_peer_q, peer_sub_keys, peer_u, peer_v)
    x = x + (ple @ w_ple) * jax.nn.sigmoid(rmsnorm(x, norm_ple) @ w_ple_gate)
    return x, conv_new, h_last, k_buf, v_buf


def setup_inputs(seed: int = 0) -> dict:
    key = jax.random.key(seed)
    ks = jax.random.split(key, 40)
    f32 = jnp.float32
    nrm = lambda k, shape, scale: jax.random.normal(k, shape, f32) * scale
    w_buf = min(WINDOW, PAST_LEN)
    a0 = jax.random.uniform(ks[20], (DEPTH, D_RNN), f32, 0.9, 0.999)
    return {
        "x_prompt": nrm(ks[0], (BATCH, SEQ, D_MODEL), 1.0),
        "x_sample": nrm(ks[1], (DEC_BATCH, DEC_SEQ, D_MODEL), 1.0),
        "p_prompt": nrm(ks[2], (DEPTH, BATCH, SEQ, D_PLE), 1.0),
        "p_sample": nrm(ks[3], (DEPTH, DEC_BATCH, DEC_SEQ, D_PLE), 1.0),
        "state_conv": nrm(ks[4], (DEPTH, DEC_BATCH, CONV_W - 1, D_RNN), 1.0),
        "state_rglru": nrm(ks[5], (DEPTH, DEC_BATCH, D_RNN), 0.5),
        "cache_k": nrm(ks[6], (DEPTH, DEC_BATCH, w_buf, N_KV_HEADS, HEAD_DIM), 1.0),
        "cache_v": nrm(ks[7], (DEPTH, DEC_BATCH, w_buf, N_KV_HEADS, HEAD_DIM), 1.0),
        "norm_mix": 1.0 + nrm(ks[8], (DEPTH, D_MODEL), 0.01),
        "w_in": nrm(ks[9], (DEPTH, D_MODEL, D_IN), D_MODEL ** -0.5),
        "conv_w": nrm(ks[10], (DEPTH, CONV_W, D_RNN), CONV_W ** -0.5),
        "conv_b": nrm(ks[11], (DEPTH, D_RNN), 0.01),
        "w_rgate": nrm(ks[12], (DEPTH, RNN_BLOCKS, RNN_BLOCK, RNN_BLOCK), RNN_BLOCK ** -0.5),
        "b_rgate": nrm(ks[13], (DEPTH, D_RNN), 0.01),
        "w_igate": nrm(ks[14], (DEPTH, RNN_BLOCKS, RNN_BLOCK, RNN_BLOCK), RNN_BLOCK ** -0.5),
        "b_igate": nrm(ks[15], (DEPTH, D_RNN), 0.01),
        "lru_lambda": jnp.log(a0) - jnp.log1p(-a0),
        "w_proj_rnn": nrm(ks[16], (DEPTH, D_RNN, D_MODEL), D_RNN ** -0.5),
        "q_norm": 1.0 + nrm(ks[17], (DEPTH, HEAD_DIM), 0.01),
        "k_norm": 1.0 + nrm(ks[18], (DEPTH, HEAD_DIM), 0.01),
        "attn_sinks": nrm(ks[19], (DEPTH, N_Q_HEADS), 0.5),
        "w_proj_attn": nrm(ks[21], (DEPTH, N_Q_HEADS * HEAD_DIM, D_MODEL), (N_Q_HEADS * HEAD_DIM) ** -0.5),
        "w_out": nrm(ks[22], (DEPTH, D_MODEL, D_MODEL), D_MODEL ** -0.5),
        "norm_ffn": 1.0 + nrm(ks[23], (DEPTH, D_MODEL), 0.01),
        "w_peer_q": nrm(ks[24], (DEPTH, D_MODEL, PEER_HEADS * D_KEY), D_MODEL ** -0.5),
        "peer_sub_keys": nrm(ks[25], (DEPTH, PEER_HEADS, 2, N_KEYS, D_KEY // 2), (D_KEY // 2) ** -0.5),
        "peer_u": nrm(ks[26], (DEPTH, N_EXPERTS, D_MODEL), D_MODEL ** -0.5),
        "peer_v": nrm(ks[27], (DEPTH, N_EXPERTS, D_MODEL), (PEER_HEADS * PEER_TOPK) ** -0.5),
        "w_ple": nrm(ks[28], (DEPTH, D_PLE, D_MODEL), D_PLE ** -0.5),
        "norm_ple": 1.0 + nrm(ks[29], (DEPTH, D_MODEL), 0.01),
        "w_ple_gate": nrm(ks[30], (DEPTH, D_MODEL, D_MODEL), D_MODEL ** -0.5),
    }


def reference(x_prompt, x_sample, p_prompt, p_sample, state_conv, state_rglru, cache_k, cache_v,
              norm_mix, w_in, conv_w, conv_b, w_rgate, b_rgate, w_igate, b_igate, lru_lambda, w_proj_rnn,
              q_norm, k_norm, attn_sinks, w_proj_attn, w_out, norm_ffn, w_peer_q, peer_sub_keys, peer_u,
              peer_v, w_ple, norm_ple, w_ple_gate):
    B, S, _ = x_prompt.shape
    DB, DS, _ = x_sample.shape
    pos_prompt = jnp.arange(S, dtype=jnp.int32)
    pos_sample = PAST_LEN + jnp.arange(DS, dtype=jnp.int32)
    yp, ys = x_prompt, x_sample
    pc, ph, pk, pv, sc, sh, sk, sv = [], [], [], [], [], [], [], []
    for l in range(DEPTH):
        w = (norm_mix[l], w_in[l], conv_w[l], conv_b[l], w_rgate[l], b_rgate[l], w_igate[l], b_igate[l],
             lru_lambda[l], w_proj_rnn[l], q_norm[l], k_norm[l], attn_sinks[l], w_proj_attn[l], w_out[l],
             norm_ffn[l], w_peer_q[l], peer_sub_keys[l], peer_u[l], peer_v[l], w_ple[l], norm_ple[l],
             w_ple_gate[l])
        yp, c_new, h_new, k_new, v_new = decoder_layer(
            yp, p_prompt[l], pos_prompt,
            jnp.zeros((B, CONV_W - 1, D_RNN), x_prompt.dtype), jnp.zeros((B, D_RNN), jnp.float32),
            swa_prompt, *w)
        pc.append(c_new); ph.append(h_new); pk.append(k_new); pv.append(v_new)
        attn_fn = functools.partial(swa_sample, cache_k=cache_k[l], cache_v=cache_v[l], past_len=PAST_LEN)
        ys, c_new, h_new, k_new, v_new = decoder_layer(
            ys, p_sample[l], pos_sample, state_conv[l], state_rglru[l], attn_fn, *w)
        sc.append(c_new); sh.append(h_new); sk.append(k_new); sv.append(v_new)
    prompt_conv, prompt_rglru = jnp.stack(pc), jnp.stack(ph)
    prompt_k, prompt_v = jnp.stack(pk), jnp.stack(pv)
    sample_conv, sample_rglru = jnp.stack(sc), jnp.stack(sh)
    sample_k, sample_v = jnp.stack(sk), jnp.stack(sv)
    return (yp, ys, prompt_conv, prompt_rglru, prompt_k, prompt_v, sample_conv, sample_rglru, sample_k, sample_v)
```

```python
import functools

import jax
import jax.numpy as jnp
import numpy as np
from jax import lax
from jax.experimental import pallas as pl
from jax.experimental.pallas import tpu as pltpu

D_MODEL = 2048
D_RNN = D_MODEL // 2
RNN_BLOCKS = 8
RNN_BLOCK = D_RNN // RNN_BLOCKS
CONV_W = 4
LRU_C = 8.0
HEAD_DIM = 64
N_Q_HEADS = D_MODEL // 2 // HEAD_DIM
N_KV_HEADS = 4
GROUP = N_Q_HEADS // N_KV_HEADS
D_Q = N_Q_HEADS * HEAD_DIM
D_KV = N_KV_HEADS * HEAD_DIM
WINDOW = 128
ROT_DIM = HEAD_DIM // 4
ROPE_THETA = 500000.0
N_KEYS = 128
N_EXPERTS = N_KEYS * N_KEYS
PEER_HEADS = 8
PEER_TOPK = 16
D_KEY = 256
D_HALF_KEY = D_KEY // 2
N_PICKS = PEER_HEADS * PEER_TOPK
D_PLE = 256
EPS = 1e-6
NEG_INF = -1e30
PAST_LEN = 16384

LANES = 128
SUBLANES = 8
BF16_TILE_ROWS = 16
VMEM_LIMIT_BYTES = 56 * 1024 * 1024

OFF_GA = 0
OFF_GB = OFF_GA + D_MODEL
OFF_XR = OFF_GB + D_MODEL
OFF_GR = OFF_XR + D_RNN
OFF_Q = OFF_GR + D_RNN
OFF_K = OFF_Q + D_Q
OFF_V = OFF_K + D_KV
D_IN = OFF_V + D_KV

BF16 = jnp.bfloat16
F32 = jnp.float32


def _params(*sem):
    return pltpu.CompilerParams(dimension_semantics=sem, vmem_limit_bytes=VMEM_LIMIT_BYTES)


def _rmsnorm_rows(x, g):
    return x * lax.rsqrt(jnp.mean(x * x, axis=-1, keepdims=True) + EPS) * g


def _dot(a, b):
    return jnp.dot(a, b, preferred_element_type=F32)


def _dot_nt(a, b):
    return lax.dot_general(a, b, (((1,), (1,)), ((), ())), preferred_element_type=F32)


def _in_proj_kernel(x_ref, g_ref, w_ref, z_ref, xn_ref):
    @pl.when(pl.program_id(1) == 0)
    def _():
        xn_ref[...] = _rmsnorm_rows(x_ref[...], g_ref[...]).astype(BF16)

    z_ref[...] = _dot(xn_ref[...], w_ref[...])


def _in_proj(x, g, w, tm=512, tn=1536):
    n = x.shape[0]
    tm = min(tm, n)
    return pl.pallas_call(
        _in_proj_kernel,
        out_shape=jax.ShapeDtypeStruct((n, D_IN), F32),
        grid=(n // tm, D_IN // tn),
        in_specs=[
            pl.BlockSpec((tm, D_MODEL), lambda i, j: (i, 0)),
            pl.BlockSpec((1, D_MODEL), lambda i, j: (0, 0)),
            pl.BlockSpec((D_MODEL, tn), lambda i, j: (0, j)),
        ],
        out_specs=pl.BlockSpec((tm, tn), lambda i, j: (i, j)),
        scratch_shapes=[pltpu.VMEM((tm, D_MODEL), BF16)],
        compiler_params=_params("parallel", "arbitrary"),
        name="in_proj",
    )(x, g, w)


def _softplus(x):
    return jnp.maximum(x, 0.0) + jnp.log(1.0 + jnp.exp(-jnp.abs(x)))


def _neg_expm1(x):
    t = jnp.tanh(-0.5 * x)
    return 2.0 * t / (1.0 + t)


def _lru_coeffs(xc, wr, br, wi, bi, lam, first_is_pos0):
    xb = xc.astype(BF16)
    r = jax.nn.sigmoid(_dot(xb, wr) + br)
    i = jax.nn.sigmoid(_dot(xb, wi) + bi)
    log_a = -LRU_C * r * _softplus(-lam)
    a = jnp.exp(log_a)
    mult = jnp.sqrt(_neg_expm1(2.0 * log_a))
    if first_is_pos0 is not None:
        row = lax.broadcasted_iota(jnp.int32, xc.shape, 0)
        mult = jnp.where(jnp.logical_and(first_is_pos0, row == 0), 1.0, mult)
    return a, mult * i * xc


def _rnn_prompt_kernel(xr_ref, gr_ref, cw_ref, cb_ref, wr_ref, br_ref, wi_ref, bi_ref, lam_ref,
                       yr_ref, hlast_ref, tail_ref, h_ref, *, tb):
    t = pl.program_id(1)

    @pl.when(t == 0)
    def _():
        tail_ref[...] = jnp.zeros_like(tail_ref)
        h_ref[...] = jnp.zeros_like(h_ref)

    x = xr_ref[...]
    tail = tail_ref[...]
    row8 = lax.broadcasted_iota(jnp.int32, (SUBLANES, D_RNN), 0)
    cw = cw_ref[...]
    xc = cb_ref[...] + cw[CONV_W - 1:CONV_W, :] * x
    for k in range(1, CONV_W):
        xs = pltpu.roll(x, k, axis=0)
        head = jnp.where(row8 < k, pltpu.roll(tail, k, axis=0), xs[:SUBLANES])
        xs = jnp.concatenate([head, xs[SUBLANES:]], axis=0)
        xc = xc + cw[CONV_W - 1 - k:CONV_W - k, :] * xs
    tail_ref[...] = x[tb - SUBLANES:, :]

    a, b = _lru_coeffs(xc, wr_ref[...], br_ref[...], wi_ref[...], bi_ref[...], lam_ref[...], t == 0)

    row = lax.broadcasted_iota(jnp.int32, (tb, D_RNN), 0)
    d = 1
    while d < tb:
        a_sh = pltpu.roll(a, d, axis=0)
        b_sh = pltpu.roll(b, d, axis=0)
        keep = row < d
        b = jnp.where(keep, b, a * b_sh + b)
        a = jnp.where(keep, a, a * a_sh)
        d *= 2
    h = b + a * h_ref[...]
    h_ref[...] = h[tb - 1:tb, :]
    hlast_ref[0] = h[tb - 1:tb, :]
    yr_ref[...] = (h * jax.nn.gelu(gr_ref[...])).astype(BF16)


def _rnn_prompt(z, batch, seq, cw, cb, wr, br, wi, bi, lam, tb=256):
    nt = seq // tb
    vec = pl.BlockSpec((1, D_RNN), lambda b, t: (0, 0))
    mat = pl.BlockSpec((D_RNN, D_RNN), lambda b, t: (0, 0))
    return pl.pallas_call(
        functools.partial(_rnn_prompt_kernel, tb=tb),
        out_shape=(jax.ShapeDtypeStruct((batch * seq, D_RNN), BF16),
                   jax.ShapeDtypeStruct((batch, 1, D_RNN), F32)),
        grid=(batch, nt),
        in_specs=[
            pl.BlockSpec((tb, D_RNN), lambda b, t: (b * nt + t, OFF_XR // D_RNN)),
            pl.BlockSpec((tb, D_RNN), lambda b, t: (b * nt + t, OFF_GR // D_RNN)),
            pl.BlockSpec((CONV_W, D_RNN), lambda b, t: (0, 0)),
            vec, mat, vec, mat, vec, vec,
        ],
        out_specs=(pl.BlockSpec((tb, D_RNN), lambda b, t: (b * nt + t, 0)),
                   pl.BlockSpec((1, 1, D_RNN), lambda b, t: (b, 0, 0))),
        scratch_shapes=[pltpu.VMEM((SUBLANES, D_RNN), F32), pltpu.VMEM((1, D_RNN), F32)],
        compiler_params=_params("parallel", "arbitrary"),
        name="rnn_prompt",
    )(z, z, cw, cb, wr, br, wi, bi, lam)


def _rnn_sample_kernel(xr_ref, gr_ref, buf_ref, h0_ref, cw_ref, cb_ref, wr_ref, br_ref, wi_ref,
                       bi_ref, lam_ref, yr_ref, hlast_ref, *, steps):
    cw = cw_ref[...]
    xp = [buf_ref[k] for k in range(CONV_W - 1)] + [xr_ref[s] for s in range(steps)]
    h = h0_ref[...]
    for s in range(steps):
        xc = cb_ref[...] + sum(cw[k:k + 1, :] * xp[s + k] for k in range(CONV_W))
        a, b = _lru_coeffs(xc, wr_ref[...], br_ref[...], wi_ref[...], bi_ref[...], lam_ref[...], None)
        h = a * h + b
        yr_ref[s] = (h * jax.nn.gelu(gr_ref[s])).astype(BF16)
    hlast_ref[...] = h


def _rnn_sample(xr_t, gr_t, buf_t, h0, cw, cb, wr, br, wi, bi, lam):
    steps, db, _ = xr_t.shape
    return pl.pallas_call(
        functools.partial(_rnn_sample_kernel, steps=steps),
        out_shape=(jax.ShapeDtypeStruct((steps, db, D_RNN), BF16),
                   jax.ShapeDtypeStruct((db, D_RNN), F32)),
        compiler_params=pltpu.CompilerParams(vmem_limit_bytes=VMEM_LIMIT_BYTES),
        name="rnn_sample",
    )(xr_t, gr_t, buf_t, h0, cw, cb, wr, br, wi, bi, lam)


def _headnorm_rope(x, gain, seg_ones, cos_t, sin_up_t, sin_dn_t, scale):
    w = x.shape[1]
    sq = x * x
    hi = sq.astype(BF16)
    lo = (sq - hi.astype(F32)).astype(BF16)
    ms = (_dot(hi, seg_ones) + _dot(lo, seg_ones)) * (1.0 / HEAD_DIM)
    xn = x * lax.rsqrt(ms + EPS)
    outs = []
    for c in range(w // LANES):
        xt = xn[:, c * LANES:(c + 1) * LANES] * gain
        up = pltpu.roll(xt, LANES - ROT_DIM // 2, axis=1)
        dn = pltpu.roll(xt, ROT_DIM // 2, axis=1)
        outs.append((xt * cos_t + up * sin_up_t + dn * sin_dn_t) * scale)
    return jnp.concatenate(outs, axis=1)


def _qk_prep_kernel(q_ref, k_ref, v_ref, cos_ref, sup_ref, sdn_ref, qg_ref, kg_ref, oq_ref, ok_ref,
                    qo_ref, kf_ref, kb_ref, vb_ref):
    cos_t, sup, sdn = cos_ref[...], sup_ref[...], sdn_ref[...]
    q = _headnorm_rope(q_ref[...], qg_ref[...], oq_ref[...], cos_t, sup, sdn, HEAD_DIM ** -0.5)
    qo_ref[...] = q.astype(BF16)
    k = _headnorm_rope(k_ref[...], kg_ref[...], ok_ref[...], cos_t, sup, sdn, 1.0)
    kf_ref[...] = k
    kb_ref[...] = k.astype(BF16)
    vb_ref[...] = v_ref[...].astype(BF16)


def _qk_prep(z, tabs, qg, kg, ones_q, ones_k, tm=512):
    n = z.shape[0]
    tm = min(tm, n)
    cos_t, sup_t, sdn_t = tabs
    ntab = cos_t.shape[0] // tm
    tab = pl.BlockSpec((tm, LANES), lambda i: (i % ntab, 0))
    const = lambda shape: pl.BlockSpec(shape, lambda i: (0, 0))
    return pl.pallas_call(
        _qk_prep_kernel,
        out_shape=(jax.ShapeDtypeStruct((n, D_Q), BF16), jax.ShapeDtypeStruct((n, D_KV), F32),
                   jax.ShapeDtypeStruct((n, D_KV), BF16), jax.ShapeDtypeStruct((n, D_KV), BF16)),
        grid=(n // tm,),
        in_specs=[
            pl.BlockSpec((tm, D_Q), lambda i: (i, OFF_Q // D_Q)),
            pl.BlockSpec((tm, D_KV), lambda i: (i, OFF_K // D_KV)),
            pl.BlockSpec((tm, D_KV), lambda i: (i, OFF_V // D_KV)),
            tab, tab, tab,
            const((1, LANES)), const((1, LANES)), const((D_Q, D_Q)), const((D_KV, D_KV)),
        ],
        out_specs=(pl.BlockSpec((tm, D_Q), lambda i: (i, 0)), pl.BlockSpec((tm, D_KV), lambda i: (i, 0)),
                   pl.BlockSpec((tm, D_KV), lambda i: (i, 0)), pl.BlockSpec((tm, D_KV), lambda i: (i, 0))),
        compiler_params=_params("parallel"),
        name="qk_prep",
    )(z, z, z, cos_t, sup_t, sdn_t, qg, kg, ones_q, ones_k)


def _sink_attention(q, k, v, valid, sink_ref, tq):
    lane_head = lax.broadcasted_iota(jnp.int32, (tq, D_KV), 1) // HEAD_DIM
    rowg = lax.broadcasted_iota(jnp.int32, (GROUP * tq, 1), 0) // tq
    validg = jnp.concatenate([valid] * GROUP, axis=0)
    out = [jnp.zeros((tq, D_KV), F32) for _ in range(GROUP)]
    for j in range(N_KV_HEADS):
        sel = lane_head == j
        keep = jnp.where(sel, 1.0, 0.0).astype(BF16)
        qs = jnp.concatenate([q[:, g * D_KV:(g + 1) * D_KV] * keep for g in range(GROUP)], axis=0)
        s = jnp.where(validg, _dot_nt(qs, k), NEG_INF)
        sk = jnp.zeros((GROUP * tq, 1), F32)
        for g in range(GROUP):
            sk = jnp.where(rowg == g, sink_ref[j * GROUP + g], sk)
        m = jnp.maximum(jnp.max(s, axis=-1, keepdims=True), sk)
        p = jnp.exp(s - m)
        denom = jnp.sum(p, axis=-1, keepdims=True) + jnp.exp(sk - m)
        pv = _dot(p.astype(BF16), v) / denom
        for g in range(GROUP):
            out[g] = jnp.where(sel, pv[g * tq:(g + 1) * tq], out[g])
    return jnp.concatenate(out, axis=1)


def _attn_prompt_kernel(sink_ref, q_ref, kp_ref, kc_ref, vp_ref, vc_ref, o_ref):
    nb = pl.program_id(1)
    k = jnp.concatenate([kp_ref[...], kc_ref[...]], axis=0)
    v = jnp.concatenate([vp_ref[...], vc_ref[...]], axis=0)
    i = lax.broadcasted_iota(jnp.int32, (WINDOW, 2 * WINDOW), 0)
    j = lax.broadcasted_iota(jnp.int32, (WINDOW, 2 * WINDOW), 1)
    d = WINDOW + i - j
    valid = (d >= 0) & (d < WINDOW) & ((j >= WINDOW) | (nb > 0))
    o_ref[...] = _sink_attention(q_ref[...], k, v, valid, sink_ref, WINDOW).astype(BF16)


def _attn_prompt(sinks, q, kb, vb, batch, seq):
    nblk = seq // WINDOW
    cur = lambda w: pl.BlockSpec((WINDOW, w), lambda b, t: (b * nblk + t, 0))
    prev = lambda w: pl.BlockSpec((WINDOW, w), lambda b, t: (b * nblk + jnp.maximum(t - 1, 0), 0))
    return pl.pallas_call(
        _attn_prompt_kernel,
        out_shape=jax.ShapeDtypeStruct((batch * seq, D_Q), BF16),
        grid=(batch, nblk),
        in_specs=[pl.BlockSpec(memory_space=pltpu.SMEM), cur(D_Q), prev(D_KV), cur(D_KV),
                  prev(D_KV), cur(D_KV)],
        out_specs=cur(D_Q),
        compiler_params=_params("parallel", "arbitrary"),
        name="attn_prompt",
    )(sinks, q, kb, kb, vb, vb)


def _attn_sample_kernel(sink_ref, q_ref, kn_ref, vn_ref, ck_ref, cv_ref, o_ref, *, tq, wbuf):
    k = jnp.concatenate([ck_ref[0].astype(BF16), kn_ref[0]], axis=0)
    v = jnp.concatenate([cv_ref[0].astype(BF16), vn_ref[0]], axis=0)
    t = lax.broadcasted_iota(jnp.int32, (tq, wbuf + tq), 0)
    c = lax.broadcasted_iota(jnp.int32, (tq, wbuf + tq), 1)
    d = wbuf + t - c
    valid = (d >= 0) & (d < WINDOW)
    o_ref[0] = _sink_attention(q_ref[0], k, v, valid, sink_ref, tq).astype(BF16)


def _attn_sample(sinks, q3, kn3, vn3, cache_k, cache_v):
    db, tq, _ = q3.shape
    wbuf = cache_k.shape[1]
    blk = lambda r, w: pl.BlockSpec((1, r, w), lambda b: (b, 0, 0))
    return pl.pallas_call(
        functools.partial(_attn_sample_kernel, tq=tq, wbuf=wbuf),
        out_shape=jax.ShapeDtypeStruct((db, tq, D_Q), BF16),
        grid=(db,),
        in_specs=[pl.BlockSpec(memory_space=pltpu.SMEM), blk(tq, D_Q), blk(tq, D_KV), blk(tq, D_KV),
                  blk(wbuf, D_KV), blk(wbuf, D_KV)],
        out_specs=blk(tq, D_Q),
        compiler_params=_params("parallel"),
        name="attn_sample",
    )(sinks, q3, kn3, vn3, cache_k, cache_v)


def _merge_kernel(yr_ref, o_ref, ga_ref, gb_ref, wr_ref, wa_ref, m_ref):
    a = _dot(yr_ref[...], wr_ref[...])
    b = _dot(o_ref[...], wa_ref[...])
    m_ref[...] = (jax.nn.sigmoid(ga_ref[...]) * a + jax.nn.sigmoid(gb_ref[...]) * b).astype(BF16)


def _merge(yr, o, z, wr, wa, tm=512, tn=1024):
    n = yr.shape[0]
    tm = min(tm, n)
    nj = D_MODEL // tn
    return pl.pallas_call(
        _merge_kernel,
        out_shape=jax.ShapeDtypeStruct((n, D_MODEL), BF16),
        grid=(n // tm, nj),
        in_specs=[
            pl.BlockSpec((tm, D_RNN), lambda i, j: (i, 0)),
            pl.BlockSpec((tm, D_Q), lambda i, j: (i, 0)),
            pl.BlockSpec((tm, tn), lambda i, j: (i, OFF_GA // tn + j)),
            pl.BlockSpec((tm, tn), lambda i, j: (i, OFF_GB // tn + j)),
            pl.BlockSpec((D_RNN, tn), lambda i, j: (0, j)),
            pl.BlockSpec((D_Q, tn), lambda i, j: (0, j)),
        ],
        out_specs=pl.BlockSpec((tm, tn), lambda i, j: (i, j)),
        compiler_params=_params("parallel", "arbitrary"),
        name="merge",
    )(yr, o, z, z, wr, wa)


def _out_proj_kernel(x_ref, m_ref, w_ref, g_ref, x1_ref, n2_ref):
    x1 = x_ref[...] + _dot(m_ref[...], w_ref[...])
    x1_ref[...] = x1
    n2_ref[...] = _rmsnorm_rows(x1, g_ref[...]).astype(BF16)


def _out_proj(x, m, w, g, tm=512):
    n = x.shape[0]
    tm = min(tm, n)
    row = lambda dt: pl.BlockSpec((tm, D_MODEL), lambda i: (i, 0))
    return pl.pallas_call(
        _out_proj_kernel,
        out_shape=(jax.ShapeDtypeStruct((n, D_MODEL), F32), jax.ShapeDtypeStruct((n, D_MODEL), BF16)),
        grid=(n // tm,),
        in_specs=[row(F32), row(BF16), pl.BlockSpec((D_MODEL, D_MODEL), lambda i: (0, 0)),
                  pl.BlockSpec((1, D_MODEL), lambda i: (0, 0))],
        out_specs=(row(F32), row(BF16)),
        compiler_params=_params("parallel"),
        name="out_proj",
    )(x, m, w, g)


def _topk_rows(s, k):
    rows, t = s.shape
    rid = lax.broadcasted_iota(jnp.int32, (rows, t), 0).astype(F32)
    slot = lax.broadcasted_iota(jnp.int32, (k, t), 0)
    vals = jnp.zeros((k, t), F32)
    ids = jnp.zeros((k, t), F32)
    for r in range(k):
        m = jnp.max(s, axis=0, keepdims=True)
        i = jnp.min(jnp.where(s == m, rid, float(rows)), axis=0, keepdims=True)
        vals = jnp.where(slot == r, m, vals)
        ids = jnp.where(slot == r, i, ids)
        s = jnp.where(rid == i, -jnp.inf, s)
    return vals, ids


def _peer_route_kernel(n2_ref, wq_ref, sk_ref, idx_ref, gate_ref):
    q = _dot(n2_ref[...], wq_ref[...]).astype(BF16)
    for h in range(PEER_HEADS):
        top = []
        for p in range(2):
            c = (2 * h + p) * D_HALF_KEY
            s = _dot_nt(sk_ref[2 * h + p], q[:, c:c + D_HALF_KEY])
            top.append(_topk_rows(s, PEER_TOPK))
        (s1, i1), (s2, i2) = top
        cand = jnp.concatenate([s1[a:a + 1, :] + s2 for a in range(PEER_TOPK)], axis=0)
        cidx = jnp.concatenate([i1[a:a + 1, :] * float(N_KEYS) + i2 for a in range(PEER_TOPK)], axis=0)
        top_s, pos = _topk_rows(cand, PEER_TOPK)
        rid = lax.broadcasted_iota(jnp.int32, cand.shape, 0).astype(F32)
        slot = lax.broadcasted_iota(jnp.int32, top_s.shape, 0)
        ids = jnp.zeros(top_s.shape, F32)
        for r in range(PEER_TOPK):
            e = jnp.max(jnp.where(rid == pos[r:r + 1, :], cidx, -1.0), axis=0, keepdims=True)
            ids = jnp.where(slot == r, e, ids)
        w = jnp.exp(top_s - top_s[0:1, :])
        gate_ref[h * PEER_TOPK:(h + 1) * PEER_TOPK, :] = w / jnp.sum(w, axis=0, keepdims=True)
        idx_ref[h * PEER_TOPK:(h + 1) * PEER_TOPK, :] = ids.astype(jnp.int32)


def _peer_route(n2, wq, sk, tm=256):
    n = n2.shape[0]
    tm = min(tm, n)
    return pl.pallas_call(
        _peer_route_kernel,
        out_shape=(jax.ShapeDtypeStruct((N_PICKS, n), jnp.int32), jax.ShapeDtypeStruct((N_PICKS, n), F32)),
        grid=(n // tm,),
        in_specs=[pl.BlockSpec((tm, D_MODEL), lambda i: (i, 0)),
                  pl.BlockSpec((D_MODEL, PEER_HEADS * D_KEY), lambda i: (0, 0)),
                  pl.BlockSpec((2 * PEER_HEADS, N_KEYS, D_HALF_KEY), lambda i: (0, 0, 0))],
        out_specs=(pl.BlockSpec((N_PICKS, tm), lambda i: (0, i)), pl.BlockSpec((N_PICKS, tm), lambda i: (0, i))),
        compiler_params=_params("parallel"),
        name="peer_route",
    )(n2, wq, sk)


D_TILES = D_MODEL // LANES
assert D_TILES == BF16_TILE_ROWS


def _peer_mix_kernel(idx_ref, idxn_ref, x_ref, gate_ref, exp_ref, expt_ref, tab_ref, o_ref, buf_ref, d_ref,
                     sem_ref, *, tb, nsteps):
    i = pl.program_id(0)
    slot = i % 2

    def row_copy(src_row, s, r):
        return pltpu.make_async_copy(tab_ref.at[src_row], buf_ref.at[s, r], sem_ref.at[s])

    def start_tile(ids_ref, s):
        def body(t, carry):
            for j in range(N_PICKS):
                row_copy(ids_ref[t, j], s, t * N_PICKS + j).start()
            return carry
        lax.fori_loop(0, tb, body, 0)

    @pl.when(i == 0)
    def _():
        start_tile(idx_ref, 0)

    @pl.when(i + 1 < nsteps)
    def _():
        start_tile(idxn_ref, 1 - slot)

    def wait_body(r, carry):
        row_copy(0, slot, r).wait()
        return carry
    lax.fori_loop(0, tb * N_PICKS, wait_body, 0)

    rows = N_PICKS * D_TILES
    diag = (lax.broadcasted_iota(jnp.int32, (D_TILES, rows), 1) % D_TILES
            == lax.broadcasted_iota(jnp.int32, (D_TILES, rows), 0))
    for t in range(tb):
        u = buf_ref[slot, t * N_PICKS:(t + 1) * N_PICKS, 0:D_TILES, :].reshape(rows, LANES)
        g = _dot_nt(x_ref[t], u)
        d_ref[t:t + 1, :] = jnp.sum(jnp.where(diag, g, 0.0), axis=0, keepdims=True)
    d = d_ref[...]
    d_hi = d.astype(BF16)
    d_lo = (d - d_hi.astype(F32)).astype(BF16)
    h = _dot(d_hi, expt_ref[...]) + _dot(d_lo, expt_ref[...])
    w = (jax.nn.gelu(h) * gate_ref[...]).astype(BF16)
    wrow = _dot(w, exp_ref[...])
    for t in range(tb):
        v = buf_ref[slot, t * N_PICKS:(t + 1) * N_PICKS, D_TILES:2 * D_TILES, :].reshape(rows, LANES)
        wm = jnp.where(diag, wrow[t:t + 1, :], 0.0).astype(BF16)
        o_ref[t] = _dot(wm, v)


def _peer_mix(idx, gates, n2, table, expand, expand_t, tb=BF16_TILE_ROWS):
    n = idx.shape[0]
    nsteps = n // tb
    x3 = n2.reshape(n, D_TILES, LANES)
    rows = N_PICKS * D_TILES
    out = pl.pallas_call(
        functools.partial(_peer_mix_kernel, tb=tb, nsteps=nsteps),
        out_shape=jax.ShapeDtypeStruct((n, D_TILES, LANES), F32),
        grid=(nsteps,),
        in_specs=[
            pl.BlockSpec((tb, N_PICKS), lambda i: (i, 0), memory_space=pltpu.SMEM),
            pl.BlockSpec((tb, N_PICKS), lambda i: (jnp.minimum(i + 1, nsteps - 1), 0),
                         memory_space=pltpu.SMEM),
            pl.BlockSpec((tb, D_TILES, LANES), lambda i: (i, 0, 0)),
            pl.BlockSpec((tb, N_PICKS), lambda i: (i, 0)),
            pl.BlockSpec((N_PICKS, rows), lambda i: (0, 0)),
            pl.BlockSpec((rows, N_PICKS), lambda i: (0, 0)),
            pl.BlockSpec(memory_space=pl.ANY),
        ],
        out_specs=pl.BlockSpec((tb, D_TILES, LANES), lambda i: (i, 0, 0)),
        scratch_shapes=[pltpu.VMEM((2, tb * N_PICKS, 2 * D_TILES, LANES), BF16),
                        pltpu.VMEM((tb, rows), F32),
                        pltpu.SemaphoreType.DMA((2,))],
        compiler_params=_params("arbitrary"),
        name="peer_mix",
    )(idx, idx, x3, gates, expand, expand_t, table)
    return out.reshape(n, D_MODEL)


def _ple_kernel(x1_ref, po_ref, ple_ref, g_ref, wp_ref, wg_ref, y_ref):
    x2 = x1_ref[...] + po_ref[...]
    n3 = _rmsnorm_rows(x2, g_ref[...]).astype(BF16)
    emb = _dot(ple_ref[...].astype(BF16), wp_ref[...])
    y_ref[...] = x2 + emb * jax.nn.sigmoid(_dot(n3, wg_ref[...]))


def _ple(x1, po, ple, g, wp, wg, tm=512):
    n = x1.shape[0]
    tm = min(tm, n)
    row = pl.BlockSpec((tm, D_MODEL), lambda i: (i, 0))
    return pl.pallas_call(
        _ple_kernel,
        out_shape=jax.ShapeDtypeStruct((n, D_MODEL), F32),
        grid=(n // tm,),
        in_specs=[row, row, pl.BlockSpec((tm, D_PLE), lambda i: (i, 0)),
                  pl.BlockSpec((1, D_MODEL), lambda i: (0, 0)),
                  pl.BlockSpec((D_PLE, D_MODEL), lambda i: (0, 0)),
                  pl.BlockSpec((D_MODEL, D_MODEL), lambda i: (0, 0))],
        out_specs=row,
        compiler_params=_params("parallel"),
        name="ple",
    )(x1, po, ple, g, wp, wg)


def _rope_tables(pos):
    half = ROT_DIM // 2
    inv = ROPE_THETA ** (-jnp.arange(0, ROT_DIM, 2, dtype=F32) / ROT_DIM)
    ang = pos.astype(F32)[:, None] * inv[None, :]
    cos, sin = jnp.cos(ang), jnp.sin(ang)
    n = pos.shape[0]
    pad = jnp.zeros((n, HEAD_DIM - ROT_DIM), F32)
    zh = jnp.zeros((n, half), F32)
    cos_h = jnp.concatenate([cos, cos, pad + 1.0], axis=1)
    up_h = jnp.concatenate([-sin, zh, pad], axis=1)
    dn_h = jnp.concatenate([zh, sin, pad], axis=1)
    rep = LANES // HEAD_DIM
    return tuple(jnp.tile(a, (1, rep)) for a in (cos_h, up_h, dn_h))


def _block_diag(w):
    eye = jnp.eye(RNN_BLOCKS, dtype=w.dtype)
    return jnp.einsum("ncd,nm->ncmd", w, eye).reshape(D_RNN, D_RNN)


def _head_perm():
    return np.array([j * GROUP + g for g in range(GROUP) for j in range(N_KV_HEADS)])


def _token_pipeline_tail(x, z, yr, o, w, ple):
    m = _merge(yr, o, z, w["proj_rnn"], w["proj_attn"])
    x1, n2 = _out_proj(x, m, w["out"], w["norm_ffn"])
    idx_t, gate_t = _peer_route(n2, w["peer_q"], w["sub_keys"])
    po = _peer_mix(idx_t.T, gate_t.T, n2, w["peer_table"], w["expand"], w["expand_t"])
    return _ple(x1, po, ple, w["norm_ple"], w["ple"], w["ple_gate"])


def kernel(x_prompt, x_sample, p_prompt, p_sample, state_conv, state_rglru, cache_k, cache_v, norm_mix, w_in, conv_w, conv_b, w_rgate, b_rgate, w_igate, b_igate, lru_lambda, w_proj_rnn, q_norm, k_norm, attn_sinks, w_proj_attn, w_out, norm_ffn, w_peer_q, peer_sub_keys, peer_u, peer_v, w_ple, norm_ple, w_ple_gate):
    depth = w_in.shape[0]
    assert depth == 1
    l = 0
    B, S, _ = x_prompt.shape
    DB, DS, _ = x_sample.shape
    wbuf = cache_k.shape[2]

    hp = _head_perm()
    offs = np.cumsum([0, D_RNN, D_RNN, D_Q, D_KV, D_KV, D_MODEL, D_MODEL])
    xr_c, gr_c, q_c, k_c, v_c, ga_c, gb_c = [np.arange(offs[i], offs[i + 1]) for i in range(7)]
    q_c = q_c.reshape(N_Q_HEADS, HEAD_DIM)[hp].reshape(-1)
    cols = np.concatenate([ga_c, gb_c, xr_c, gr_c, q_c, k_c, v_c])
    row2 = lambda a: a[l].reshape(1, -1)
    w = {
        "proj_rnn": w_proj_rnn[l].astype(BF16),
        "proj_attn": w_proj_attn[l].reshape(N_Q_HEADS, HEAD_DIM, D_MODEL)[hp].reshape(D_Q, D_MODEL).astype(BF16),
        "out": w_out[l].astype(BF16),
        "norm_ffn": row2(norm_ffn),
        "peer_q": w_peer_q[l].astype(BF16),
        "sub_keys": peer_sub_keys[l].reshape(2 * PEER_HEADS, N_KEYS, D_HALF_KEY).astype(BF16),
        "peer_table": jnp.concatenate(
            [peer_u[l].astype(BF16).reshape(N_EXPERTS, D_TILES, LANES),
             peer_v[l].astype(BF16).reshape(N_EXPERTS, D_TILES, LANES)], axis=1),
        "norm_ple": row2(norm_ple),
        "ple": w_ple[l].astype(BF16),
        "ple_gate": w_ple_gate[l].astype(BF16),
    }
    expand = np.repeat(np.eye(N_PICKS, dtype=np.float32), D_TILES, axis=1)
    w["expand"] = jnp.asarray(expand, BF16)
    w["expand_t"] = jnp.asarray(expand.T, BF16)
    w_in_b = w_in[l][:, cols].astype(BF16)
    g_mix = row2(norm_mix)
    cw, cb = conv_w[l], row2(conv_b)
    wr, br = _block_diag(w_rgate[l]).astype(BF16), row2(b_rgate)
    wi, bi = _block_diag(w_igate[l]).astype(BF16), row2(b_igate)
    lam = row2(lru_lambda)
    rep = LANES // HEAD_DIM
    qg = jnp.tile(q_norm[l], rep).reshape(1, LANES)
    kg = jnp.tile(k_norm[l], rep).reshape(1, LANES)
    seg = lambda width: jnp.asarray(
        np.kron(np.eye(width // HEAD_DIM, dtype=np.float32), np.ones((HEAD_DIM, HEAD_DIM), np.float32)), BF16)
    ones_q, ones_k = seg(D_Q), seg(D_KV)
    sinks = attn_sinks[l]

    xp = x_prompt.reshape(B * S, D_MODEL)
    zp = _in_proj(xp, g_mix, w_in_b)
    yr_p, h_p = _rnn_prompt(zp, B, S, cw, cb, wr, br, wi, bi, lam)
    tabs_p = _rope_tables(jnp.arange(S, dtype=jnp.int32))
    q_p, kf_p, kb_p, vb_p = _qk_prep(zp, tabs_p, qg, kg, ones_q, ones_k)
    o_p = _attn_prompt(sinks, q_p, kb_p, vb_p, B, S)
    y_p = _token_pipeline_tail(xp, zp, yr_p, o_p, w, p_prompt[l].reshape(B * S, D_PLE))

    zp3 = zp.reshape(B, S, D_IN)
    keep = min(WINDOW, S)
    prompt_conv = zp3[:, S - (CONV_W - 1):, OFF_XR:OFF_XR + D_RNN]
    prompt_k = kf_p.reshape(B, S, N_KV_HEADS, HEAD_DIM)[:, S - keep:]
    prompt_v = zp3[:, S - keep:, OFF_V:OFF_V + D_KV].reshape(B, keep, N_KV_HEADS, HEAD_DIM)

    ns = DB * DS
    xs = x_sample.reshape(ns, D_MODEL)
    zs = _in_proj(xs, g_mix, w_in_b)
    zs3 = zs.reshape(DB, DS, D_IN)
    tmaj = lambda a: jnp.transpose(a, (1, 0, 2))
    yr_s_t, h_s = _rnn_sample(tmaj(zs3[:, :, OFF_XR:OFF_XR + D_RNN]), tmaj(zs3[:, :, OFF_GR:OFF_GR + D_RNN]),
                              tmaj(state_conv[l]), state_rglru[l], cw, cb, wr, br, wi, bi, lam)
    yr_s = tmaj(yr_s_t).reshape(ns, D_RNN)
    pos_s = PAST_LEN + jnp.arange(DS, dtype=jnp.int32)
    tabs_s = tuple(jnp.tile(a, (DB, 1)) for a in _rope_tables(pos_s))
    q_s, kf_s, kb_s, vb_s = _qk_prep(zs, tabs_s, qg, kg, ones_q, ones_k)
    tq = BF16_TILE_ROWS
    pad_t = lambda a: jnp.pad(a.reshape(DB, DS, -1), ((0, 0), (0, tq - DS), (0, 0)))
    ck = cache_k[l].reshape(DB, wbuf, D_KV)
    cv = cache_v[l].reshape(DB, wbuf, D_KV)
    o_s = _attn_sample(sinks, pad_t(q_s), pad_t(kb_s), pad_t(vb_s), ck, cv)[:, :DS].reshape(ns, D_Q)
    y_s = _token_pipeline_tail(xs, zs, yr_s, o_s, w, p_sample[l].reshape(ns, D_PLE))

    sample_conv = jnp.concatenate([state_conv[l], zs3[:, :, OFF_XR:OFF_XR + D_RNN]], axis=1)[:, DS:]
    k_new = kf_s.reshape(DB, DS, D_KV)
    v_new = zs3[:, :, OFF_V:OFF_V + D_KV]
    sample_k = jnp.concatenate([ck, k_new], axis=1)[:, DS:].reshape(DB, wbuf, N_KV_HEADS, HEAD_DIM)
    sample_v = jnp.concatenate([cv, v_new], axis=1)[:, DS:].reshape(DB, wbuf, N_KV_HEADS, HEAD_DIM)

    return (y_p.reshape(B, S, D_MODEL), y_s.reshape(DB, DS, D_MODEL),
            prompt_conv[None], h_p.reshape(1, B, D_RNN), prompt_k[None], prompt_v[None],
            sample_conv[None], h_s[None], sample_k[None], sample_v[None])
```

```python
import functools

import jax
import jax.numpy as jnp
import numpy as np
from jax import lax
from jax.experimental import pallas as pl
from jax.experimental.pallas import tpu as pltpu

D_MODEL = 2048
D_RNN = D_MODEL // 2
RNN_BLOCKS = 8
RNN_BLOCK = D_RNN // RNN_BLOCKS
CONV_W = 4
LRU_C = 8.0
HEAD_DIM = 64
N_Q_HEADS = D_MODEL // 2 // HEAD_DIM
N_KV_HEADS = 4
GROUP = N_Q_HEADS // N_KV_HEADS
D_Q = N_Q_HEADS * HEAD_DIM
D_KV = N_KV_HEADS * HEAD_DIM
WINDOW = 128
ROT_DIM = HEAD_DIM // 4
ROPE_THETA = 500000.0
N_KEYS = 128
N_EXPERTS = N_KEYS * N_KEYS
PEER_HEADS = 8
PEER_TOPK = 16
D_KEY = 256
D_HALF_KEY = D_KEY // 2
N_PICKS = PEER_HEADS * PEER_TOPK
D_PLE = 256
EPS = 1e-6
NEG_INF = -1e30
PAST_LEN = 16384

LANES = 128
SUBLANES = 8
BF16_TILE_ROWS = 16
VMEM_LIMIT_BYTES = 56 * 1024 * 1024

OFF_GA = 0
OFF_GB = OFF_GA + D_MODEL
OFF_XR = OFF_GB + D_MODEL
OFF_GR = OFF_XR + D_RNN
OFF_Q = OFF_GR + D_RNN
OFF_K = OFF_Q + D_Q
OFF_V = OFF_K + D_KV
D_IN = OFF_V + D_KV

BF16 = jnp.bfloat16
F32 = jnp.float32


def _params(*sem):
    return pltpu.CompilerParams(dimension_semantics=sem, vmem_limit_bytes=VMEM_LIMIT_BYTES)


def _rmsnorm_rows(x, g):
    return x * lax.rsqrt(jnp.mean(x * x, axis=-1, keepdims=True) + EPS) * g


def _dot(a, b):
    return jnp.dot(a, b, preferred_element_type=F32)


def _dot_nt(a, b):
    return lax.dot_general(a, b, (((1,), (1,)), ((), ())), preferred_element_type=F32)


def _in_proj_kernel(x_ref, g_ref, w_ref, z_ref, xn_ref):
    @pl.when(pl.program_id(1) == 0)
    def _():
        xn_ref[...] = _rmsnorm_rows(x_ref[...], g_ref[...]).astype(BF16)

    z_ref[...] = _dot(xn_ref[...], w_ref[...])


def _in_proj(x, g, w, tm=512, tn=1536):
    n = x.shape[0]
    tm = min(tm, n)
    return pl.pallas_call(
        _in_proj_kernel,
        out_shape=jax.ShapeDtypeStruct((n, D_IN), F32),
        grid=(n // tm, D_IN // tn),
        in_specs=[
            pl.BlockSpec((tm, D_MODEL), lambda i, j: (i, 0)),
            pl.BlockSpec((1, D_MODEL), lambda i, j: (0, 0)),
            pl.BlockSpec((D_MODEL, tn), lambda i, j: (0, j)),
        ],
        out_specs=pl.BlockSpec((tm, tn), lambda i, j: (i, j)),
        scratch_shapes=[pltpu.VMEM((tm, D_MODEL), BF16)],
        compiler_params=_params("parallel", "arbitrary"),
        name="in_proj",
    )(x, g, w)


def _softplus(x):
    return jnp.maximum(x, 0.0) + jnp.log(1.0 + jnp.exp(-jnp.abs(x)))


def _neg_expm1(x):
    t = jnp.tanh(-0.5 * x)
    return 2.0 * t / (1.0 + t)


def _lru_coeffs(xc, wr, br, wi, bi, lam, first_is_pos0):
    xb = xc.astype(BF16)
    r = jax.nn.sigmoid(_dot(xb, wr) + br)
    i = jax.nn.sigmoid(_dot(xb, wi) + bi)
    log_a = -LRU_C * r * _softplus(-lam)
    a = jnp.exp(log_a)
    mult = jnp.sqrt(_neg_expm1(2.0 * log_a))
    if first_is_pos0 is not None:
        row = lax.broadcasted_iota(jnp.int32, xc.shape, 0)
        mult = jnp.where(jnp.logical_and(first_is_pos0, row == 0), 1.0, mult)
    return a, mult * i * xc


def _rnn_prompt_kernel(xr_ref, gr_ref, cw_ref, cb_ref, wr_ref, br_ref, wi_ref, bi_ref, lam_ref,
                       yr_ref, hlast_ref, tail_ref, h_ref, *, tb):
    t = pl.program_id(1)

    @pl.when(t == 0)
    def _():
        tail_ref[...] = jnp.zeros_like(tail_ref)
        h_ref[...] = jnp.zeros_like(h_ref)

    x = xr_ref[...]
    tail = tail_ref[...]
    row8 = lax.broadcasted_iota(jnp.int32, (SUBLANES, D_RNN), 0)
    cw = cw_ref[...]
    xc = cb_ref[...] + cw[CONV_W - 1:CONV_W, :] * x
    for k in range(1, CONV_W):
        xs = pltpu.roll(x, k, axis=0)
        head = jnp.where(row8 < k, pltpu.roll(tail, k, axis=0), xs[:SUBLANES])
        xs = jnp.concatenate([head, xs[SUBLANES:]], axis=0)
        xc = xc + cw[CONV_W - 1 - k:CONV_W - k, :] * xs
    tail_ref[...] = x[tb - SUBLANES:, :]

    a, b = _lru_coeffs(xc, wr_ref[...], br_ref[...], wi_ref[...], bi_ref[...], lam_ref[...], t == 0)

    row = lax.broadcasted_iota(jnp.int32, (tb, D_RNN), 0)
    d = 1
    while d < tb:
        a_sh = pltpu.roll(a, d, axis=0)
        b_sh = pltpu.roll(b, d, axis=0)
        keep = row < d
        b = jnp.where(keep, b, a * b_sh + b)
        a = jnp.where(keep, a, a * a_sh)
        d *= 2
    h = b + a * h_ref[...]
    h_ref[...] = h[tb - 1:tb, :]
    hlast_ref[0] = h[tb - 1:tb, :]
    yr_ref[...] = (h * jax.nn.gelu(gr_ref[...])).astype(BF16)


def _rnn_prompt(z, batch, seq, cw, cb, wr, br, wi, bi, lam, tb=256):
    nt = seq // tb
    vec = pl.BlockSpec((1, D_RNN), lambda b, t: (0, 0))
    mat = pl.BlockSpec((D_RNN, D_RNN), lambda b, t: (0, 0))
    return pl.pallas_call(
        functools.partial(_rnn_prompt_kernel, tb=tb),
        out_shape=(jax.ShapeDtypeStruct((batch * seq, D_RNN), BF16),
                   jax.ShapeDtypeStruct((batch, 1, D_RNN), F32)),
        grid=(batch, nt),
        in_specs=[
            pl.BlockSpec((tb, D_RNN), lambda b, t: (b * nt + t, OFF_XR // D_RNN)),
            pl.BlockSpec((tb, D_RNN), lambda b, t: (b * nt + t, OFF_GR // D_RNN)),
            pl.BlockSpec((CONV_W, D_RNN), lambda b, t: (0, 0)),
            vec, mat, vec, mat, vec, vec,
        ],
        out_specs=(pl.BlockSpec((tb, D_RNN), lambda b, t: (b * nt + t, 0)),
                   pl.BlockSpec((1, 1, D_RNN), lambda b, t: (b, 0, 0))),
        scratch_shapes=[pltpu.VMEM((SUBLANES, D_RNN), F32), pltpu.VMEM((1, D_RNN), F32)],
        compiler_params=_params("parallel", "arbitrary"),
        name="rnn_prompt",
    )(z, z, cw, cb, wr, br, wi, bi, lam)


def _rnn_sample_kernel(xr_ref, gr_ref, buf_ref, h0_ref, cw_ref, cb_ref, wr_ref, br_ref, wi_ref,
                       bi_ref, lam_ref, yr_ref, hlast_ref, *, steps):
    cw = cw_ref[...]
    xp = [buf_ref[k] for k in range(CONV_W - 1)] + [xr_ref[s] for s in range(steps)]
    h = h0_ref[...]
    for s in range(steps):
        xc = cb_ref[...] + sum(cw[k:k + 1, :] * xp[s + k] for k in range(CONV_W))
        a, b = _lru_coeffs(xc, wr_ref[...], br_ref[...], wi_ref[...], bi_ref[...], lam_ref[...], None)
        h = a * h + b
        yr_ref[s] = (h * jax.nn.gelu(gr_ref[s])).astype(BF16)
    hlast_ref[...] = h


def _rnn_sample(xr_t, gr_t, buf_t, h0, cw, cb, wr, br, wi, bi, lam):
    steps, db, _ = xr_t.shape
    return pl.pallas_call(
        functools.partial(_rnn_sample_kernel, steps=steps),
        out_shape=(jax.ShapeDtypeStruct((steps, db, D_RNN), BF16),
                   jax.ShapeDtypeStruct((db, D_RNN), F32)),
        compiler_params=pltpu.CompilerParams(vmem_limit_bytes=VMEM_LIMIT_BYTES),
        name="rnn_sample",
    )(xr_t, gr_t, buf_t, h0, cw, cb, wr, br, wi, bi, lam)


def _headnorm_rope(x, gain, seg_ones, cos_t, sin_up_t, sin_dn_t, scale):
    w = x.shape[1]
    sq = x * x
    hi = sq.astype(BF16)
    lo = (sq - hi.astype(F32)).astype(BF16)
    ms = (_dot(hi, seg_ones) + _dot(lo, seg_ones)) * (1.0 / HEAD_DIM)
    xn = x * lax.rsqrt(ms + EPS)
    outs = []
    for c in range(w // LANES):
        xt = xn[:, c * LANES:(c + 1) * LANES] * gain
        up = pltpu.roll(xt, LANES - ROT_DIM // 2, axis=1)
        dn = pltpu.roll(xt, ROT_DIM // 2, axis=1)
        outs.append((xt * cos_t + up * sin_up_t + dn * sin_dn_t) * scale)
    return jnp.concatenate(outs, axis=1)


def _qk_prep_kernel(q_ref, k_ref, v_ref, cos_ref, sup_ref, sdn_ref, qg_ref, kg_ref, oq_ref, ok_ref,
                    qo_ref, kf_ref, kb_ref, vb_ref):
    cos_t, sup, sdn = cos_ref[...], sup_ref[...], sdn_ref[...]
    q = _headnorm_rope(q_ref[...], qg_ref[...], oq_ref[...], cos_t, sup, sdn, HEAD_DIM ** -0.5)
    qo_ref[...] = q.astype(BF16)
    k = _headnorm_rope(k_ref[...], kg_ref[...], ok_ref[...], cos_t, sup, sdn, 1.0)
    kf_ref[...] = k
    kb_ref[...] = k.astype(BF16)
    vb_ref[...] = v_ref[...].astype(BF16)


def _qk_prep(z, tabs, qg, kg, ones_q, ones_k, tm=512):
    n = z.shape[0]
    tm = min(tm, n)
    cos_t, sup_t, sdn_t = tabs
    ntab = cos_t.shape[0] // tm
    tab = pl.BlockSpec((tm, LANES), lambda i: (i % ntab, 0))
    const = lambda shape: pl.BlockSpec(shape, lambda i: (0, 0))
    return pl.pallas_call(
        _qk_prep_kernel,
        out_shape=(jax.ShapeDtypeStruct((n, D_Q), BF16), jax.ShapeDtypeStruct((n, D_KV), F32),
                   jax.ShapeDtypeStruct((n, D_KV), BF16), jax.ShapeDtypeStruct((n, D_KV), BF16)),
        grid=(n // tm,),
        in_specs=[
            pl.BlockSpec((tm, D_Q), lambda i: (i, OFF_Q // D_Q)),
            pl.BlockSpec((tm, D_KV), lambda i: (i, OFF_K // D_KV)),
            pl.BlockSpec((tm, D_KV), lambda i: (i, OFF_V // D_KV)),
            tab, tab, tab,
            const((1, LANES)), const((1, LANES)), const((D_Q, D_Q)), const((D_KV, D_KV)),
        ],
        out_specs=(pl.BlockSpec((tm, D_Q), lambda i: (i, 0)), pl.BlockSpec((tm, D_KV), lambda i: (i, 0)),
                   pl.BlockSpec((tm, D_KV), lambda i: (i, 0)), pl.BlockSpec((tm, D_KV), lambda i: (i, 0))),
        compiler_params=_params("parallel"),
        name="qk_prep",
    )(z, z, z, cos_t, sup_t, sdn_t, qg, kg, ones_q, ones_k)


def _sink_attention(q, k, v, valid, sink_ref, tq):
    lane_head = lax.broadcasted_iota(jnp.int32, (tq, D_KV), 1) // HEAD_DIM
    rowg = lax.broadcasted_iota(jnp.int32, (GROUP * tq, 1), 0) // tq
    validg = jnp.concatenate([valid] * GROUP, axis=0)
    out = [jnp.zeros((tq, D_KV), F32) for _ in range(GROUP)]
    for j in range(N_KV_HEADS):
        sel = lane_head == j
        keep = jnp.where(sel, 1.0, 0.0).astype(BF16)
        qs = jnp.concatenate([q[:, g * D_KV:(g + 1) * D_KV] * keep for g in range(GROUP)], axis=0)
        s = jnp.where(validg, _dot_nt(qs, k), NEG_INF)
        sk = jnp.zeros((GROUP * tq, 1), F32)
        for g in range(GROUP):
            sk = jnp.where(rowg == g, sink_ref[j * GROUP + g], sk)
        m = jnp.maximum(jnp.max(s, axis=-1, keepdims=True), sk)
        p = jnp.exp(s - m)
        denom = jnp.sum(p, axis=-1, keepdims=True) + jnp.exp(sk - m)
        pv = _dot(p.astype(BF16), v) / denom
        for g in range(GROUP):
            out[g] = jnp.where(sel, pv[g * tq:(g + 1) * tq], out[g])
    return jnp.concatenate(out, axis=1)


def _attn_prompt_kernel(sink_ref, q_ref, kp_ref, kc_ref, vp_ref, vc_ref, o_ref):
    nb = pl.program_id(1)
    k = jnp.concatenate([kp_ref[...], kc_ref[...]], axis=0)
    v = jnp.concatenate([vp_ref[...], vc_ref[...]], axis=0)
    i = lax.broadcasted_iota(jnp.int32, (WINDOW, 2 * WINDOW), 0)
    j = lax.broadcasted_iota(jnp.int32, (WINDOW, 2 * WINDOW), 1)
    d = WINDOW + i - j
    valid = (d >= 0) & (d < WINDOW) & ((j >= WINDOW) | (nb > 0))
    o_ref[...] = _sink_attention(q_ref[...], k, v, valid, sink_ref, WINDOW).astype(BF16)


def _attn_prompt(sinks, q, kb, vb, batch, seq):
    nblk = seq // WINDOW
    cur = lambda w: pl.BlockSpec((WINDOW, w), lambda b, t: (b * nblk + t, 0))
    prev = lambda w: pl.BlockSpec((WINDOW, w), lambda b, t: (b * nblk + jnp.maximum(t - 1, 0), 0))
    return pl.pallas_call(
        _attn_prompt_kernel,
        out_shape=jax.ShapeDtypeStruct((batch * seq, D_Q), BF16),
        grid=(batch, nblk),
        in_specs=[pl.BlockSpec(memory_space=pltpu.SMEM), cur(D_Q), prev(D_KV), cur(D_KV),
                  prev(D_KV), cur(D_KV)],
        out_specs=cur(D_Q),
        compiler_params=_params("parallel", "arbitrary"),
        name="attn_prompt",
    )(sinks, q, kb, kb, vb, vb)


def _attn_sample_kernel(sink_ref, q_ref, kn_ref, vn_ref, ck_ref, cv_ref, o_ref, *, tq, wbuf):
    k = jnp.concatenate([ck_ref[0].astype(BF16), kn_ref[0]], axis=0)
    v = jnp.concatenate([cv_ref[0].astype(BF16), vn_ref[0]], axis=0)
    t = lax.broadcasted_iota(jnp.int32, (tq, wbuf + tq), 0)
    c = lax.broadcasted_iota(jnp.int32, (tq, wbuf + tq), 1)
    d = wbuf + t - c
    valid = (d >= 0) & (d < WINDOW)
    o_ref[0] = _sink_attention(q_ref[0], k, v, valid, sink_ref, tq).astype(BF16)


def _attn_sample(sinks, q3, kn3, vn3, cache_k, cache_v):
    db, tq, _ = q3.shape
    wbuf = cache_k.shape[1]
    blk = lambda r, w: pl.BlockSpec((1, r, w), lambda b: (b, 0, 0))
    return pl.pallas_call(
        functools.partial(_attn_sample_kernel, tq=tq, wbuf=wbuf),
        out_shape=jax.ShapeDtypeStruct((db, tq, D_Q), BF16),
        grid=(db,),
        in_specs=[pl.BlockSpec(memory_space=pltpu.SMEM), blk(tq, D_Q), blk(tq, D_KV), blk(tq, D_KV),
                  blk(wbuf, D_KV), blk(wbuf, D_KV)],
        out_specs=blk(tq, D_Q),
        compiler_params=_params("parallel"),
        name="attn_sample",
    )(sinks, q3, kn3, vn3, cache_k, cache_v)


def _merge_kernel(yr_ref, o_ref, ga_ref, gb_ref, wr_ref, wa_ref, m_ref):
    a = _dot(yr_ref[...], wr_ref[...])
    b = _dot(o_ref[...], wa_ref[...])
    m_ref[...] = (jax.nn.sigmoid(ga_ref[...]) * a + jax.nn.sigmoid(gb_ref[...]) * b).astype(BF16)


def _merge(yr, o, z, wr, wa, tm=512, tn=1024):
    n = yr.shape[0]
    tm = min(tm, n)
    nj = D_MODEL // tn
    return pl.pallas_call(
        _merge_kernel,
        out_shape=jax.ShapeDtypeStruct((n, D_MODEL), BF16),
        grid=(n // tm, nj),
        in_specs=[
            pl.BlockSpec((tm, D_RNN), lambda i, j: (i, 0)),
            pl.BlockSpec((tm, D_Q), lambda i, j: (i, 0)),
            pl.BlockSpec((tm, tn), lambda i, j: (i, OFF_GA // tn + j)),
            pl.BlockSpec((tm, tn), lambda i, j: (i, OFF_GB // tn + j)),
            pl.BlockSpec((D_RNN, tn), lambda i, j: (0, j)),
            pl.BlockSpec((D_Q, tn), lambda i, j: (0, j)),
        ],
        out_specs=pl.BlockSpec((tm, tn), lambda i, j: (i, j)),
        compiler_params=_params("parallel", "arbitrary"),
        name="merge",
    )(yr, o, z, z, wr, wa)


def _out_proj_kernel(x_ref, m_ref, w_ref, g_ref, x1_ref, n2_ref):
    x1 = x_ref[...] + _dot(m_ref[...], w_ref[...])
    x1_ref[...] = x1
    n2_ref[...] = _rmsnorm_rows(x1, g_ref[...]).astype(BF16)


def _out_proj(x, m, w, g, tm=512):
    n = x.shape[0]
    tm = min(tm, n)
    row = lambda dt: pl.BlockSpec((tm, D_MODEL), lambda i: (i, 0))
    return pl.pallas_call(
        _out_proj_kernel,
        out_shape=(jax.ShapeDtypeStruct((n, D_MODEL), F32), jax.ShapeDtypeStruct((n, D_MODEL), BF16)),
        grid=(n // tm,),
        in_specs=[row(F32), row(BF16), pl.BlockSpec((D_MODEL, D_MODEL), lambda i: (0, 0)),
                  pl.BlockSpec((1, D_MODEL), lambda i: (0, 0))],
        out_specs=(row(F32), row(BF16)),
        compiler_params=_params("parallel"),
        name="out_proj",
    )(x, m, w, g)


def _topk_rows(s, k):
    rows, t = s.shape
    rid = lax.broadcasted_iota(jnp.int32, (rows, t), 0).astype(F32)
    slot = lax.broadcasted_iota(jnp.int32, (k, t), 0)
    vals = jnp.zeros((k, t), F32)
    ids = jnp.zeros((k, t), F32)
    for r in range(k):
        m = jnp.max(s, axis=0, keepdims=True)
        i = jnp.min(jnp.where(s == m, rid, float(rows)), axis=0, keepdims=True)
        vals = jnp.where(slot == r, m, vals)
        ids = jnp.where(slot == r, i, ids)
        s = jnp.where(rid == i, -jnp.inf, s)
    return vals, ids


def _peer_route_kernel(n2_ref, wq_ref, sk_ref, idx_ref, gate_ref):
    q = _dot(n2_ref[...], wq_ref[...]).astype(BF16)
    for h in range(PEER_HEADS):
        top = []
        for p in range(2):
            c = (2 * h + p) * D_HALF_KEY
            s = _dot_nt(sk_ref[2 * h + p], q[:, c:c + D_HALF_KEY])
            top.append(_topk_rows(s, PEER_TOPK))
        (s1, i1), (s2, i2) = top
        cand = jnp.concatenate([s1[a:a + 1, :] + s2 for a in range(PEER_TOPK)], axis=0)
        cidx = jnp.concatenate([i1[a:a + 1, :] * float(N_KEYS) + i2 for a in range(PEER_TOPK)], axis=0)
        top_s, pos = _topk_rows(cand, PEER_TOPK)
        rid = lax.broadcasted_iota(jnp.int32, cand.shape, 0).astype(F32)
        slot = lax.broadcasted_iota(jnp.int32, top_s.shape, 0)
        ids = jnp.zeros(top_s.shape, F32)
        for r in range(PEER_TOPK):
            e = jnp.max(jnp.where(rid == pos[r:r + 1, :], cidx, -1.0), axis=0, keepdims=True)
            ids = jnp.where(slot == r, e, ids)
        w = jnp.exp(top_s - top_s[0:1, :])
        gate_ref[h * PEER_TOPK:(h + 1) * PEER_TOPK, :] = w / jnp.sum(w, axis=0, keepdims=True)
        idx_ref[h * PEER_TOPK:(h + 1) * PEER_TOPK, :] = ids.astype(jnp.int32)


def _peer_route(n2, wq, sk, tm=256):
    n = n2.shape[0]
    tm = min(tm, n)
    return pl.pallas_call(
        _peer_route_kernel,
        out_shape=(jax.ShapeDtypeStruct((N_PICKS, n), jnp.int32), jax.ShapeDtypeStruct((N_PICKS, n), F32)),
        grid=(n // tm,),
        in_specs=[pl.BlockSpec((tm, D_MODEL), lambda i: (i, 0)),
                  pl.BlockSpec((D_MODEL, PEER_HEADS * D_KEY), lambda i: (0, 0)),
                  pl.BlockSpec((2 * PEER_HEADS, N_KEYS, D_HALF_KEY), lambda i: (0, 0, 0))],
        out_specs=(pl.BlockSpec((N_PICKS, tm), lambda i: (0, i)), pl.BlockSpec((N_PICKS, tm), lambda i: (0, i))),
        compiler_params=_params("parallel"),
        name="peer_route",
    )(n2, wq, sk)


D_TILES = D_MODEL // LANES
assert D_TILES == BF16_TILE_ROWS


def _peer_mix_kernel(idx_ref, idxn_ref, x_ref, gate_ref, exp_ref, expt_ref, tab_ref, o_ref, buf_a, buf_b, d_ref,
                     sem_ref, *, tb, nsteps):
    i = pl.program_id(0)
    bufs = (buf_a, buf_b)

    def start_token(ids_ref, row, half, t):
        for j in range(N_PICKS):
            pltpu.make_async_copy(tab_ref.at[ids_ref[row, j]], bufs[half].at[t * N_PICKS + j],
                                  sem_ref.at[half, t]).start(priority=j % 2)

    def wait_token(half, t):
        pltpu.make_async_copy(tab_ref.at[pl.ds(0, N_PICKS)], bufs[half].at[pl.ds(t * N_PICKS, N_PICKS)],
                              sem_ref.at[half, t]).wait()

    @pl.when(i == 0)
    def _():
        for t in range(tb):
            start_token(idx_ref, t, 0, t)

    rows = N_PICKS * D_TILES
    diag = (lax.broadcasted_iota(jnp.int32, (D_TILES, rows), 1) % D_TILES
            == lax.broadcasted_iota(jnp.int32, (D_TILES, rows), 0))

    def run_tile(half, start_other):
        buf = bufs[half]
        for t in range(tb):
            start_other(t)
            wait_token(half, t)
            u = buf[t * N_PICKS:(t + 1) * N_PICKS, 0:D_TILES, :].reshape(rows, LANES)
            g = _dot_nt(x_ref[half * tb + t], u)
            d_ref[t:t + 1, :] = jnp.sum(jnp.where(diag, g, 0.0), axis=0, keepdims=True)
        d = d_ref[...]
        d_hi = d.astype(BF16)
        d_lo = (d - d_hi.astype(F32)).astype(BF16)
        h = _dot(d_hi, expt_ref[...]) + _dot(d_lo, expt_ref[...])
        w = (jax.nn.gelu(h) * gate_ref[half * tb:(half + 1) * tb, :]).astype(BF16)
        wrow = _dot(w, exp_ref[...])
        for t in range(tb):
            v = buf[t * N_PICKS:(t + 1) * N_PICKS, D_TILES:2 * D_TILES, :].reshape(rows, LANES)
            wm = jnp.where(diag, wrow[t:t + 1, :], 0.0).astype(BF16)
            o_ref[half * tb + t] = _dot(wm, v)

    run_tile(0, lambda t: start_token(idx_ref, tb + t, 1, t))
    run_tile(1, lambda t: start_token(idxn_ref, t, 0, t))

    @pl.when(i == nsteps - 1)
    def _():
        for t in range(tb):
            wait_token(0, t)


def _peer_mix(idx, gates, n2, table, expand, expand_t, tb=BF16_TILE_ROWS):
    n = idx.shape[0]
    nsteps = n // (2 * tb)
    x3 = n2.reshape(n, D_TILES, LANES)
    rows = N_PICKS * D_TILES
    out = pl.pallas_call(
        functools.partial(_peer_mix_kernel, tb=tb, nsteps=nsteps),
        out_shape=jax.ShapeDtypeStruct((n, D_TILES, LANES), F32),
        grid=(nsteps,),
        in_specs=[
            pl.BlockSpec((2 * tb, N_PICKS), lambda i: (i, 0), memory_space=pltpu.SMEM),
            pl.BlockSpec((tb, N_PICKS), lambda i: (jnp.minimum(2 * i + 2, 2 * nsteps - 2), 0),
                         memory_space=pltpu.SMEM),
            pl.BlockSpec((2 * tb, D_TILES, LANES), lambda i: (i, 0, 0)),
            pl.BlockSpec((2 * tb, N_PICKS), lambda i: (i, 0)),
            pl.BlockSpec((N_PICKS, rows), lambda i: (0, 0)),
            pl.BlockSpec((rows, N_PICKS), lambda i: (0, 0)),
            pl.BlockSpec(memory_space=pl.ANY),
        ],
        out_specs=pl.BlockSpec((2 * tb, D_TILES, LANES), lambda i: (i, 0, 0)),
        scratch_shapes=[pltpu.VMEM((tb * N_PICKS, 2 * D_TILES, LANES), BF16),
                        pltpu.VMEM((tb * N_PICKS, 2 * D_TILES, LANES), BF16),
                        pltpu.VMEM((tb, rows), F32),
                        pltpu.SemaphoreType.DMA((2, tb))],
        compiler_params=_params("arbitrary"),
        name="peer_mix",
    )(idx, idx, x3, gates, expand, expand_t, table)
    return out.reshape(n, D_MODEL)


def _ple_kernel(x1_ref, po_ref, ple_ref, g_ref, wp_ref, wg_ref, y_ref):
    x2 = x1_ref[...] + po_ref[...]
    n3 = _rmsnorm_rows(x2, g_ref[...]).astype(BF16)
    emb = _dot(ple_ref[...].astype(BF16), wp_ref[...])
    y_ref[...] = x2 + emb * jax.nn.sigmoid(_dot(n3, wg_ref[...]))


def _ple(x1, po, ple, g, wp, wg, tm=512):
    n = x1.shape[0]
    tm = min(tm, n)
    row = pl.BlockSpec((tm, D_MODEL), lambda i: (i, 0))
    return pl.pallas_call(
        _ple_kernel,
        out_shape=jax.ShapeDtypeStruct((n, D_MODEL), F32),
        grid=(n // tm,),
        in_specs=[row, row, pl.BlockSpec((tm, D_PLE), lambda i: (i, 0)),
                  pl.BlockSpec((1, D_MODEL), lambda i: (0, 0)),
                  pl.BlockSpec((D_PLE, D_MODEL), lambda i: (0, 0)),
                  pl.BlockSpec((D_MODEL, D_MODEL), lambda i: (0, 0))],
        out_specs=row,
        compiler_params=_params("parallel"),
        name="ple",
    )(x1, po, ple, g, wp, wg)


def _rope_tables(pos):
    half = ROT_DIM // 2
    inv = ROPE_THETA ** (-jnp.arange(0, ROT_DIM, 2, dtype=F32) / ROT_DIM)
    ang = pos.astype(F32)[:, None] * inv[None, :]
    cos, sin = jnp.cos(ang), jnp.sin(ang)
    n = pos.shape[0]
    pad = jnp.zeros((n, HEAD_DIM - ROT_DIM), F32)
    zh = jnp.zeros((n, half), F32)
    cos_h = jnp.concatenate([cos, cos, pad + 1.0], axis=1)
    up_h = jnp.concatenate([-sin, zh, pad], axis=1)
    dn_h = jnp.concatenate([zh, sin, pad], axis=1)
    rep = LANES // HEAD_DIM
    return tuple(jnp.tile(a, (1, rep)) for a in (cos_h, up_h, dn_h))


def _block_diag(w):
    eye = jnp.eye(RNN_BLOCKS, dtype=w.dtype)
    return jnp.einsum("ncd,nm->ncmd", w, eye).reshape(D_RNN, D_RNN)


def _head_perm():
    return np.array([j * GROUP + g for g in range(GROUP) for j in range(N_KV_HEADS)])


def _token_pipeline_tail(x, z, yr, o, w, ple):
    m = _merge(yr, o, z, w["proj_rnn"], w["proj_attn"])
    x1, n2 = _out_proj(x, m, w["out"], w["norm_ffn"])
    idx_t, gate_t = _peer_route(n2, w["peer_q"], w["sub_keys"])
    po = _peer_mix(idx_t.T, gate_t.T, n2, w["peer_table"], w["expand"], w["expand_t"])
    return _ple(x1, po, ple, w["norm_ple"], w["ple"], w["ple_gate"])


def kernel(x_prompt, x_sample, p_prompt, p_sample, state_conv, state_rglru, cache_k, cache_v, norm_mix, w_in, conv_w, conv_b, w_rgate, b_rgate, w_igate, b_igate, lru_lambda, w_proj_rnn, q_norm, k_norm, attn_sinks, w_proj_attn, w_out, norm_ffn, w_peer_q, peer_sub_keys, peer_u, peer_v, w_ple, norm_ple, w_ple_gate):
    depth = w_in.shape[0]
    assert depth == 1
    l = 0
    B, S, _ = x_prompt.shape
    DB, DS, _ = x_sample.shape
    wbuf = cache_k.shape[2]

    hp = _head_perm()
    offs = np.cumsum([0, D_RNN, D_RNN, D_Q, D_KV, D_KV, D_MODEL, D_MODEL])
    xr_c, gr_c, q_c, k_c, v_c, ga_c, gb_c = [np.arange(offs[i], offs[i + 1]) for i in range(7)]
    q_c = q_c.reshape(N_Q_HEADS, HEAD_DIM)[hp].reshape(-1)
    cols = np.concatenate([ga_c, gb_c, xr_c, gr_c, q_c, k_c, v_c])
    row2 = lambda a: a[l].reshape(1, -1)
    w = {
        "proj_rnn": w_proj_rnn[l].astype(BF16),
        "proj_attn": w_proj_attn[l].reshape(N_Q_HEADS, HEAD_DIM, D_MODEL)[hp].reshape(D_Q, D_MODEL).astype(BF16),
        "out": w_out[l].astype(BF16),
        "norm_ffn": row2(norm_ffn),
        "peer_q": w_peer_q[l].astype(BF16),
        "sub_keys": peer_sub_keys[l].reshape(2 * PEER_HEADS, N_KEYS, D_HALF_KEY).astype(BF16),
        "peer_table": jnp.concatenate(
            [peer_u[l].astype(BF16).reshape(N_EXPERTS, D_TILES, LANES),
             peer_v[l].astype(BF16).reshape(N_EXPERTS, D_TILES, LANES)], axis=1),
        "norm_ple": row2(norm_ple),
        "ple": w_ple[l].astype(BF16),
        "ple_gate": w_ple_gate[l].astype(BF16),
    }
    expand = np.repeat(np.eye(N_PICKS, dtype=np.float32), D_TILES, axis=1)
    w["expand"] = jnp.asarray(expand, BF16)
    w["expand_t"] = jnp.asarray(expand.T, BF16)
    w_in_b = w_in[l][:, cols].astype(BF16)
    g_mix = row2(norm_mix)
    cw, cb = conv_w[l], row2(conv_b)
    wr, br = _block_diag(w_rgate[l]).astype(BF16), row2(b_rgate)
    wi, bi = _block_diag(w_igate[l]).astype(BF16), row2(b_igate)
    lam = row2(lru_lambda)
    rep = LANES // HEAD_DIM
    qg = jnp.tile(q_norm[l], rep).reshape(1, LANES)
    kg = jnp.tile(k_norm[l], rep).reshape(1, LANES)
    seg = lambda width: jnp.asarray(
        np.kron(np.eye(width // HEAD_DIM, dtype=np.float32), np.ones((HEAD_DIM, HEAD_DIM), np.float32)), BF16)
    ones_q, ones_k = seg(D_Q), seg(D_KV)
    sinks = attn_sinks[l]

    xp = x_prompt.reshape(B * S, D_MODEL)
    zp = _in_proj(xp, g_mix, w_in_b)
    yr_p, h_p = _rnn_prompt(zp, B, S, cw, cb, wr, br, wi, bi, lam)
    tabs_p = _rope_tables(jnp.arange(S, dtype=jnp.int32))
    q_p, kf_p, kb_p, vb_p = _qk_prep(zp, tabs_p, qg, kg, ones_q, ones_k)
    o_p = _attn_prompt(sinks, q_p, kb_p, vb_p, B, S)
    y_p = _token_pipeline_tail(xp, zp, yr_p, o_p, w, p_prompt[l].reshape(B * S, D_PLE))

    zp3 = zp.reshape(B, S, D_IN)
    keep = min(WINDOW, S)
    prompt_conv = zp3[:, S - (CONV_W - 1):, OFF_XR:OFF_XR + D_RNN]
    prompt_k = kf_p.reshape(B, S, N_KV_HEADS, HEAD_DIM)[:, S - keep:]
    prompt_v = zp3[:, S - keep:, OFF_V:OFF_V + D_KV].reshape(B, keep, N_KV_HEADS, HEAD_DIM)

    ns = DB * DS
    xs = x_sample.reshape(ns, D_MODEL)
    zs = _in_proj(xs, g_mix, w_in_b)
    zs3 = zs.reshape(DB, DS, D_IN)
    tmaj = lambda a: jnp.transpose(a, (1, 0, 2))
    yr_s_t, h_s = _rnn_sample(tmaj(zs3[:, :, OFF_XR:OFF_XR + D_RNN]), tmaj(zs3[:, :, OFF_GR:OFF_GR + D_RNN]),
                              tmaj(state_conv[l]), state_rglru[l], cw, cb, wr, br, wi, bi, lam)
    yr_s = tmaj(yr_s_t).reshape(ns, D_RNN)
    pos_s = PAST_LEN + jnp.arange(DS, dtype=jnp.int32)
    tabs_s = tuple(jnp.tile(a, (DB, 1)) for a in _rope_tables(pos_s))
    q_s, kf_s, kb_s, vb_s = _qk_prep(zs, tabs_s, qg, kg, ones_q, ones_k)
    tq = BF16_TILE_ROWS
    pad_t = lambda a: jnp.pad(a.reshape(DB, DS, -1), ((0, 0), (0, tq - DS), (0, 0)))
    ck = cache_k[l].reshape(DB, wbuf, D_KV)
    cv = cache_v[l].reshape(DB, wbuf, D_KV)
    o_s = _attn_sample(sinks, pad_t(q_s), pad_t(kb_s), pad_t(vb_s), ck, cv)[:, :DS].reshape(ns, D_Q)
    y_s = _token_pipeline_tail(xs, zs, yr_s, o_s, w, p_sample[l].reshape(ns, D_PLE))

    sample_conv = jnp.concatenate([state_conv[l], zs3[:, :, OFF_XR:OFF_XR + D_RNN]], axis=1)[:, DS:]
    k_new = kf_s.reshape(DB, DS, D_KV)
    v_new = zs3[:, :, OFF_V:OFF_V + D_KV]
    sample_k = jnp.concatenate([ck, k_new], axis=1)[:, DS:].reshape(DB, wbuf, N_KV_HEADS, HEAD_DIM)
    sample_v = jnp.concatenate([cv, v_new], axis=1)[:, DS:].reshape(DB, wbuf, N_KV_HEADS, HEAD_DIM)

    return (y_p.reshape(B, S, D_MODEL), y_s.reshape(DB, DS, D_MODEL),
            prompt_conv[None], h_p.reshape(1, B, D_RNN), prompt_k[None], prompt_v[None],
            sample_conv[None], h_s[None], sample_k[None], sample_v[None])
```

```python
import functools

import jax
import jax.numpy as jnp
import numpy as np
from jax import lax
from jax.experimental import pallas as pl
from jax.experimental.pallas import tpu as pltpu

D_MODEL = 2048
D_RNN = D_MODEL // 2
RNN_BLOCKS = 8
RNN_BLOCK = D_RNN // RNN_BLOCKS
CONV_W = 4
LRU_C = 8.0
HEAD_DIM = 64
N_Q_HEADS = D_MODEL // 2 // HEAD_DIM
N_KV_HEADS = 4
GROUP = N_Q_HEADS // N_KV_HEADS
D_Q = N_Q_HEADS * HEAD_DIM
D_KV = N_KV_HEADS * HEAD_DIM
WINDOW = 128
ROT_DIM = HEAD_DIM // 4
ROPE_THETA = 500000.0
N_KEYS = 128
N_EXPERTS = N_KEYS * N_KEYS
PEER_HEADS = 8
PEER_TOPK = 16
D_KEY = 256
D_HALF_KEY = D_KEY // 2
N_PICKS = PEER_HEADS * PEER_TOPK
D_PLE = 256
EPS = 1e-6
NEG_INF = -1e30
PAST_LEN = 16384

LANES = 128
SUBLANES = 8
BF16_TILE_ROWS = 16
VMEM_LIMIT_BYTES = 56 * 1024 * 1024

OFF_GA = 0
OFF_GB = OFF_GA + D_MODEL
OFF_XR = OFF_GB + D_MODEL
OFF_GR = OFF_XR + D_RNN
OFF_Q = OFF_GR + D_RNN
OFF_K = OFF_Q + D_Q
OFF_V = OFF_K + D_KV
D_IN = OFF_V + D_KV

BF16 = jnp.bfloat16
F32 = jnp.float32


def _params(*sem):
    return pltpu.CompilerParams(dimension_semantics=sem, vmem_limit_bytes=VMEM_LIMIT_BYTES)


def _rmsnorm_rows(x, g):
    return x * lax.rsqrt(jnp.mean(x * x, axis=-1, keepdims=True) + EPS) * g


def _dot(a, b):
    return jnp.dot(a, b, preferred_element_type=F32)


def _dot_nt(a, b):
    return lax.dot_general(a, b, (((1,), (1,)), ((), ())), preferred_element_type=F32)


def _in_proj_kernel(x_ref, g_ref, w_ref, z_ref, xn_ref):
    @pl.when(pl.program_id(1) == 0)
    def _():
        xn_ref[...] = _rmsnorm_rows(x_ref[...], g_ref[...]).astype(BF16)

    z_ref[...] = _dot(xn_ref[...], w_ref[...])


def _in_proj(x, g, w, tm=512, tn=1536):
    n = x.shape[0]
    tm = min(tm, n)
    return pl.pallas_call(
        _in_proj_kernel,
        out_shape=jax.ShapeDtypeStruct((n, D_IN), F32),
        grid=(n // tm, D_IN // tn),
        in_specs=[
            pl.BlockSpec((tm, D_MODEL), lambda i, j: (i, 0)),
            pl.BlockSpec((1, D_MODEL), lambda i, j: (0, 0)),
            pl.BlockSpec((D_MODEL, tn), lambda i, j: (0, j)),
        ],
        out_specs=pl.BlockSpec((tm, tn), lambda i, j: (i, j)),
        scratch_shapes=[pltpu.VMEM((tm, D_MODEL), BF16)],
        compiler_params=_params("parallel", "arbitrary"),
        name="in_proj",
    )(x, g, w)


def _softplus(x):
    return jnp.maximum(x, 0.0) + jnp.log(1.0 + jnp.exp(-jnp.abs(x)))


def _neg_expm1(x):
    t = jnp.tanh(-0.5 * x)
    return 2.0 * t / (1.0 + t)


def _lru_coeffs(xc, wr, br, wi, bi, lam, first_is_pos0):
    xb = xc.astype(BF16)
    r = jax.nn.sigmoid(_dot(xb, wr) + br)
    i = jax.nn.sigmoid(_dot(xb, wi) + bi)
    log_a = -LRU_C * r * _softplus(-lam)
    a = jnp.exp(log_a)
    mult = jnp.sqrt(_neg_expm1(2.0 * log_a))
    if first_is_pos0 is not None:
        row = lax.broadcasted_iota(jnp.int32, xc.shape, 0)
        mult = jnp.where(jnp.logical_and(first_is_pos0, row == 0), 1.0, mult)
    return a, mult * i * xc


def _rnn_prompt_kernel(xr_ref, gr_ref, cw_ref, cb_ref, wr_ref, br_ref, wi_ref, bi_ref, lam_ref,
                       yr_ref, hlast_ref, tail_ref, h_ref, *, tb):
    t = pl.program_id(1)

    @pl.when(t == 0)
    def _():
        tail_ref[...] = jnp.zeros_like(tail_ref)
        h_ref[...] = jnp.zeros_like(h_ref)

    x = xr_ref[...]
    tail = tail_ref[...]
    row8 = lax.broadcasted_iota(jnp.int32, (SUBLANES, D_RNN), 0)
    cw = cw_ref[...]
    xc = cb_ref[...] + cw[CONV_W - 1:CONV_W, :] * x
    for k in range(1, CONV_W):
        xs = pltpu.roll(x, k, axis=0)
        head = jnp.where(row8 < k, pltpu.roll(tail, k, axis=0), xs[:SUBLANES])
        xs = jnp.concatenate([head, xs[SUBLANES:]], axis=0)
        xc = xc + cw[CONV_W - 1 - k:CONV_W - k, :] * xs
    tail_ref[...] = x[tb - SUBLANES:, :]

    a, b = _lru_coeffs(xc, wr_ref[...], br_ref[...], wi_ref[...], bi_ref[...], lam_ref[...], t == 0)

    row = lax.broadcasted_iota(jnp.int32, (tb, D_RNN), 0)
    d = 1
    while d < tb:
        a_sh = pltpu.roll(a, d, axis=0)
        b_sh = pltpu.roll(b, d, axis=0)
        keep = row < d
        b = jnp.where(keep, b, a * b_sh + b)
        a = jnp.where(keep, a, a * a_sh)
        d *= 2
    h = b + a * h_ref[...]
    h_ref[...] = h[tb - 1:tb, :]
    hlast_ref[0] = h[tb - 1:tb, :]
    yr_ref[...] = (h * jax.nn.gelu(gr_ref[...])).astype(BF16)


def _rnn_prompt(z, batch, seq, cw, cb, wr, br, wi, bi, lam, tb=256):
    nt = seq // tb
    vec = pl.BlockSpec((1, D_RNN), lambda b, t: (0, 0))
    mat = pl.BlockSpec((D_RNN, D_RNN), lambda b, t: (0, 0))
    return pl.pallas_call(
        functools.partial(_rnn_prompt_kernel, tb=tb),
        out_shape=(jax.ShapeDtypeStruct((batch * seq, D_RNN), BF16),
                   jax.ShapeDtypeStruct((batch, 1, D_RNN), F32)),
        grid=(batch, nt),
        in_specs=[
            pl.BlockSpec((tb, D_RNN), lambda b, t: (b * nt + t, OFF_XR // D_RNN)),
            pl.BlockSpec((tb, D_RNN), lambda b, t: (b * nt + t, OFF_GR // D_RNN)),
            pl.BlockSpec((CONV_W, D_RNN), lambda b, t: (0, 0)),
            vec, mat, vec, mat, vec, vec,
        ],
        out_specs=(pl.BlockSpec((tb, D_RNN), lambda b, t: (b * nt + t, 0)),
                   pl.BlockSpec((1, 1, D_RNN), lambda b, t: (b, 0, 0))),
        scratch_shapes=[pltpu.VMEM((SUBLANES, D_RNN), F32), pltpu.VMEM((1, D_RNN), F32)],
        compiler_params=_params("parallel", "arbitrary"),
        name="rnn_prompt",
    )(z, z, cw, cb, wr, br, wi, bi, lam)


def _rnn_sample_kernel(xr_ref, gr_ref, buf_ref, h0_ref, cw_ref, cb_ref, wr_ref, br_ref, wi_ref,
                       bi_ref, lam_ref, yr_ref, hlast_ref, *, steps):
    cw = cw_ref[...]
    xp = [buf_ref[k] for k in range(CONV_W - 1)] + [xr_ref[s] for s in range(steps)]
    h = h0_ref[...]
    for s in range(steps):
        xc = cb_ref[...] + sum(cw[k:k + 1, :] * xp[s + k] for k in range(CONV_W))
        a, b = _lru_coeffs(xc, wr_ref[...], br_ref[...], wi_ref[...], bi_ref[...], lam_ref[...], None)
        h = a * h + b
        yr_ref[s] = (h * jax.nn.gelu(gr_ref[s])).astype(BF16)
    hlast_ref[...] = h


def _rnn_sample(xr_t, gr_t, buf_t, h0, cw, cb, wr, br, wi, bi, lam):
    steps, db, _ = xr_t.shape
    return pl.pallas_call(
        functools.partial(_rnn_sample_kernel, steps=steps),
        out_shape=(jax.ShapeDtypeStruct((steps, db, D_RNN), BF16),
                   jax.ShapeDtypeStruct((db, D_RNN), F32)),
        compiler_params=pltpu.CompilerParams(vmem_limit_bytes=VMEM_LIMIT_BYTES),
        name="rnn_sample",
    )(xr_t, gr_t, buf_t, h0, cw, cb, wr, br, wi, bi, lam)


def _headnorm_rope(x, gain, seg_ones, cos_t, sin_up_t, sin_dn_t, scale):
    w = x.shape[1]
    sq = x * x
    hi = sq.astype(BF16)
    lo = (sq - hi.astype(F32)).astype(BF16)
    ms = (_dot(hi, seg_ones) + _dot(lo, seg_ones)) * (1.0 / HEAD_DIM)
    xn = x * lax.rsqrt(ms + EPS)
    outs = []
    for c in range(w // LANES):
        xt = xn[:, c * LANES:(c + 1) * LANES] * gain
        up = pltpu.roll(xt, LANES - ROT_DIM // 2, axis=1)
        dn = pltpu.roll(xt, ROT_DIM // 2, axis=1)
        outs.append((xt * cos_t + up * sin_up_t + dn * sin_dn_t) * scale)
    return jnp.concatenate(outs, axis=1)


def _qk_prep_kernel(q_ref, k_ref, v_ref, cos_ref, sup_ref, sdn_ref, qg_ref, kg_ref, oq_ref, ok_ref,
                    qo_ref, kf_ref, kb_ref, vb_ref):
    cos_t, sup, sdn = cos_ref[...], sup_ref[...], sdn_ref[...]
    q = _headnorm_rope(q_ref[...], qg_ref[...], oq_ref[...], cos_t, sup, sdn, HEAD_DIM ** -0.5)
    qo_ref[...] = q.astype(BF16)
    k = _headnorm_rope(k_ref[...], kg_ref[...], ok_ref[...], cos_t, sup, sdn, 1.0)
    kf_ref[...] = k
    kb_ref[...] = k.astype(BF16)
    vb_ref[...] = v_ref[...].astype(BF16)


def _qk_prep(z, tabs, qg, kg, ones_q, ones_k, tm=512):
    n = z.shape[0]
    tm = min(tm, n)
    cos_t, sup_t, sdn_t = tabs
    ntab = cos_t.shape[0] // tm
    tab = pl.BlockSpec((tm, LANES), lambda i: (i % ntab, 0))
    const = lambda shape: pl.BlockSpec(shape, lambda i: (0, 0))
    return pl.pallas_call(
        _qk_prep_kernel,
        out_shape=(jax.ShapeDtypeStruct((n, D_Q), BF16), jax.ShapeDtypeStruct((n, D_KV), F32),
                   jax.ShapeDtypeStruct((n, D_KV), BF16), jax.ShapeDtypeStruct((n, D_KV), BF16)),
        grid=(n // tm,),
        in_specs=[
            pl.BlockSpec((tm, D_Q), lambda i: (i, OFF_Q // D_Q)),
            pl.BlockSpec((tm, D_KV), lambda i: (i, OFF_K // D_KV)),
            pl.BlockSpec((tm, D_KV), lambda i: (i, OFF_V // D_KV)),
            tab, tab, tab,
            const((1, LANES)), const((1, LANES)), const((D_Q, D_Q)), const((D_KV, D_KV)),
        ],
        out_specs=(pl.BlockSpec((tm, D_Q), lambda i: (i, 0)), pl.BlockSpec((tm, D_KV), lambda i: (i, 0)),
                   pl.BlockSpec((tm, D_KV), lambda i: (i, 0)), pl.BlockSpec((tm, D_KV), lambda i: (i, 0))),
        compiler_params=_params("parallel"),
        name="qk_prep",
    )(z, z, z, cos_t, sup_t, sdn_t, qg, kg, ones_q, ones_k)


def _sink_attention(q, k, v, valid, sink_ref, tq):
    lane_head = lax.broadcasted_iota(jnp.int32, (tq, D_KV), 1) // HEAD_DIM
    rowg = lax.broadcasted_iota(jnp.int32, (GROUP * tq, 1), 0) // tq
    validg = jnp.concatenate([valid] * GROUP, axis=0)
    out = [jnp.zeros((tq, D_KV), F32) for _ in range(GROUP)]
    for j in range(N_KV_HEADS):
        sel = lane_head == j
        keep = jnp.where(sel, 1.0, 0.0).astype(BF16)
        qs = jnp.concatenate([q[:, g * D_KV:(g + 1) * D_KV] * keep for g in range(GROUP)], axis=0)
        s = jnp.where(validg, _dot_nt(qs, k), NEG_INF)
        sk = jnp.zeros((GROUP * tq, 1), F32)
        for g in range(GROUP):
            sk = jnp.where(rowg == g, sink_ref[j * GROUP + g], sk)
        m = jnp.maximum(jnp.max(s, axis=-1, keepdims=True), sk)
        p = jnp.exp(s - m)
        denom = jnp.sum(p, axis=-1, keepdims=True) + jnp.exp(sk - m)
        pv = _dot(p.astype(BF16), v) / denom
        for g in range(GROUP):
            out[g] = jnp.where(sel, pv[g * tq:(g + 1) * tq], out[g])
    return jnp.concatenate(out, axis=1)


def _attn_prompt_kernel(sink_ref, q_ref, kp_ref, kc_ref, vp_ref, vc_ref, o_ref):
    nb = pl.program_id(1)
    k = jnp.concatenate([kp_ref[...], kc_ref[...]], axis=0)
    v = jnp.concatenate([vp_ref[...], vc_ref[...]], axis=0)
    i = lax.broadcasted_iota(jnp.int32, (WINDOW, 2 * WINDOW), 0)
    j = lax.broadcasted_iota(jnp.int32, (WINDOW, 2 * WINDOW), 1)
    d = WINDOW + i - j
    valid = (d >= 0) & (d < WINDOW) & ((j >= WINDOW) | (nb > 0))
    o_ref[...] = _sink_attention(q_ref[...], k, v, valid, sink_ref, WINDOW).astype(BF16)


def _attn_prompt(sinks, q, kb, vb, batch, seq):
    nblk = seq // WINDOW
    cur = lambda w: pl.BlockSpec((WINDOW, w), lambda b, t: (b * nblk + t, 0))
    prev = lambda w: pl.BlockSpec((WINDOW, w), lambda b, t: (b * nblk + jnp.maximum(t - 1, 0), 0))
    return pl.pallas_call(
        _attn_prompt_kernel,
        out_shape=jax.ShapeDtypeStruct((batch * seq, D_Q), BF16),
        grid=(batch, nblk),
        in_specs=[pl.BlockSpec(memory_space=pltpu.SMEM), cur(D_Q), prev(D_KV), cur(D_KV),
                  prev(D_KV), cur(D_KV)],
        out_specs=cur(D_Q),
        compiler_params=_params("parallel", "arbitrary"),
        name="attn_prompt",
    )(sinks, q, kb, kb, vb, vb)


def _shift_in(cache, new8, steps):
    wbuf = cache.shape[0]
    rolled = pltpu.roll(cache, wbuf - steps, axis=0)
    row8 = lax.broadcasted_iota(jnp.int32, new8.shape, 0)
    tail = jnp.where(row8 >= SUBLANES - steps, pltpu.roll(new8, SUBLANES - steps, axis=0),
                     rolled[wbuf - SUBLANES:])
    return jnp.concatenate([rolled[:wbuf - SUBLANES], tail], axis=0)


def _attn_sample_kernel(sink_ref, q_ref, kn_ref, vn_ref, knf_ref, vnf_ref, ck_ref, cv_ref, o_ref, ok_ref, ov_ref,
                        *, bs, tq, wbuf, steps):
    t = lax.broadcasted_iota(jnp.int32, (tq, wbuf + tq), 0)
    c = lax.broadcasted_iota(jnp.int32, (tq, wbuf + tq), 1)
    d = wbuf + t - c
    valid = (d >= 0) & (d < WINDOW)
    for s in range(bs):
        ck, cv = ck_ref[s], cv_ref[s]
        k = jnp.concatenate([ck.astype(BF16), kn_ref[s]], axis=0)
        v = jnp.concatenate([cv.astype(BF16), vn_ref[s]], axis=0)
        o_ref[s] = _sink_attention(q_ref[s], k, v, valid, sink_ref, tq).astype(BF16)
        ok_ref[s] = _shift_in(ck, knf_ref[s], steps)
        ov_ref[s] = _shift_in(cv, vnf_ref[s], steps)


def _attn_sample(sinks, q3, kn3, vn3, knf3, vnf3, cache_k, cache_v, steps, bs=8):
    db, tq, _ = q3.shape
    wbuf = cache_k.shape[1]
    assert steps <= SUBLANES and knf3.shape[1] == SUBLANES and db % bs == 0
    blk = lambda r, w: pl.BlockSpec((bs, r, w), lambda b: (b, 0, 0))
    cache = jax.ShapeDtypeStruct((db, wbuf, D_KV), F32)
    return pl.pallas_call(
        functools.partial(_attn_sample_kernel, bs=bs, tq=tq, wbuf=wbuf, steps=steps),
        out_shape=(jax.ShapeDtypeStruct((db, tq, D_Q), BF16), cache, cache),
        grid=(db // bs,),
        in_specs=[pl.BlockSpec(memory_space=pltpu.SMEM), blk(tq, D_Q), blk(tq, D_KV), blk(tq, D_KV),
                  blk(SUBLANES, D_KV), blk(SUBLANES, D_KV), blk(wbuf, D_KV), blk(wbuf, D_KV)],
        out_specs=(blk(tq, D_Q), blk(wbuf, D_KV), blk(wbuf, D_KV)),
        compiler_params=_params("parallel"),
        name="attn_sample",
    )(sinks, q3, kn3, vn3, knf3, vnf3, cache_k, cache_v)


def _merge_kernel(yr_ref, o_ref, ga_ref, gb_ref, wr_ref, wa_ref, m_ref):
    a = _dot(yr_ref[...], wr_ref[...])
    b = _dot(o_ref[...], wa_ref[...])
    m_ref[...] = (jax.nn.sigmoid(ga_ref[...]) * a + jax.nn.sigmoid(gb_ref[...]) * b).astype(BF16)


def _merge(yr, o, z, wr, wa, tm=512, tn=1024):
    n = yr.shape[0]
    tm = min(tm, n)
    nj = D_MODEL // tn
    return pl.pallas_call(
        _merge_kernel,
        out_shape=jax.ShapeDtypeStruct((n, D_MODEL), BF16),
        grid=(n // tm, nj),
        in_specs=[
            pl.BlockSpec((tm, D_RNN), lambda i, j: (i, 0)),
            pl.BlockSpec((tm, D_Q), lambda i, j: (i, 0)),
            pl.BlockSpec((tm, tn), lambda i, j: (i, OFF_GA // tn + j)),
            pl.BlockSpec((tm, tn), lambda i, j: (i, OFF_GB // tn + j)),
            pl.BlockSpec((D_RNN, tn), lambda i, j: (0, j)),
            pl.BlockSpec((D_Q, tn), lambda i, j: (0, j)),
        ],
        out_specs=pl.BlockSpec((tm, tn), lambda i, j: (i, j)),
        compiler_params=_params("parallel", "arbitrary"),
        name="merge",
    )(yr, o, z, z, wr, wa)


def _out_proj_kernel(x_ref, m_ref, w_ref, g_ref, x1_ref, n2_ref):
    x1 = x_ref[...] + _dot(m_ref[...], w_ref[...])
    x1_ref[...] = x1
    n2_ref[...] = _rmsnorm_rows(x1, g_ref[...]).astype(BF16)


def _out_proj(x, m, w, g, tm=512):
    n = x.shape[0]
    tm = min(tm, n)
    row = lambda dt: pl.BlockSpec((tm, D_MODEL), lambda i: (i, 0))
    return pl.pallas_call(
        _out_proj_kernel,
        out_shape=(jax.ShapeDtypeStruct((n, D_MODEL), F32), jax.ShapeDtypeStruct((n, D_MODEL), BF16)),
        grid=(n // tm,),
        in_specs=[row(F32), row(BF16), pl.BlockSpec((D_MODEL, D_MODEL), lambda i: (0, 0)),
                  pl.BlockSpec((1, D_MODEL), lambda i: (0, 0))],
        out_specs=(row(F32), row(BF16)),
        compiler_params=_params("parallel"),
        name="out_proj",
    )(x, m, w, g)


def _topk_rows(s, k, rid=None):
    rows, t = s.shape
    if rid is None:
        rid = lax.broadcasted_iota(jnp.int32, (rows, t), 0).astype(F32)
    slot = lax.broadcasted_iota(jnp.int32, (k, t), 0)
    vals = jnp.zeros((k, t), F32)
    ids = jnp.zeros((k, t), F32)
    for r in range(k):
        m = jnp.max(s, axis=0, keepdims=True)
        i = jnp.min(jnp.where(s == m, rid, jnp.inf), axis=0, keepdims=True)
        vals = jnp.where(slot == r, m, vals)
        ids = jnp.where(slot == r, i, ids)
        s = jnp.where(rid == i, -jnp.inf, s)
    return vals, ids


def _peer_route_kernel(n2_ref, wq_ref, sk_ref, idx_ref, gate_ref):
    q = _dot(n2_ref[...], wq_ref[...]).astype(BF16)
    gates, experts = [], []
    for h in range(PEER_HEADS):
        top = []
        for p in range(2):
            c = (2 * h + p) * D_HALF_KEY
            s = _dot_nt(sk_ref[2 * h + p], q[:, c:c + D_HALF_KEY])
            top.append(_topk_rows(s, PEER_TOPK))
        (s1, i1), (s2, i2) = top
        hk = PEER_TOPK // 2
        tcol = s1.shape[1]
        brow = lambda n: lax.broadcasted_iota(jnp.int32, (n, tcol), 0).astype(F32)
        cand = [s1[0:1, :] + s2]
        cidx = [i1[0:1, :] * float(N_KEYS) + i2]
        rid = [brow(PEER_TOPK)]
        for a in range(1, hk):
            cand.append(s1[a:a + 1, :] + s2[0:hk, :])
            cidx.append(i1[a:a + 1, :] * float(N_KEYS) + i2[0:hk, :])
            rid.append(brow(hk) + float(a * PEER_TOPK))
        cand.append(s1[hk:, :] + s2[0:1, :])
        cidx.append(i1[hk:, :] * float(N_KEYS) + i2[0:1, :])
        rid.append((brow(hk) + float(hk)) * float(PEER_TOPK))
        cand, cidx, rid = (jnp.concatenate(x, axis=0) for x in (cand, cidx, rid))
        top_s, pos = _topk_rows(cand, PEER_TOPK, rid)
        slot = lax.broadcasted_iota(jnp.int32, top_s.shape, 0)
        ids = jnp.zeros(top_s.shape, F32)
        for r in range(PEER_TOPK):
            e = jnp.max(jnp.where(rid == pos[r:r + 1, :], cidx, -1.0), axis=0, keepdims=True)
            ids = jnp.where(slot == r, e, ids)
        w = jnp.exp(top_s - top_s[0:1, :])
        gates.append(w / jnp.sum(w, axis=0, keepdims=True))
        experts.append(ids)
    gate_ref[...] = jnp.concatenate(gates, axis=0).T
    idx_ref[...] = jnp.concatenate(experts, axis=0).T.astype(jnp.int32)


def _peer_route(n2, wq, sk, tm=256):
    n = n2.shape[0]
    tm = min(tm, n)
    return pl.pallas_call(
        _peer_route_kernel,
        out_shape=(jax.ShapeDtypeStruct((n, N_PICKS), jnp.int32), jax.ShapeDtypeStruct((n, N_PICKS), F32)),
        grid=(n // tm,),
        in_specs=[pl.BlockSpec((tm, D_MODEL), lambda i: (i, 0)),
                  pl.BlockSpec((D_MODEL, PEER_HEADS * D_KEY), lambda i: (0, 0)),
                  pl.BlockSpec((2 * PEER_HEADS, N_KEYS, D_HALF_KEY), lambda i: (0, 0, 0))],
        out_specs=(pl.BlockSpec((tm, N_PICKS), lambda i: (i, 0)), pl.BlockSpec((tm, N_PICKS), lambda i: (i, 0))),
        compiler_params=_params("parallel"),
        name="peer_route",
    )(n2, wq, sk)


D_TILES = D_MODEL // LANES
assert D_TILES == BF16_TILE_ROWS


def _peer_mix_kernel(idx_ref, idxn_ref, x_ref, gate_ref, exp_ref, expt_ref, tab_ref, o_ref, buf_a, buf_b, d_ref,
                     sem_ref, *, tb, nsteps):
    i = pl.program_id(0)
    bufs = (buf_a, buf_b)

    def start_token(ids_ref, row, half, t):
        for j in range(N_PICKS):
            pltpu.make_async_copy(tab_ref.at[ids_ref[row, j]], bufs[half].at[t * N_PICKS + j],
                                  sem_ref.at[half, t]).start(priority=j % 2)

    def wait_token(half, t):
        pltpu.make_async_copy(tab_ref.at[pl.ds(0, N_PICKS)], bufs[half].at[pl.ds(t * N_PICKS, N_PICKS)],
                              sem_ref.at[half, t]).wait()

    @pl.when(i == 0)
    def _():
        for t in range(tb):
            start_token(idx_ref, t, 0, t)

    rows = N_PICKS * D_TILES
    diag = (lax.broadcasted_iota(jnp.int32, (D_TILES, rows), 1) % D_TILES
            == lax.broadcasted_iota(jnp.int32, (D_TILES, rows), 0))

    def run_tile(half, start_other):
        buf = bufs[half]
        for t in range(tb):
            start_other(t)
            wait_token(half, t)
            u = buf[t * N_PICKS:(t + 1) * N_PICKS, 0:D_TILES, :].reshape(rows, LANES)
            g = _dot_nt(x_ref[half * tb + t], u)
            d_ref[t:t + 1, :] = jnp.sum(jnp.where(diag, g, 0.0), axis=0, keepdims=True)
        d = d_ref[...]
        d_hi = d.astype(BF16)
        d_lo = (d - d_hi.astype(F32)).astype(BF16)
        h = _dot(d_hi, expt_ref[...]) + _dot(d_lo, expt_ref[...])
        w = (jax.nn.gelu(h) * gate_ref[half * tb:(half + 1) * tb, :]).astype(BF16)
        wrow = _dot(w, exp_ref[...])
        for t in range(tb):
            v = buf[t * N_PICKS:(t + 1) * N_PICKS, D_TILES:2 * D_TILES, :].reshape(rows, LANES)
            wm = jnp.where(diag, wrow[t:t + 1, :], 0.0).astype(BF16)
            o_ref[half * tb + t] = _dot(wm, v)

    run_tile(0, lambda t: start_token(idx_ref, tb + t, 1, t))
    run_tile(1, lambda t: start_token(idxn_ref, t, 0, t))

    @pl.when(i == nsteps - 1)
    def _():
        for t in range(tb):
            wait_token(0, t)


def _peer_mix(idx, gates, n2, table, expand, expand_t, tb=BF16_TILE_ROWS):
    n = idx.shape[0]
    nsteps = n // (2 * tb)
    x3 = n2.reshape(n, D_TILES, LANES)
    rows = N_PICKS * D_TILES
    out = pl.pallas_call(
        functools.partial(_peer_mix_kernel, tb=tb, nsteps=nsteps),
        out_shape=jax.ShapeDtypeStruct((n, D_TILES, LANES), F32),
        grid=(nsteps,),
        in_specs=[
            pl.BlockSpec((2 * tb, N_PICKS), lambda i: (i, 0), memory_space=pltpu.SMEM),
            pl.BlockSpec((tb, N_PICKS), lambda i: (jnp.minimum(2 * i + 2, 2 * nsteps - 2), 0),
                         memory_space=pltpu.SMEM),
            pl.BlockSpec((2 * tb, D_TILES, LANES), lambda i: (i, 0, 0)),
            pl.BlockSpec((2 * tb, N_PICKS), lambda i: (i, 0)),
            pl.BlockSpec((N_PICKS, rows), lambda i: (0, 0)),
            pl.BlockSpec((rows, N_PICKS), lambda i: (0, 0)),
            pl.BlockSpec(memory_space=pl.ANY),
        ],
        out_specs=pl.BlockSpec((2 * tb, D_TILES, LANES), lambda i: (i, 0, 0)),
        scratch_shapes=[pltpu.VMEM((tb * N_PICKS, 2 * D_TILES, LANES), BF16),
                        pltpu.VMEM((tb * N_PICKS, 2 * D_TILES, LANES), BF16),
                        pltpu.VMEM((tb, rows), F32),
                        pltpu.SemaphoreType.DMA((2, tb))],
        compiler_params=_params("arbitrary"),
        name="peer_mix",
    )(idx, idx, x3, gates, expand, expand_t, table)
    return out.reshape(n, D_MODEL)


def _pack_table_kernel(u_ref, v_ref, o_ref):
    o_ref[:, 0:D_TILES, :] = u_ref[...].astype(BF16)
    o_ref[:, D_TILES:2 * D_TILES, :] = v_ref[...].astype(BF16)


def _pack_table(u3, v3, te=256):
    e = u3.shape[0]
    te = min(te, e)
    blk = pl.BlockSpec((te, D_TILES, LANES), lambda i: (i, 0, 0))
    return pl.pallas_call(
        _pack_table_kernel,
        out_shape=jax.ShapeDtypeStruct((e, 2 * D_TILES, LANES), BF16),
        grid=(e // te,),
        in_specs=[blk, blk],
        out_specs=pl.BlockSpec((te, 2 * D_TILES, LANES), lambda i: (i, 0, 0)),
        compiler_params=_params("parallel"),
        name="pack_table",
    )(u3, v3)


def _ple_kernel(x1_ref, po_ref, ple_ref, g_ref, wp_ref, wg_ref, y_ref):
    x2 = x1_ref[...] + po_ref[...]
    n3 = _rmsnorm_rows(x2, g_ref[...]).astype(BF16)
    emb = _dot(ple_ref[...].astype(BF16), wp_ref[...])
    y_ref[...] = x2 + emb * jax.nn.sigmoid(_dot(n3, wg_ref[...]))


def _ple(x1, po, ple, g, wp, wg, tm=512):
    n = x1.shape[0]
    tm = min(tm, n)
    row = pl.BlockSpec((tm, D_MODEL), lambda i: (i, 0))
    return pl.pallas_call(
        _ple_kernel,
        out_shape=jax.ShapeDtypeStruct((n, D_MODEL), F32),
        grid=(n // tm,),
        in_specs=[row, row, pl.BlockSpec((tm, D_PLE), lambda i: (i, 0)),
                  pl.BlockSpec((1, D_MODEL), lambda i: (0, 0)),
                  pl.BlockSpec((D_PLE, D_MODEL), lambda i: (0, 0)),
                  pl.BlockSpec((D_MODEL, D_MODEL), lambda i: (0, 0))],
        out_specs=row,
        compiler_params=_params("parallel"),
        name="ple",
    )(x1, po, ple, g, wp, wg)


def _rope_tables(pos):
    half = ROT_DIM // 2
    inv = ROPE_THETA ** (-jnp.arange(0, ROT_DIM, 2, dtype=F32) / ROT_DIM)
    ang = pos.astype(F32)[:, None] * inv[None, :]
    cos, sin = jnp.cos(ang), jnp.sin(ang)
    n = pos.shape[0]
    pad = jnp.zeros((n, HEAD_DIM - ROT_DIM), F32)
    zh = jnp.zeros((n, half), F32)
    cos_h = jnp.concatenate([cos, cos, pad + 1.0], axis=1)
    up_h = jnp.concatenate([-sin, zh, pad], axis=1)
    dn_h = jnp.concatenate([zh, sin, pad], axis=1)
    rep = LANES // HEAD_DIM
    return tuple(jnp.tile(a, (1, rep)) for a in (cos_h, up_h, dn_h))


def _block_diag(w):
    eye = jnp.eye(RNN_BLOCKS, dtype=w.dtype)
    return jnp.einsum("ncd,nm->ncmd", w, eye).reshape(D_RNN, D_RNN)


def _head_perm():
    return np.array([j * GROUP + g for g in range(GROUP) for j in range(N_KV_HEADS)])


def _token_pipeline_tail(x, z, yr, o, w, ple):
    m = _merge(yr, o, z, w["proj_rnn"], w["proj_attn"])
    x1, n2 = _out_proj(x, m, w["out"], w["norm_ffn"])
    idx, gate = _peer_route(n2, w["peer_q"], w["sub_keys"])
    po = _peer_mix(idx, gate, n2, w["peer_table"], w["expand"], w["expand_t"])
    return _ple(x1, po, ple, w["norm_ple"], w["ple"], w["ple_gate"])


def kernel(x_prompt, x_sample, p_prompt, p_sample, state_conv, state_rglru, cache_k, cache_v, norm_mix, w_in, conv_w, conv_b, w_rgate, b_rgate, w_igate, b_igate, lru_lambda, w_proj_rnn, q_norm, k_norm, attn_sinks, w_proj_attn, w_out, norm_ffn, w_peer_q, peer_sub_keys, peer_u, peer_v, w_ple, norm_ple, w_ple_gate):
    depth = w_in.shape[0]
    assert depth == 1
    l = 0
    B, S, _ = x_prompt.shape
    DB, DS, _ = x_sample.shape
    wbuf = cache_k.shape[2]

    hp = _head_perm()
    offs = np.cumsum([0, D_RNN, D_RNN, D_Q, D_KV, D_KV, D_MODEL, D_MODEL])
    xr_c, gr_c, q_c, k_c, v_c, ga_c, gb_c = [np.arange(offs[i], offs[i + 1]) for i in range(7)]
    q_c = q_c.reshape(N_Q_HEADS, HEAD_DIM)[hp].reshape(-1)
    cols = np.concatenate([ga_c, gb_c, xr_c, gr_c, q_c, k_c, v_c])
    row2 = lambda a: a[l].reshape(1, -1)
    w = {
        "proj_rnn": w_proj_rnn[l].astype(BF16),
        "proj_attn": w_proj_attn[l].reshape(N_Q_HEADS, HEAD_DIM, D_MODEL)[hp].reshape(D_Q, D_MODEL).astype(BF16),
        "out": w_out[l].astype(BF16),
        "norm_ffn": row2(norm_ffn),
        "peer_q": w_peer_q[l].astype(BF16),
        "sub_keys": peer_sub_keys[l].reshape(2 * PEER_HEADS, N_KEYS, D_HALF_KEY).astype(BF16),
        "peer_table": _pack_table(peer_u[l].reshape(N_EXPERTS, D_TILES, LANES),
                                  peer_v[l].reshape(N_EXPERTS, D_TILES, LANES)),
        "norm_ple": row2(norm_ple),
        "ple": w_ple[l].astype(BF16),
        "ple_gate": w_ple_gate[l].astype(BF16),
    }
    expand = np.repeat(np.eye(N_PICKS, dtype=np.float32), D_TILES, axis=1)
    w["expand"] = jnp.asarray(expand, BF16)
    w["expand_t"] = jnp.asarray(expand.T, BF16)
    runs = np.split(cols, np.flatnonzero(np.diff(cols) != 1) + 1)
    w_in_b = jnp.concatenate([w_in[l][:, r[0]:r[-1] + 1] for r in runs], axis=1).astype(BF16)
    g_mix = row2(norm_mix)
    cw, cb = conv_w[l], row2(conv_b)
    wr, br = _block_diag(w_rgate[l]).astype(BF16), row2(b_rgate)
    wi, bi = _block_diag(w_igate[l]).astype(BF16), row2(b_igate)
    lam = row2(lru_lambda)
    rep = LANES // HEAD_DIM
    qg = jnp.tile(q_norm[l], rep).reshape(1, LANES)
    kg = jnp.tile(k_norm[l], rep).reshape(1, LANES)
    seg = lambda width: jnp.asarray(
        np.kron(np.eye(width // HEAD_DIM, dtype=np.float32), np.ones((HEAD_DIM, HEAD_DIM), np.float32)), BF16)
    ones_q, ones_k = seg(D_Q), seg(D_KV)
    sinks = attn_sinks[l]

    xp = x_prompt.reshape(B * S, D_MODEL)
    zp = _in_proj(xp, g_mix, w_in_b)
    yr_p, h_p = _rnn_prompt(zp, B, S, cw, cb, wr, br, wi, bi, lam)
    tabs_p = _rope_tables(jnp.arange(S, dtype=jnp.int32))
    q_p, kf_p, kb_p, vb_p = _qk_prep(zp, tabs_p, qg, kg, ones_q, ones_k)
    o_p = _attn_prompt(sinks, q_p, kb_p, vb_p, B, S)
    y_p = _token_pipeline_tail(xp, zp, yr_p, o_p, w, p_prompt[l].reshape(B * S, D_PLE))

    zp3 = zp.reshape(B, S, D_IN)
    keep = min(WINDOW, S)
    prompt_conv = zp3[:, S - (CONV_W - 1):, OFF_XR:OFF_XR + D_RNN]
    prompt_k = kf_p.reshape(B, S, N_KV_HEADS, HEAD_DIM)[:, S - keep:]
    prompt_v = zp3[:, S - keep:, OFF_V:OFF_V + D_KV].reshape(B, keep, N_KV_HEADS, HEAD_DIM)

    ns = DB * DS
    xs = x_sample.reshape(ns, D_MODEL)
    zs = _in_proj(xs, g_mix, w_in_b)
    zs3 = zs.reshape(DB, DS, D_IN)
    tmaj = lambda a: jnp.transpose(a, (1, 0, 2))
    yr_s_t, h_s = _rnn_sample(tmaj(zs3[:, :, OFF_XR:OFF_XR + D_RNN]), tmaj(zs3[:, :, OFF_GR:OFF_GR + D_RNN]),
                              tmaj(state_conv[l]), state_rglru[l], cw, cb, wr, br, wi, bi, lam)
    yr_s = tmaj(yr_s_t).reshape(ns, D_RNN)
    pos_s = PAST_LEN + jnp.arange(DS, dtype=jnp.int32)
    tabs_s = tuple(jnp.tile(a, (DB, 1)) for a in _rope_tables(pos_s))
    q_s, kf_s, kb_s, vb_s = _qk_prep(zs, tabs_s, qg, kg, ones_q, ones_k)
    tq = BF16_TILE_ROWS
    pad_t = lambda a, rows: jnp.pad(a.reshape(DB, DS, -1), ((0, 0), (0, rows - DS), (0, 0)))
    ck = cache_k[l].reshape(DB, wbuf, D_KV)
    cv = cache_v[l].reshape(DB, wbuf, D_KV)
    o_s, sample_k, sample_v = _attn_sample(
        sinks, pad_t(q_s, tq), pad_t(kb_s, tq), pad_t(vb_s, tq), pad_t(kf_s, SUBLANES),
        pad_t(zs3[:, :, OFF_V:OFF_V + D_KV], SUBLANES), ck, cv, DS)
    o_s = o_s[:, :DS].reshape(ns, D_Q)
    y_s = _token_pipeline_tail(xs, zs, yr_s, o_s, w, p_sample[l].reshape(ns, D_PLE))

    sample_conv = jnp.concatenate([state_conv[l], zs3[:, :, OFF_XR:OFF_XR + D_RNN]], axis=1)[:, DS:]
    sample_k = sample_k.reshape(DB, wbuf, N_KV_HEADS, HEAD_DIM)
    sample_v = sample_v.reshape(DB, wbuf, N_KV_HEADS, HEAD_DIM)

    return (y_p.reshape(B, S, D_MODEL), y_s.reshape(DB, DS, D_MODEL),
            prompt_conv[None], h_p.reshape(1, B, D_RNN), prompt_k[None], prompt_v[None],
            sample_conv[None], h_s[None], sample_k[None], sample_v[None])
```

```python
import functools

import jax
import jax.numpy as jnp
import numpy as np
from jax import lax
from jax.experimental import pallas as pl
from jax.experimental.pallas import tpu as pltpu

D_MODEL = 2048
D_RNN = D_MODEL // 2
RNN_BLOCKS = 8
RNN_BLOCK = D_RNN // RNN_BLOCKS
CONV_W = 4
LRU_C = 8.0
HEAD_DIM = 64
N_Q_HEADS = D_MODEL // 2 // HEAD_DIM
N_KV_HEADS = 4
GROUP = N_Q_HEADS // N_KV_HEADS
D_Q = N_Q_HEADS * HEAD_DIM
D_KV = N_KV_HEADS * HEAD_DIM
WINDOW = 128
ROT_DIM = HEAD_DIM // 4
ROPE_THETA = 500000.0
N_KEYS = 128
N_EXPERTS = N_KEYS * N_KEYS
PEER_HEADS = 8
PEER_TOPK = 16
D_KEY = 256
D_HALF_KEY = D_KEY // 2
N_PICKS = PEER_HEADS * PEER_TOPK
D_PLE = 256
EPS = 1e-6
NEG_INF = -1e30
PAST_LEN = 16384

LANES = 128
SUBLANES = 8
BF16_TILE_ROWS = 16
VMEM_LIMIT_BYTES = 56 * 1024 * 1024

OFF_GA = 0
OFF_GB = OFF_GA + D_MODEL
OFF_XR = OFF_GB + D_MODEL
OFF_GR = OFF_XR + D_RNN
OFF_Q = OFF_GR + D_RNN
OFF_K = OFF_Q + D_Q
OFF_V = OFF_K + D_KV
D_IN = OFF_V + D_KV

BF16 = jnp.bfloat16
F32 = jnp.float32


def _params(*sem):
    return pltpu.CompilerParams(dimension_semantics=sem, vmem_limit_bytes=VMEM_LIMIT_BYTES)


def _rmsnorm_rows(x, g):
    return x * lax.rsqrt(jnp.mean(x * x, axis=-1, keepdims=True) + EPS) * g


def _dot(a, b):
    return jnp.dot(a, b, preferred_element_type=F32)


def _dot_nt(a, b):
    return lax.dot_general(a, b, (((1,), (1,)), ((), ())), preferred_element_type=F32)


def _in_proj_kernel(x_ref, g_ref, w_ref, z_ref, xn_ref):
    @pl.when(pl.program_id(1) == 0)
    def _():
        xn_ref[...] = _rmsnorm_rows(x_ref[...], g_ref[...]).astype(BF16)

    z_ref[...] = _dot(xn_ref[...], w_ref[...])


def _in_proj(x, g, w, tm=512, tn=1536):
    n = x.shape[0]
    tm = min(tm, n)
    return pl.pallas_call(
        _in_proj_kernel,
        out_shape=jax.ShapeDtypeStruct((n, D_IN), F32),
        grid=(n // tm, D_IN // tn),
        in_specs=[
            pl.BlockSpec((tm, D_MODEL), lambda i, j: (i, 0)),
            pl.BlockSpec((1, D_MODEL), lambda i, j: (0, 0)),
            pl.BlockSpec((D_MODEL, tn), lambda i, j: (0, j)),
        ],
        out_specs=pl.BlockSpec((tm, tn), lambda i, j: (i, j)),
        scratch_shapes=[pltpu.VMEM((tm, D_MODEL), BF16)],
        compiler_params=_params("parallel", "arbitrary"),
        name="in_proj",
    )(x, g, w)


def _softplus(x):
    return jnp.maximum(x, 0.0) + jnp.log(1.0 + jnp.exp(-jnp.abs(x)))


def _neg_expm1(x):
    t = jnp.tanh(-0.5 * x)
    return 2.0 * t / (1.0 + t)


def _lru_coeffs(xc, wr, br, wi, bi, lam, first_is_pos0):
    xb = xc.astype(BF16)
    r = jax.nn.sigmoid(_dot(xb, wr) + br)
    i = jax.nn.sigmoid(_dot(xb, wi) + bi)
    log_a = -LRU_C * r * _softplus(-lam)
    a = jnp.exp(log_a)
    mult = jnp.sqrt(_neg_expm1(2.0 * log_a))
    if first_is_pos0 is not None:
        row = lax.broadcasted_iota(jnp.int32, xc.shape, 0)
        mult = jnp.where(jnp.logical_and(first_is_pos0, row == 0), 1.0, mult)
    return a, mult * i * xc


def _rnn_prompt_kernel(xr_ref, gr_ref, cw_ref, cb_ref, wr_ref, br_ref, wi_ref, bi_ref, lam_ref,
                       yr_ref, hlast_ref, tail_ref, h_ref, *, tb):
    t = pl.program_id(1)

    @pl.when(t == 0)
    def _():
        tail_ref[...] = jnp.zeros_like(tail_ref)
        h_ref[...] = jnp.zeros_like(h_ref)

    x = xr_ref[...]
    tail = tail_ref[...]
    row8 = lax.broadcasted_iota(jnp.int32, (SUBLANES, D_RNN), 0)
    cw = cw_ref[...]
    xc = cb_ref[...] + cw[CONV_W - 1:CONV_W, :] * x
    for k in range(1, CONV_W):
        xs = pltpu.roll(x, k, axis=0)
        head = jnp.where(row8 < k, pltpu.roll(tail, k, axis=0), xs[:SUBLANES])
        xs = jnp.concatenate([head, xs[SUBLANES:]], axis=0)
        xc = xc + cw[CONV_W - 1 - k:CONV_W - k, :] * xs
    tail_ref[...] = x[tb - SUBLANES:, :]

    a, b = _lru_coeffs(xc, wr_ref[...], br_ref[...], wi_ref[...], bi_ref[...], lam_ref[...], t == 0)

    row = lax.broadcasted_iota(jnp.int32, (tb, D_RNN), 0)
    d = 1
    while d < tb:
        a_sh = pltpu.roll(a, d, axis=0)
        b_sh = pltpu.roll(b, d, axis=0)
        keep = row < d
        b = jnp.where(keep, b, a * b_sh + b)
        a = jnp.where(keep, a, a * a_sh)
        d *= 2
    h = b + a * h_ref[...]
    h_ref[...] = h[tb - 1:tb, :]
    hlast_ref[0] = h[tb - 1:tb, :]
    yr_ref[...] = (h * jax.nn.gelu(gr_ref[...])).astype(BF16)


def _rnn_prompt(z, batch, seq, cw, cb, wr, br, wi, bi, lam, tb=256):
    nt = seq // tb
    vec = pl.BlockSpec((1, D_RNN), lambda b, t: (0, 0))
    mat = pl.BlockSpec((D_RNN, D_RNN), lambda b, t: (0, 0))
    return pl.pallas_call(
        functools.partial(_rnn_prompt_kernel, tb=tb),
        out_shape=(jax.ShapeDtypeStruct((batch * seq, D_RNN), BF16),
                   jax.ShapeDtypeStruct((batch, 1, D_RNN), F32)),
        grid=(batch, nt),
        in_specs=[
            pl.BlockSpec((tb, D_RNN), lambda b, t: (b * nt + t, OFF_XR // D_RNN)),
            pl.BlockSpec((tb, D_RNN), lambda b, t: (b * nt + t, OFF_GR // D_RNN)),
            pl.BlockSpec((CONV_W, D_RNN), lambda b, t: (0, 0)),
            vec, mat, vec, mat, vec, vec,
        ],
        out_specs=(pl.BlockSpec((tb, D_RNN), lambda b, t: (b * nt + t, 0)),
                   pl.BlockSpec((1, 1, D_RNN), lambda b, t: (b, 0, 0))),
        scratch_shapes=[pltpu.VMEM((SUBLANES, D_RNN), F32), pltpu.VMEM((1, D_RNN), F32)],
        compiler_params=_params("parallel", "arbitrary"),
        name="rnn_prompt",
    )(z, z, cw, cb, wr, br, wi, bi, lam)


def _rnn_sample_kernel(xr_ref, gr_ref, buf_ref, h0_ref, cw_ref, cb_ref, wr_ref, br_ref, wi_ref,
                       bi_ref, lam_ref, yr_ref, hlast_ref, *, steps):
    cw = cw_ref[...]
    xp = [buf_ref[k] for k in range(CONV_W - 1)] + [xr_ref[s] for s in range(steps)]
    h = h0_ref[...]
    for s in range(steps):
        xc = cb_ref[...] + sum(cw[k:k + 1, :] * xp[s + k] for k in range(CONV_W))
        a, b = _lru_coeffs(xc, wr_ref[...], br_ref[...], wi_ref[...], bi_ref[...], lam_ref[...], None)
        h = a * h + b
        yr_ref[s] = (h * jax.nn.gelu(gr_ref[s])).astype(BF16)
    hlast_ref[...] = h


def _rnn_sample(xr_t, gr_t, buf_t, h0, cw, cb, wr, br, wi, bi, lam):
    steps, db, _ = xr_t.shape
    return pl.pallas_call(
        functools.partial(_rnn_sample_kernel, steps=steps),
        out_shape=(jax.ShapeDtypeStruct((steps, db, D_RNN), BF16),
                   jax.ShapeDtypeStruct((db, D_RNN), F32)),
        compiler_params=pltpu.CompilerParams(vmem_limit_bytes=VMEM_LIMIT_BYTES),
        name="rnn_sample",
    )(xr_t, gr_t, buf_t, h0, cw, cb, wr, br, wi, bi, lam)


def _headnorm_rope(x, gain, seg_ones, cos_t, sin_up_t, sin_dn_t, scale):
    w = x.shape[1]
    sq = x * x
    hi = sq.astype(BF16)
    lo = (sq - hi.astype(F32)).astype(BF16)
    ms = (_dot(hi, seg_ones) + _dot(lo, seg_ones)) * (1.0 / HEAD_DIM)
    xn = x * lax.rsqrt(ms + EPS)
    outs = []
    for c in range(w // LANES):
        xt = xn[:, c * LANES:(c + 1) * LANES] * gain
        up = pltpu.roll(xt, LANES - ROT_DIM // 2, axis=1)
        dn = pltpu.roll(xt, ROT_DIM // 2, axis=1)
        outs.append((xt * cos_t + up * sin_up_t + dn * sin_dn_t) * scale)
    return jnp.concatenate(outs, axis=1)


def _qk_prep_kernel(q_ref, k_ref, v_ref, cos_ref, sup_ref, sdn_ref, qg_ref, kg_ref, oq_ref, ok_ref,
                    qo_ref, kf_ref, kb_ref, vb_ref):
    cos_t, sup, sdn = cos_ref[...], sup_ref[...], sdn_ref[...]
    q = _headnorm_rope(q_ref[...], qg_ref[...], oq_ref[...], cos_t, sup, sdn, HEAD_DIM ** -0.5)
    qo_ref[...] = q.astype(BF16)
    k = _headnorm_rope(k_ref[...], kg_ref[...], ok_ref[...], cos_t, sup, sdn, 1.0)
    kf_ref[...] = k
    kb_ref[...] = k.astype(BF16)
    vb_ref[...] = v_ref[...].astype(BF16)


def _qk_prep(z, tabs, qg, kg, ones_q, ones_k, tm=512):
    n = z.shape[0]
    tm = min(tm, n)
    cos_t, sup_t, sdn_t = tabs
    ntab = cos_t.shape[0] // tm
    tab = pl.BlockSpec((tm, LANES), lambda i: (i % ntab, 0))
    const = lambda shape: pl.BlockSpec(shape, lambda i: (0, 0))
    return pl.pallas_call(
        _qk_prep_kernel,
        out_shape=(jax.ShapeDtypeStruct((n, D_Q), BF16), jax.ShapeDtypeStruct((n, D_KV), F32),
                   jax.ShapeDtypeStruct((n, D_KV), BF16), jax.ShapeDtypeStruct((n, D_KV), BF16)),
        grid=(n // tm,),
        in_specs=[
            pl.BlockSpec((tm, D_Q), lambda i: (i, OFF_Q // D_Q)),
            pl.BlockSpec((tm, D_KV), lambda i: (i, OFF_K // D_KV)),
            pl.BlockSpec((tm, D_KV), lambda i: (i, OFF_V // D_KV)),
            tab, tab, tab,
            const((1, LANES)), const((1, LANES)), const((D_Q, D_Q)), const((D_KV, D_KV)),
        ],
        out_specs=(pl.BlockSpec((tm, D_Q), lambda i: (i, 0)), pl.BlockSpec((tm, D_KV), lambda i: (i, 0)),
                   pl.BlockSpec((tm, D_KV), lambda i: (i, 0)), pl.BlockSpec((tm, D_KV), lambda i: (i, 0))),
        compiler_params=_params("parallel"),
        name="qk_prep",
    )(z, z, z, cos_t, sup_t, sdn_t, qg, kg, ones_q, ones_k)


def _sink_attention(q, k, v, valid, sink_ref, tq):
    lane_head = lax.broadcasted_iota(jnp.int32, (tq, D_KV), 1) // HEAD_DIM
    rowg = lax.broadcasted_iota(jnp.int32, (GROUP * tq, 1), 0) // tq
    validg = jnp.concatenate([valid] * GROUP, axis=0)
    out = [jnp.zeros((tq, D_KV), F32) for _ in range(GROUP)]
    for j in range(N_KV_HEADS):
        sel = lane_head == j
        keep = jnp.where(sel, 1.0, 0.0).astype(BF16)
        qs = jnp.concatenate([q[:, g * D_KV:(g + 1) * D_KV] * keep for g in range(GROUP)], axis=0)
        s = jnp.where(validg, _dot_nt(qs, k), NEG_INF)
        sk = jnp.zeros((GROUP * tq, 1), F32)
        for g in range(GROUP):
            sk = jnp.where(rowg == g, sink_ref[j * GROUP + g], sk)
        m = jnp.maximum(jnp.max(s, axis=-1, keepdims=True), sk)
        p = jnp.exp(s - m)
        denom = jnp.sum(p, axis=-1, keepdims=True) + jnp.exp(sk - m)
        pv = _dot(p.astype(BF16), v) / denom
        for g in range(GROUP):
            out[g] = jnp.where(sel, pv[g * tq:(g + 1) * tq], out[g])
    return jnp.concatenate(out, axis=1)


def _attn_prompt_kernel(sink_ref, q_ref, kp_ref, kc_ref, vp_ref, vc_ref, o_ref):
    nb = pl.program_id(1)
    k = jnp.concatenate([kp_ref[...], kc_ref[...]], axis=0)
    v = jnp.concatenate([vp_ref[...], vc_ref[...]], axis=0)
    i = lax.broadcasted_iota(jnp.int32, (WINDOW, 2 * WINDOW), 0)
    j = lax.broadcasted_iota(jnp.int32, (WINDOW, 2 * WINDOW), 1)
    d = WINDOW + i - j
    valid = (d >= 0) & (d < WINDOW) & ((j >= WINDOW) | (nb > 0))
    o_ref[...] = _sink_attention(q_ref[...], k, v, valid, sink_ref, WINDOW).astype(BF16)


def _attn_prompt(sinks, q, kb, vb, batch, seq):
    nblk = seq // WINDOW
    cur = lambda w: pl.BlockSpec((WINDOW, w), lambda b, t: (b * nblk + t, 0))
    prev = lambda w: pl.BlockSpec((WINDOW, w), lambda b, t: (b * nblk + jnp.maximum(t - 1, 0), 0))
    return pl.pallas_call(
        _attn_prompt_kernel,
        out_shape=jax.ShapeDtypeStruct((batch * seq, D_Q), BF16),
        grid=(batch, nblk),
        in_specs=[pl.BlockSpec(memory_space=pltpu.SMEM), cur(D_Q), prev(D_KV), cur(D_KV),
                  prev(D_KV), cur(D_KV)],
        out_specs=cur(D_Q),
        compiler_params=_params("parallel", "arbitrary"),
        name="attn_prompt",
    )(sinks, q, kb, kb, vb, vb)


def _shift_in(cache, new8, steps):
    wbuf = cache.shape[0]
    rolled = pltpu.roll(cache, wbuf - steps, axis=0)
    row8 = lax.broadcasted_iota(jnp.int32, new8.shape, 0)
    tail = jnp.where(row8 >= SUBLANES - steps, pltpu.roll(new8, SUBLANES - steps, axis=0),
                     rolled[wbuf - SUBLANES:])
    return jnp.concatenate([rolled[:wbuf - SUBLANES], tail], axis=0)


def _attn_sample_kernel(sink_ref, q_ref, kn_ref, vn_ref, knf_ref, vnf_ref, ck_ref, cv_ref, o_ref, ok_ref, ov_ref,
                        *, bs, tq, wbuf, steps):
    t = lax.broadcasted_iota(jnp.int32, (tq, wbuf + tq), 0)
    c = lax.broadcasted_iota(jnp.int32, (tq, wbuf + tq), 1)
    d = wbuf + t - c
    valid = (d >= 0) & (d < WINDOW)
    for s in range(bs):
        ck, cv = ck_ref[s], cv_ref[s]
        k = jnp.concatenate([ck.astype(BF16), kn_ref[s]], axis=0)
        v = jnp.concatenate([cv.astype(BF16), vn_ref[s]], axis=0)
        o_ref[s] = _sink_attention(q_ref[s], k, v, valid, sink_ref, tq).astype(BF16)
        ok_ref[s] = _shift_in(ck, knf_ref[s], steps)
        ov_ref[s] = _shift_in(cv, vnf_ref[s], steps)


def _attn_sample(sinks, q3, kn3, vn3, knf3, vnf3, cache_k, cache_v, steps, bs=8):
    db, tq, _ = q3.shape
    wbuf = cache_k.shape[1]
    assert steps <= SUBLANES and knf3.shape[1] == SUBLANES and db % bs == 0
    blk = lambda r, w: pl.BlockSpec((bs, r, w), lambda b: (b, 0, 0))
    cache = jax.ShapeDtypeStruct((db, wbuf, D_KV), F32)
    return pl.pallas_call(
        functools.partial(_attn_sample_kernel, bs=bs, tq=tq, wbuf=wbuf, steps=steps),
        out_shape=(jax.ShapeDtypeStruct((db, tq, D_Q), BF16), cache, cache),
        grid=(db // bs,),
        in_specs=[pl.BlockSpec(memory_space=pltpu.SMEM), blk(tq, D_Q), blk(tq, D_KV), blk(tq, D_KV),
                  blk(SUBLANES, D_KV), blk(SUBLANES, D_KV), blk(wbuf, D_KV), blk(wbuf, D_KV)],
        out_specs=(blk(tq, D_Q), blk(wbuf, D_KV), blk(wbuf, D_KV)),
        compiler_params=_params("parallel"),
        name="attn_sample",
    )(sinks, q3, kn3, vn3, knf3, vnf3, cache_k, cache_v)


def _merge_kernel(yr_ref, o_ref, ga_ref, gb_ref, wr_ref, wa_ref, m_ref):
    a = _dot(yr_ref[...], wr_ref[...])
    b = _dot(o_ref[...], wa_ref[...])
    m_ref[...] = (jax.nn.sigmoid(ga_ref[...]) * a + jax.nn.sigmoid(gb_ref[...]) * b).astype(BF16)


def _merge(yr, o, z, wr, wa, tm=512, tn=1024):
    n = yr.shape[0]
    tm = min(tm, n)
    nj = D_MODEL // tn
    return pl.pallas_call(
        _merge_kernel,
        out_shape=jax.ShapeDtypeStruct((n, D_MODEL), BF16),
        grid=(n // tm, nj),
        in_specs=[
            pl.BlockSpec((tm, D_RNN), lambda i, j: (i, 0)),
            pl.BlockSpec((tm, D_Q), lambda i, j: (i, 0)),
            pl.BlockSpec((tm, tn), lambda i, j: (i, OFF_GA // tn + j)),
            pl.BlockSpec((tm, tn), lambda i, j: (i, OFF_GB // tn + j)),
            pl.BlockSpec((D_RNN, tn), lambda i, j: (0, j)),
            pl.BlockSpec((D_Q, tn), lambda i, j: (0, j)),
        ],
        out_specs=pl.BlockSpec((tm, tn), lambda i, j: (i, j)),
        compiler_params=_params("parallel", "arbitrary"),
        name="merge",
    )(yr, o, z, z, wr, wa)


def _out_proj_kernel(x_ref, m_ref, w_ref, g_ref, x1_ref, n2_ref):
    x1 = x_ref[...] + _dot(m_ref[...], w_ref[...])
    x1_ref[...] = x1
    n2_ref[...] = _rmsnorm_rows(x1, g_ref[...]).astype(BF16)


def _out_proj(x, m, w, g, tm=512):
    n = x.shape[0]
    tm = min(tm, n)
    row = lambda dt: pl.BlockSpec((tm, D_MODEL), lambda i: (i, 0))
    return pl.pallas_call(
        _out_proj_kernel,
        out_shape=(jax.ShapeDtypeStruct((n, D_MODEL), F32), jax.ShapeDtypeStruct((n, D_MODEL), BF16)),
        grid=(n // tm,),
        in_specs=[row(F32), row(BF16), pl.BlockSpec((D_MODEL, D_MODEL), lambda i: (0, 0)),
                  pl.BlockSpec((1, D_MODEL), lambda i: (0, 0))],
        out_specs=(row(F32), row(BF16)),
        compiler_params=_params("parallel"),
        name="out_proj",
    )(x, m, w, g)


def _topk_rows(s, k, rid=None):
    rows, t = s.shape
    if rid is None:
        rid = lax.broadcasted_iota(jnp.int32, (rows, t), 0).astype(F32)
    slot = lax.broadcasted_iota(jnp.int32, (k, t), 0)
    vals = jnp.zeros((k, t), F32)
    ids = jnp.zeros((k, t), F32)
    for r in range(k):
        m = jnp.max(s, axis=0, keepdims=True)
        i = jnp.min(jnp.where(s == m, rid, jnp.inf), axis=0, keepdims=True)
        vals = jnp.where(slot == r, m, vals)
        ids = jnp.where(slot == r, i, ids)
        s = jnp.where(rid == i, -jnp.inf, s)
    return vals, ids


def _peer_route_kernel(n2_ref, wq_ref, sk_ref, idx_ref, gate_ref):
    q = _dot(n2_ref[...], wq_ref[...]).astype(BF16)
    gates, experts = [], []
    for h in range(PEER_HEADS):
        top = []
        for p in range(2):
            c = (2 * h + p) * D_HALF_KEY
            s = _dot_nt(sk_ref[2 * h + p], q[:, c:c + D_HALF_KEY])
            top.append(_topk_rows(s, PEER_TOPK))
        (s1, i1), (s2, i2) = top
        hk = PEER_TOPK // 2
        tcol = s1.shape[1]
        brow = lambda n: lax.broadcasted_iota(jnp.int32, (n, tcol), 0).astype(F32)
        cand = [s1[0:1, :] + s2]
        cidx = [i1[0:1, :] * float(N_KEYS) + i2]
        rid = [brow(PEER_TOPK)]
        for a in range(1, hk):
            cand.append(s1[a:a + 1, :] + s2[0:hk, :])
            cidx.append(i1[a:a + 1, :] * float(N_KEYS) + i2[0:hk, :])
            rid.append(brow(hk) + float(a * PEER_TOPK))
        cand.append(s1[hk:, :] + s2[0:1, :])
        cidx.append(i1[hk:, :] * float(N_KEYS) + i2[0:1, :])
        rid.append((brow(hk) + float(hk)) * float(PEER_TOPK))
        cand, cidx, rid = (jnp.concatenate(x, axis=0) for x in (cand, cidx, rid))
        top_s, pos = _topk_rows(cand, PEER_TOPK, rid)
        slot = lax.broadcasted_iota(jnp.int32, top_s.shape, 0)
        ids = jnp.zeros(top_s.shape, F32)
        for r in range(PEER_TOPK):
            e = jnp.max(jnp.where(rid == pos[r:r + 1, :], cidx, -1.0), axis=0, keepdims=True)
            ids = jnp.where(slot == r, e, ids)
        w = jnp.exp(top_s - top_s[0:1, :])
        gates.append(w / jnp.sum(w, axis=0, keepdims=True))
        experts.append(ids)
    gate_ref[...] = jnp.concatenate(gates, axis=0).T
    idx_ref[...] = jnp.concatenate(experts, axis=0).T.astype(jnp.int32)


def _peer_route(n2, wq, sk, tm=256):
    n = n2.shape[0]
    tm = min(tm, n)
    return pl.pallas_call(
        _peer_route_kernel,
        out_shape=(jax.ShapeDtypeStruct((n, N_PICKS), jnp.int32), jax.ShapeDtypeStruct((n, N_PICKS), F32)),
        grid=(n // tm,),
        in_specs=[pl.BlockSpec((tm, D_MODEL), lambda i: (i, 0)),
                  pl.BlockSpec((D_MODEL, PEER_HEADS * D_KEY), lambda i: (0, 0)),
                  pl.BlockSpec((2 * PEER_HEADS, N_KEYS, D_HALF_KEY), lambda i: (0, 0, 0))],
        out_specs=(pl.BlockSpec((tm, N_PICKS), lambda i: (i, 0)), pl.BlockSpec((tm, N_PICKS), lambda i: (i, 0))),
        compiler_params=_params("parallel"),
        name="peer_route",
    )(n2, wq, sk)


D_TILES = D_MODEL // LANES
assert D_TILES == BF16_TILE_ROWS
PICK_GROUP = 16
WAIT_GROUP = 4


def _peer_mix_kernel(idx_ref, idxn_ref, x_ref, gate_ref, exp_ref, sel_ref, tab_ref, o_ref, buf_a, buf_b,
                     sem_ref, *, tb, nsteps):
    i = pl.program_id(0)
    bufs = (buf_a, buf_b)

    group_rows = WAIT_GROUP * N_PICKS

    def start_token(ids_ref, row, half, t, picks=(0, N_PICKS)):
        for j in range(*picks):
            pltpu.make_async_copy(tab_ref.at[ids_ref[row, j]], bufs[half].at[t * N_PICKS + j],
                                  sem_ref.at[half, t // WAIT_GROUP]).start(priority=j % 2)

    def wait_group(half, g):
        pltpu.make_async_copy(tab_ref.at[pl.ds(0, group_rows)], bufs[half].at[pl.ds(g * group_rows, group_rows)],
                              sem_ref.at[half, g]).wait()

    @pl.when(i == 0)
    def _():
        for t in range(tb):
            start_token(idx_ref, t, 0, t)

    rows = N_PICKS * D_TILES
    diag = (lax.broadcasted_iota(jnp.int32, (D_TILES, rows), 1) % D_TILES
            == lax.broadcasted_iota(jnp.int32, (D_TILES, rows), 0))

    sel = sel_ref[...]
    lane = lax.broadcasted_iota(jnp.int32, (N_PICKS, LANES), 1)

    first_picks, second_picks = (0, N_PICKS // 2), (N_PICKS // 2, N_PICKS)

    def run_tile(half, start_other):
        buf = bufs[half]
        hcols = jnp.zeros((N_PICKS, LANES), F32)
        for t in range(tb):
            if t % WAIT_GROUP == 0:
                wait_group(half, t // WAIT_GROUP)
            start_other(t, first_picks)
            u3 = buf[t * N_PICKS:(t + 1) * N_PICKS, 0:D_TILES, :]
            p = (u3 * x_ref[half * tb + t][None, :, :]).reshape(rows, LANES)
            grp = PICK_GROUP * D_TILES
            r = jnp.concatenate([_dot(sel, p[c * grp:(c + 1) * grp]) for c in range(N_PICKS // PICK_GROUP)],
                                axis=0)
            hcols = jnp.where(lane == t, jnp.sum(r, axis=1, keepdims=True), hcols)
        h = hcols.T[0:tb, :]
        w = (jax.nn.gelu(h) * gate_ref[half * tb:(half + 1) * tb, :]).astype(BF16)
        wrow = _dot(w, exp_ref[...])
        for t in range(tb):
            start_other(t, second_picks)
            v = buf[t * N_PICKS:(t + 1) * N_PICKS, D_TILES:2 * D_TILES, :].reshape(rows, LANES)
            wm = jnp.where(diag, wrow[t:t + 1, :], 0.0).astype(BF16)
            o_ref[half * tb + t] = _dot(wm, v)

    run_tile(0, lambda t, picks: start_token(idx_ref, tb + t, 1, t, picks))
    run_tile(1, lambda t, picks: start_token(idxn_ref, t, 0, t, picks))

    @pl.when(i == nsteps - 1)
    def _():
        for g in range(tb // WAIT_GROUP):
            wait_group(0, g)


def _peer_mix(idx, gates, n2, table, expand, sel, tb=BF16_TILE_ROWS):
    n = idx.shape[0]
    nsteps = n // (2 * tb)
    x3 = n2.reshape(n, D_TILES, LANES)
    rows = N_PICKS * D_TILES
    out = pl.pallas_call(
        functools.partial(_peer_mix_kernel, tb=tb, nsteps=nsteps),
        out_shape=jax.ShapeDtypeStruct((n, D_TILES, LANES), F32),
        grid=(nsteps,),
        in_specs=[
            pl.BlockSpec((2 * tb, N_PICKS), lambda i: (i, 0), memory_space=pltpu.SMEM),
            pl.BlockSpec((tb, N_PICKS), lambda i: (jnp.minimum(2 * i + 2, 2 * nsteps - 2), 0),
                         memory_space=pltpu.SMEM),
            pl.BlockSpec((2 * tb, D_TILES, LANES), lambda i: (i, 0, 0)),
            pl.BlockSpec((2 * tb, N_PICKS), lambda i: (i, 0)),
            pl.BlockSpec((N_PICKS, rows), lambda i: (0, 0)),
            pl.BlockSpec((PICK_GROUP, PICK_GROUP * D_TILES), lambda i: (0, 0)),
            pl.BlockSpec(memory_space=pl.ANY),
        ],
        out_specs=pl.BlockSpec((2 * tb, D_TILES, LANES), lambda i: (i, 0, 0)),
        scratch_shapes=[pltpu.VMEM((tb * N_PICKS, 2 * D_TILES, LANES), BF16),
                        pltpu.VMEM((tb * N_PICKS, 2 * D_TILES, LANES), BF16),
                        pltpu.SemaphoreType.DMA((2, tb // WAIT_GROUP))],
        compiler_params=_params("arbitrary"),
        name="peer_mix",
    )(idx, idx, x3, gates, expand, sel, table)
    return out.reshape(n, D_MODEL)


def _pack_table_kernel(u_ref, v_ref, o_ref):
    o_ref[:, 0:D_TILES, :] = u_ref[...].astype(BF16)
    o_ref[:, D_TILES:2 * D_TILES, :] = v_ref[...].astype(BF16)


def _pack_table(u3, v3, te=256):
    e = u3.shape[0]
    te = min(te, e)
    blk = pl.BlockSpec((te, D_TILES, LANES), lambda i: (i, 0, 0))
    return pl.pallas_call(
        _pack_table_kernel,
        out_shape=jax.ShapeDtypeStruct((e, 2 * D_TILES, LANES), BF16),
        grid=(e // te,),
        in_specs=[blk, blk],
        out_specs=pl.BlockSpec((te, 2 * D_TILES, LANES), lambda i: (i, 0, 0)),
        compiler_params=_params("parallel"),
        name="pack_table",
    )(u3, v3)


def _ple_kernel(x1_ref, po_ref, ple_ref, g_ref, wp_ref, wg_ref, y_ref):
    x2 = x1_ref[...] + po_ref[...]
    n3 = _rmsnorm_rows(x2, g_ref[...]).astype(BF16)
    emb = _dot(ple_ref[...].astype(BF16), wp_ref[...])
    y_ref[...] = x2 + emb * jax.nn.sigmoid(_dot(n3, wg_ref[...]))


def _ple(x1, po, ple, g, wp, wg, tm=512):
    n = x1.shape[0]
    tm = min(tm, n)
    row = pl.BlockSpec((tm, D_MODEL), lambda i: (i, 0))
    return pl.pallas_call(
        _ple_kernel,
        out_shape=jax.ShapeDtypeStruct((n, D_MODEL), F32),
        grid=(n // tm,),
        in_specs=[row, row, pl.BlockSpec((tm, D_PLE), lambda i: (i, 0)),
                  pl.BlockSpec((1, D_MODEL), lambda i: (0, 0)),
                  pl.BlockSpec((D_PLE, D_MODEL), lambda i: (0, 0)),
                  pl.BlockSpec((D_MODEL, D_MODEL), lambda i: (0, 0))],
        out_specs=row,
        compiler_params=_params("parallel"),
        name="ple",
    )(x1, po, ple, g, wp, wg)


def _rope_tables(pos):
    half = ROT_DIM // 2
    inv = ROPE_THETA ** (-jnp.arange(0, ROT_DIM, 2, dtype=F32) / ROT_DIM)
    ang = pos.astype(F32)[:, None] * inv[None, :]
    cos, sin = jnp.cos(ang), jnp.sin(ang)
    n = pos.shape[0]
    pad = jnp.zeros((n, HEAD_DIM - ROT_DIM), F32)
    zh = jnp.zeros((n, half), F32)
    cos_h = jnp.concatenate([cos, cos, pad + 1.0], axis=1)
    up_h = jnp.concatenate([-sin, zh, pad], axis=1)
    dn_h = jnp.concatenate([zh, sin, pad], axis=1)
    rep = LANES // HEAD_DIM
    return tuple(jnp.tile(a, (1, rep)) for a in (cos_h, up_h, dn_h))


def _block_diag(w):
    eye = jnp.eye(RNN_BLOCKS, dtype=w.dtype)
    return jnp.einsum("ncd,nm->ncmd", w, eye).reshape(D_RNN, D_RNN)


def _head_perm():
    return np.array([j * GROUP + g for g in range(GROUP) for j in range(N_KV_HEADS)])


def _token_pipeline_tail(x, z, yr, o, w, ple):
    m = _merge(yr, o, z, w["proj_rnn"], w["proj_attn"])
    x1, n2 = _out_proj(x, m, w["out"], w["norm_ffn"])
    idx, gate = _peer_route(n2, w["peer_q"], w["sub_keys"])
    po = _peer_mix(idx, gate, n2, w["peer_table"], w["expand"], w["pick_sum"])
    return _ple(x1, po, ple, w["norm_ple"], w["ple"], w["ple_gate"])


def kernel(x_prompt, x_sample, p_prompt, p_sample, state_conv, state_rglru, cache_k, cache_v, norm_mix, w_in, conv_w, conv_b, w_rgate, b_rgate, w_igate, b_igate, lru_lambda, w_proj_rnn, q_norm, k_norm, attn_sinks, w_proj_attn, w_out, norm_ffn, w_peer_q, peer_sub_keys, peer_u, peer_v, w_ple, norm_ple, w_ple_gate):
    depth = w_in.shape[0]
    assert depth == 1
    l = 0
    B, S, _ = x_prompt.shape
    DB, DS, _ = x_sample.shape
    wbuf = cache_k.shape[2]

    hp = _head_perm()
    offs = np.cumsum([0, D_RNN, D_RNN, D_Q, D_KV, D_KV, D_MODEL, D_MODEL])
    xr_c, gr_c, q_c, k_c, v_c, ga_c, gb_c = [np.arange(offs[i], offs[i + 1]) for i in range(7)]
    q_c = q_c.reshape(N_Q_HEADS, HEAD_DIM)[hp].reshape(-1)
    cols = np.concatenate([ga_c, gb_c, xr_c, gr_c, q_c, k_c, v_c])
    row2 = lambda a: a[l].reshape(1, -1)
    w = {
        "proj_rnn": w_proj_rnn[l].astype(BF16),
        "proj_attn": w_proj_attn[l].reshape(N_Q_HEADS, HEAD_DIM, D_MODEL)[hp].reshape(D_Q, D_MODEL).astype(BF16),
        "out": w_out[l].astype(BF16),
        "norm_ffn": row2(norm_ffn),
        "peer_q": w_peer_q[l].astype(BF16),
        "sub_keys": peer_sub_keys[l].reshape(2 * PEER_HEADS, N_KEYS, D_HALF_KEY).astype(BF16),
        "peer_table": _pack_table(peer_u[l].reshape(N_EXPERTS, D_TILES, LANES),
                                  peer_v[l].reshape(N_EXPERTS, D_TILES, LANES)),
        "norm_ple": row2(norm_ple),
        "ple": w_ple[l].astype(BF16),
        "ple_gate": w_ple_gate[l].astype(BF16),
    }
    expand = np.repeat(np.eye(N_PICKS, dtype=np.float32), D_TILES, axis=1)
    w["expand"] = jnp.asarray(expand, BF16)
    w["pick_sum"] = jnp.asarray(expand[:PICK_GROUP, :PICK_GROUP * D_TILES], BF16)
    runs = np.split(cols, np.flatnonzero(np.diff(cols) != 1) + 1)
    w_in_b = jnp.concatenate([w_in[l][:, r[0]:r[-1] + 1] for r in runs], axis=1).astype(BF16)
    g_mix = row2(norm_mix)
    cw, cb = conv_w[l], row2(conv_b)
    wr, br = _block_diag(w_rgate[l]).astype(BF16), row2(b_rgate)
    wi, bi = _block_diag(w_igate[l]).astype(BF16), row2(b_igate)
    lam = row2(lru_lambda)
    rep = LANES // HEAD_DIM
    qg = jnp.tile(q_norm[l], rep).reshape(1, LANES)
    kg = jnp.tile(k_norm[l], rep).reshape(1, LANES)
    seg = lambda width: jnp.asarray(
        np.kron(np.eye(width // HEAD_DIM, dtype=np.float32), np.ones((HEAD_DIM, HEAD_DIM), np.float32)), BF16)
    ones_q, ones_k = seg(D_Q), seg(D_KV)
    sinks = attn_sinks[l]

    xp = x_prompt.reshape(B * S, D_MODEL)
    zp = _in_proj(xp, g_mix, w_in_b)
    yr_p, h_p = _rnn_prompt(zp, B, S, cw, cb, wr, br, wi, bi, lam)
    tabs_p = _rope_tables(jnp.arange(S, dtype=jnp.int32))
    q_p, kf_p, kb_p, vb_p = _qk_prep(zp, tabs_p, qg, kg, ones_q, ones_k)
    o_p = _attn_prompt(sinks, q_p, kb_p, vb_p, B, S)
    y_p = _token_pipeline_tail(xp, zp, yr_p, o_p, w, p_prompt[l].reshape(B * S, D_PLE))

    zp3 = zp.reshape(B, S, D_IN)
    keep = min(WINDOW, S)
    prompt_conv = zp3[:, S - (CONV_W - 1):, OFF_XR:OFF_XR + D_RNN]
    prompt_k = kf_p.reshape(B, S, N_KV_HEADS, HEAD_DIM)[:, S - keep:]
    prompt_v = zp3[:, S - keep:, OFF_V:OFF_V + D_KV].reshape(B, keep, N_KV_HEADS, HEAD_DIM)

    ns = DB * DS
    xs = x_sample.reshape(ns, D_MODEL)
    zs = _in_proj(xs, g_mix, w_in_b)
    zs3 = zs.reshape(DB, DS, D_IN)
    tmaj = lambda a: jnp.transpose(a, (1, 0, 2))
    yr_s_t, h_s = _rnn_sample(tmaj(zs3[:, :, OFF_XR:OFF_XR + D_RNN]), tmaj(zs3[:, :, OFF_GR:OFF_GR + D_RNN]),
                              tmaj(state_conv[l]), state_rglru[l], cw, cb, wr, br, wi, bi, lam)
    yr_s = tmaj(yr_s_t).reshape(ns, D_RNN)
    pos_s = PAST_LEN + jnp.arange(DS, dtype=jnp.int32)
    tabs_s = tuple(jnp.tile(a, (DB, 1)) for a in _rope_tables(pos_s))
    q_s, kf_s, kb_s, vb_s = _qk_prep(zs, tabs_s, qg, kg, ones_q, ones_k)
    tq = BF16_TILE_ROWS
    pad_t = lambda a, rows: jnp.pad(a.reshape(DB, DS, -1), ((0, 0), (0, rows - DS), (0, 0)))
    ck = cache_k[l].reshape(DB, wbuf, D_KV)
    cv = cache_v[l].reshape(DB, wbuf, D_KV)
    o_s, sample_k, sample_v = _attn_sample(
        sinks, pad_t(q_s, tq), pad_t(kb_s, tq), pad_t(vb_s, tq), pad_t(kf_s, SUBLANES),
        pad_t(zs3[:, :, OFF_V:OFF_V + D_KV], SUBLANES), ck, cv, DS)
    o_s = o_s[:, :DS].reshape(ns, D_Q)
    y_s = _token_pipeline_tail(xs, zs, yr_s, o_s, w, p_sample[l].reshape(ns, D_PLE))

    sample_conv = jnp.concatenate([state_conv[l], zs3[:, :, OFF_XR:OFF_XR + D_RNN]], axis=1)[:, DS:]
    sample_k = sample_k.reshape(DB, wbuf, N_KV_HEADS, HEAD_DIM)
    sample_v = sample_v.reshape(DB, wbuf, N_KV_HEADS, HEAD_DIM)

    return (y_p.reshape(B, S, D_MODEL), y_s.reshape(DB, DS, D_MODEL),
            prompt_conv[None], h_p.reshape(1, B, D_RNN), prompt_k[None], prompt_v[None],
            sample_conv[None], h_s[None], sample_k[None], sample_v[None])
```

```python
import functools

import jax
import jax.numpy as jnp
import numpy as np
from jax import lax
from jax.experimental import pallas as pl
from jax.experimental.pallas import tpu as pltpu

D_MODEL = 2048
D_RNN = D_MODEL // 2
RNN_BLOCKS = 8
RNN_BLOCK = D_RNN // RNN_BLOCKS
CONV_W = 4
LRU_C = 8.0
HEAD_DIM = 64
N_Q_HEADS = D_MODEL // 2 // HEAD_DIM
N_KV_HEADS = 4
GROUP = N_Q_HEADS // N_KV_HEADS
D_Q = N_Q_HEADS * HEAD_DIM
D_KV = N_KV_HEADS * HEAD_DIM
WINDOW = 128
ROT_DIM = HEAD_DIM // 4
ROPE_THETA = 500000.0
N_KEYS = 128
N_EXPERTS = N_KEYS * N_KEYS
PEER_HEADS = 8
PEER_TOPK = 16
D_KEY = 256
D_HALF_KEY = D_KEY // 2
N_PICKS = PEER_HEADS * PEER_TOPK
D_PLE = 256
EPS = 1e-6
NEG_INF = -1e30
PAST_LEN = 16384

LANES = 128
SUBLANES = 8
BF16_TILE_ROWS = 16
VMEM_LIMIT_BYTES = 56 * 1024 * 1024

OFF_GA = 0
OFF_GB = OFF_GA + D_MODEL
OFF_XR = OFF_GB + D_MODEL
OFF_GR = OFF_XR + D_RNN
OFF_Q = OFF_GR + D_RNN
OFF_K = OFF_Q + D_Q
OFF_V = OFF_K + D_KV
D_IN = OFF_V + D_KV

BF16 = jnp.bfloat16
F32 = jnp.float32


def _params(*sem):
    return pltpu.CompilerParams(dimension_semantics=sem, vmem_limit_bytes=VMEM_LIMIT_BYTES)


def _rmsnorm_rows(x, g):
    return x * lax.rsqrt(jnp.mean(x * x, axis=-1, keepdims=True) + EPS) * g


def _dot(a, b):
    return jnp.dot(a, b, preferred_element_type=F32)


def _dot_nt(a, b):
    return lax.dot_general(a, b, (((1,), (1,)), ((), ())), preferred_element_type=F32)


def _in_proj_kernel(x_ref, g_ref, w_ref, z_ref, xn_ref):
    @pl.when(pl.program_id(1) == 0)
    def _():
        xn_ref[...] = _rmsnorm_rows(x_ref[...], g_ref[...]).astype(BF16)

    z_ref[...] = _dot(xn_ref[...], w_ref[...])


def _in_proj(x, g, w, tm=1024, tn=1536):
    n = x.shape[0]
    tm = min(tm, n)
    return pl.pallas_call(
        _in_proj_kernel,
        out_shape=jax.ShapeDtypeStruct((n, D_IN), F32),
        grid=(n // tm, D_IN // tn),
        in_specs=[
            pl.BlockSpec((tm, D_MODEL), lambda i, j: (i, 0)),
            pl.BlockSpec((1, D_MODEL), lambda i, j: (0, 0)),
            pl.BlockSpec((D_MODEL, tn), lambda i, j: (0, j)),
        ],
        out_specs=pl.BlockSpec((tm, tn), lambda i, j: (i, j)),
        scratch_shapes=[pltpu.VMEM((tm, D_MODEL), BF16)],
        compiler_params=_params("parallel", "arbitrary"),
        name="in_proj",
    )(x, g, w)


def _softplus(x):
    return jnp.maximum(x, 0.0) + jnp.log(1.0 + jnp.exp(-jnp.abs(x)))


def _neg_expm1(x):
    t = jnp.tanh(-0.5 * x)
    return 2.0 * t / (1.0 + t)


def _lru_coeffs(xc, wr, br, wi, bi, lam, first_is_pos0):
    xb = xc.astype(BF16)
    r = jax.nn.sigmoid(_dot(xb, wr) + br)
    i = jax.nn.sigmoid(_dot(xb, wi) + bi)
    log_a = -LRU_C * r * _softplus(-lam)
    a = jnp.exp(log_a)
    mult = jnp.sqrt(_neg_expm1(2.0 * log_a))
    if first_is_pos0 is not None:
        row = lax.broadcasted_iota(jnp.int32, xc.shape, 0)
        mult = jnp.where(jnp.logical_and(first_is_pos0, row == 0), 1.0, mult)
    return a, mult * i * xc


def _rnn_prompt_kernel(xr_ref, gr_ref, cw_ref, cb_ref, wr_ref, br_ref, wi_ref, bi_ref, lam_ref,
                       yr_ref, hlast_ref, tail_ref, h_ref, *, tb):
    t = pl.program_id(1)

    @pl.when(t == 0)
    def _():
        tail_ref[...] = jnp.zeros_like(tail_ref)
        h_ref[...] = jnp.zeros_like(h_ref)

    x = xr_ref[...]
    tail = tail_ref[...]
    row8 = lax.broadcasted_iota(jnp.int32, (SUBLANES, D_RNN), 0)
    cw = cw_ref[...]
    xc = cb_ref[...] + cw[CONV_W - 1:CONV_W, :] * x
    for k in range(1, CONV_W):
        xs = pltpu.roll(x, k, axis=0)
        head = jnp.where(row8 < k, pltpu.roll(tail, k, axis=0), xs[:SUBLANES])
        xs = jnp.concatenate([head, xs[SUBLANES:]], axis=0)
        xc = xc + cw[CONV_W - 1 - k:CONV_W - k, :] * xs
    tail_ref[...] = x[tb - SUBLANES:, :]

    a, b = _lru_coeffs(xc, wr_ref[...], br_ref[...], wi_ref[...], bi_ref[...], lam_ref[...], t == 0)

    ng = tb // SUBLANES
    a3 = a.reshape(ng, SUBLANES, D_RNN)
    b3 = b.reshape(ng, SUBLANES, D_RNN)
    sub = lax.broadcasted_iota(jnp.int32, (ng, SUBLANES, D_RNN), 1)
    d = 1
    while d < SUBLANES:
        a_sh = pltpu.roll(a3, d, axis=1)
        b_sh = pltpu.roll(b3, d, axis=1)
        keep = sub < d
        b3 = jnp.where(keep, b3, a3 * b_sh + b3)
        a3 = jnp.where(keep, a3, a3 * a_sh)
        d *= 2
    carry = h_ref[...]
    hs = []
    for g in range(ng):
        hs.append(b3[g] + a3[g] * carry)
        carry = hs[-1][SUBLANES - 1:SUBLANES, :]
    h_ref[...] = carry
    hlast_ref[0] = carry
    yr_ref[...] = (jnp.concatenate(hs, axis=0) * jax.nn.gelu(gr_ref[...])).astype(BF16)


def _rnn_prompt(z, batch, seq, cw, cb, wr, br, wi, bi, lam, tb=256):
    nt = seq // tb
    vec = pl.BlockSpec((1, D_RNN), lambda b, t: (0, 0))
    mat = pl.BlockSpec((D_RNN, D_RNN), lambda b, t: (0, 0))
    return pl.pallas_call(
        functools.partial(_rnn_prompt_kernel, tb=tb),
        out_shape=(jax.ShapeDtypeStruct((batch * seq, D_RNN), BF16),
                   jax.ShapeDtypeStruct((batch, 1, D_RNN), F32)),
        grid=(batch, nt),
        in_specs=[
            pl.BlockSpec((tb, D_RNN), lambda b, t: (b * nt + t, OFF_XR // D_RNN)),
            pl.BlockSpec((tb, D_RNN), lambda b, t: (b * nt + t, OFF_GR // D_RNN)),
            pl.BlockSpec((CONV_W, D_RNN), lambda b, t: (0, 0)),
            vec, mat, vec, mat, vec, vec,
        ],
        out_specs=(pl.BlockSpec((tb, D_RNN), lambda b, t: (b * nt + t, 0)),
                   pl.BlockSpec((1, 1, D_RNN), lambda b, t: (b, 0, 0))),
        scratch_shapes=[pltpu.VMEM((SUBLANES, D_RNN), F32), pltpu.VMEM((1, D_RNN), F32)],
        compiler_params=_params("parallel", "arbitrary"),
        name="rnn_prompt",
    )(z, z, cw, cb, wr, br, wi, bi, lam)


def _rnn_sample_kernel(xr_ref, gr_ref, buf_ref, h0_ref, cw_ref, cb_ref, wr_ref, br_ref, wi_ref,
                       bi_ref, lam_ref, yr_ref, hlast_ref, *, steps):
    cw = cw_ref[...]
    xp = [buf_ref[k] for k in range(CONV_W - 1)] + [xr_ref[s] for s in range(steps)]
    h = h0_ref[...]
    for s in range(steps):
        xc = cb_ref[...] + sum(cw[k:k + 1, :] * xp[s + k] for k in range(CONV_W))
        a, b = _lru_coeffs(xc, wr_ref[...], br_ref[...], wi_ref[...], bi_ref[...], lam_ref[...], None)
        h = a * h + b
        yr_ref[s] = (h * jax.nn.gelu(gr_ref[s])).astype(BF16)
    hlast_ref[...] = h


def _rnn_sample(xr_t, gr_t, buf_t, h0, cw, cb, wr, br, wi, bi, lam):
    steps, db, _ = xr_t.shape
    return pl.pallas_call(
        functools.partial(_rnn_sample_kernel, steps=steps),
        out_shape=(jax.ShapeDtypeStruct((steps, db, D_RNN), BF16),
                   jax.ShapeDtypeStruct((db, D_RNN), F32)),
        compiler_params=pltpu.CompilerParams(vmem_limit_bytes=VMEM_LIMIT_BYTES),
        name="rnn_sample",
    )(xr_t, gr_t, buf_t, h0, cw, cb, wr, br, wi, bi, lam)


def _headnorm_rope(x, gain, seg_ones, cos_t, sin_up_t, sin_dn_t, scale):
    w = x.shape[1]
    sq = x * x
    hi = sq.astype(BF16)
    lo = (sq - hi.astype(F32)).astype(BF16)
    ms = (_dot(hi, seg_ones) + _dot(lo, seg_ones)) * (1.0 / HEAD_DIM)
    xn = x * lax.rsqrt(ms + EPS)
    outs = []
    for c in range(w // LANES):
        xt = xn[:, c * LANES:(c + 1) * LANES] * gain
        up = pltpu.roll(xt, LANES - ROT_DIM // 2, axis=1)
        dn = pltpu.roll(xt, ROT_DIM // 2, axis=1)
        outs.append((xt * cos_t + up * sin_up_t + dn * sin_dn_t) * scale)
    return jnp.concatenate(outs, axis=1)


def _qk_prep_kernel(q_ref, k_ref, v_ref, cos_ref, sup_ref, sdn_ref, qg_ref, kg_ref, oq_ref, ok_ref,
                    qo_ref, kf_ref, kb_ref, vb_ref):
    cos_t, sup, sdn = cos_ref[...], sup_ref[...], sdn_ref[...]
    q = _headnorm_rope(q_ref[...], qg_ref[...], oq_ref[...], cos_t, sup, sdn, HEAD_DIM ** -0.5)
    q = jnp.concatenate([q[:, h * HEAD_DIM:(h + 1) * HEAD_DIM] for h in _head_perm()], axis=1)
    qo_ref[...] = q.astype(BF16)
    k = _headnorm_rope(k_ref[...], kg_ref[...], ok_ref[...], cos_t, sup, sdn, 1.0)
    kf_ref[...] = k
    kb_ref[...] = k.astype(BF16)
    vb_ref[...] = v_ref[...].astype(BF16)


def _qk_prep(z, tabs, qg, kg, ones_q, ones_k, tm=512):
    n = z.shape[0]
    tm = min(tm, n)
    cos_t, sup_t, sdn_t = tabs
    ntab = cos_t.shape[0] // tm
    tab = pl.BlockSpec((tm, LANES), lambda i: (i % ntab, 0))
    const = lambda shape: pl.BlockSpec(shape, lambda i: (0, 0))
    return pl.pallas_call(
        _qk_prep_kernel,
        out_shape=(jax.ShapeDtypeStruct((n, D_Q), BF16), jax.ShapeDtypeStruct((n, D_KV), F32),
                   jax.ShapeDtypeStruct((n, D_KV), BF16), jax.ShapeDtypeStruct((n, D_KV), BF16)),
        grid=(n // tm,),
        in_specs=[
            pl.BlockSpec((tm, D_Q), lambda i: (i, OFF_Q // D_Q)),
            pl.BlockSpec((tm, D_KV), lambda i: (i, OFF_K // D_KV)),
            pl.BlockSpec((tm, D_KV), lambda i: (i, OFF_V // D_KV)),
            tab, tab, tab,
            const((1, LANES)), const((1, LANES)), const((D_Q, D_Q)), const((D_KV, D_KV)),
        ],
        out_specs=(pl.BlockSpec((tm, D_Q), lambda i: (i, 0)), pl.BlockSpec((tm, D_KV), lambda i: (i, 0)),
                   pl.BlockSpec((tm, D_KV), lambda i: (i, 0)), pl.BlockSpec((tm, D_KV), lambda i: (i, 0))),
        compiler_params=_params("parallel"),
        name="qk_prep",
    )(z, z, z, cos_t, sup_t, sdn_t, qg, kg, ones_q, ones_k)


def _sink_attention(q, k, v, valid, sink_ref, tq):
    lane_head = lax.broadcasted_iota(jnp.int32, (tq, D_KV), 1) // HEAD_DIM
    rowg = lax.broadcasted_iota(jnp.int32, (GROUP * tq, 1), 0) // tq
    validg = jnp.concatenate([valid] * GROUP, axis=0)
    out = [jnp.zeros((tq, D_KV), F32) for _ in range(GROUP)]
    for j in range(N_KV_HEADS):
        sel = lane_head == j
        keep = jnp.where(sel, 1.0, 0.0).astype(BF16)
        qs = jnp.concatenate([q[:, g * D_KV:(g + 1) * D_KV] * keep for g in range(GROUP)], axis=0)
        s = jnp.where(validg, _dot_nt(qs, k), NEG_INF)
        sk = jnp.zeros((GROUP * tq, 1), F32)
        for g in range(GROUP):
            sk = jnp.where(rowg == g, sink_ref[j * GROUP + g], sk)
        m = jnp.maximum(jnp.max(s, axis=-1, keepdims=True), sk)
        p = jnp.exp(s - m)
        denom = jnp.sum(p, axis=-1, keepdims=True) + jnp.exp(sk - m)
        pv = _dot(p.astype(BF16), v) / denom
        for g in range(GROUP):
            out[g] = jnp.where(sel, pv[g * tq:(g + 1) * tq], out[g])
    return jnp.concatenate(out, axis=1)


def _attn_prompt_kernel(sink_ref, q_ref, kp_ref, kc_ref, vp_ref, vc_ref, o_ref):
    nb = pl.program_id(1)
    k = jnp.concatenate([kp_ref[...], kc_ref[...]], axis=0)
    v = jnp.concatenate([vp_ref[...], vc_ref[...]], axis=0)
    i = lax.broadcasted_iota(jnp.int32, (WINDOW, 2 * WINDOW), 0)
    j = lax.broadcasted_iota(jnp.int32, (WINDOW, 2 * WINDOW), 1)
    d = WINDOW + i - j
    valid = (d >= 0) & (d < WINDOW) & ((j >= WINDOW) | (nb > 0))
    o_ref[...] = _sink_attention(q_ref[...], k, v, valid, sink_ref, WINDOW).astype(BF16)


def _attn_prompt(sinks, q, kb, vb, batch, seq):
    nblk = seq // WINDOW
    cur = lambda w: pl.BlockSpec((WINDOW, w), lambda b, t: (b * nblk + t, 0))
    prev = lambda w: pl.BlockSpec((WINDOW, w), lambda b, t: (b * nblk + jnp.maximum(t - 1, 0), 0))
    return pl.pallas_call(
        _attn_prompt_kernel,
        out_shape=jax.ShapeDtypeStruct((batch * seq, D_Q), BF16),
        grid=(batch, nblk),
        in_specs=[pl.BlockSpec(memory_space=pltpu.SMEM), cur(D_Q), prev(D_KV), cur(D_KV),
                  prev(D_KV), cur(D_KV)],
        out_specs=cur(D_Q),
        compiler_params=_params("parallel", "arbitrary"),
        name="attn_prompt",
    )(sinks, q, kb, kb, vb, vb)


def _shift_in(cache, new8, steps):
    wbuf = cache.shape[0]
    rolled = pltpu.roll(cache, wbuf - steps, axis=0)
    row8 = lax.broadcasted_iota(jnp.int32, new8.shape, 0)
    tail = jnp.where(row8 >= SUBLANES - steps, pltpu.roll(new8, SUBLANES - steps, axis=0),
                     rolled[wbuf - SUBLANES:])
    return jnp.concatenate([rolled[:wbuf - SUBLANES], tail], axis=0)


def _attn_sample_kernel(sink_ref, q_ref, kn_ref, vn_ref, knf_ref, vnf_ref, ck_ref, cv_ref, o_ref, ok_ref, ov_ref,
                        *, bs, tq, wbuf, steps):
    t = lax.broadcasted_iota(jnp.int32, (tq, wbuf + tq), 0)
    c = lax.broadcasted_iota(jnp.int32, (tq, wbuf + tq), 1)
    d = wbuf + t - c
    valid = (d >= 0) & (d < WINDOW)
    for s in range(bs):
        ck, cv = ck_ref[s], cv_ref[s]
        k = jnp.concatenate([ck.astype(BF16), kn_ref[s]], axis=0)
        v = jnp.concatenate([cv.astype(BF16), vn_ref[s]], axis=0)
        o_ref[s] = _sink_attention(q_ref[s], k, v, valid, sink_ref, tq).astype(BF16)
        ok_ref[s] = _shift_in(ck, knf_ref[s], steps)
        ov_ref[s] = _shift_in(cv, vnf_ref[s], steps)


def _attn_sample(sinks, q3, kn3, vn3, knf3, vnf3, cache_k, cache_v, steps, bs=8):
    db, tq, _ = q3.shape
    wbuf = cache_k.shape[1]
    assert steps <= SUBLANES and knf3.shape[1] == SUBLANES and db % bs == 0
    blk = lambda r, w: pl.BlockSpec((bs, r, w), lambda b: (b, 0, 0))
    cache = jax.ShapeDtypeStruct((db, wbuf, D_KV), F32)
    return pl.pallas_call(
        functools.partial(_attn_sample_kernel, bs=bs, tq=tq, wbuf=wbuf, steps=steps),
        out_shape=(jax.ShapeDtypeStruct((db, tq, D_Q), BF16), cache, cache),
        grid=(db // bs,),
        in_specs=[pl.BlockSpec(memory_space=pltpu.SMEM), blk(tq, D_Q), blk(tq, D_KV), blk(tq, D_KV),
                  blk(SUBLANES, D_KV), blk(SUBLANES, D_KV), blk(wbuf, D_KV), blk(wbuf, D_KV)],
        out_specs=(blk(tq, D_Q), blk(wbuf, D_KV), blk(wbuf, D_KV)),
        compiler_params=_params("parallel"),
        name="attn_sample",
    )(sinks, q3, kn3, vn3, knf3, vnf3, cache_k, cache_v)


def _merge_kernel(yr_ref, o_ref, ga_ref, gb_ref, wr_ref, wa_ref, m_ref):
    a = _dot(yr_ref[...], wr_ref[...])
    b = _dot(o_ref[...], wa_ref[...])
    m_ref[...] = (jax.nn.sigmoid(ga_ref[...]) * a + jax.nn.sigmoid(gb_ref[...]) * b).astype(BF16)


def _merge(yr, o, z, wr, wa, tm=512, tn=1024):
    n = yr.shape[0]
    tm = min(tm, n)
    nj = D_MODEL // tn
    return pl.pallas_call(
        _merge_kernel,
        out_shape=jax.ShapeDtypeStruct((n, D_MODEL), BF16),
        grid=(n // tm, nj),
        in_specs=[
            pl.BlockSpec((tm, D_RNN), lambda i, j: (i, 0)),
            pl.BlockSpec((tm, D_Q), lambda i, j: (i, 0)),
            pl.BlockSpec((tm, tn), lambda i, j: (i, OFF_GA // tn + j)),
            pl.BlockSpec((tm, tn), lambda i, j: (i, OFF_GB // tn + j)),
            pl.BlockSpec((D_RNN, tn), lambda i, j: (0, j)),
            pl.BlockSpec((D_Q, tn), lambda i, j: (0, j)),
        ],
        out_specs=pl.BlockSpec((tm, tn), lambda i, j: (i, j)),
        compiler_params=_params("parallel", "arbitrary"),
        name="merge",
    )(yr, o, z, z, wr, wa)


def _out_proj_kernel(x_ref, m_ref, w_ref, g_ref, x1_ref, n2_ref):
    x1 = x_ref[...] + _dot(m_ref[...], w_ref[...])
    x1_ref[...] = x1
    n2_ref[...] = _rmsnorm_rows(x1, g_ref[...]).astype(BF16)


def _out_proj(x, m, w, g, tm=512):
    n = x.shape[0]
    tm = min(tm, n)
    row = lambda dt: pl.BlockSpec((tm, D_MODEL), lambda i: (i, 0))
    return pl.pallas_call(
        _out_proj_kernel,
        out_shape=(jax.ShapeDtypeStruct((n, D_MODEL), F32), jax.ShapeDtypeStruct((n, D_MODEL), BF16)),
        grid=(n // tm,),
        in_specs=[row(F32), row(BF16), pl.BlockSpec((D_MODEL, D_MODEL), lambda i: (0, 0)),
                  pl.BlockSpec((1, D_MODEL), lambda i: (0, 0))],
        out_specs=(row(F32), row(BF16)),
        compiler_params=_params("parallel"),
        name="out_proj",
    )(x, m, w, g)


def _topk_rows(s, k, rid=None):
    rows, t = s.shape
    if rid is None:
        rid = lax.broadcasted_iota(jnp.int32, (rows, t), 0).astype(F32)
    slot = lax.broadcasted_iota(jnp.int32, (k, t), 0)
    vals = jnp.zeros((k, t), F32)
    ids = jnp.zeros((k, t), F32)
    for r in range(k):
        m = jnp.max(s, axis=0, keepdims=True)
        i = jnp.min(jnp.where(s == m, rid, jnp.inf), axis=0, keepdims=True)
        vals = jnp.where(slot == r, m, vals)
        ids = jnp.where(slot == r, i, ids)
        s = jnp.where(rid == i, -jnp.inf, s)
    return vals, ids


def _peer_route_kernel(n2_ref, wq_ref, sk_ref, idx_ref, gate_ref):
    q = _dot(n2_ref[...], wq_ref[...]).astype(BF16)
    gates, experts = [], []
    for h in range(PEER_HEADS):
        top = []
        for p in range(2):
            c = (2 * h + p) * D_HALF_KEY
            s = _dot_nt(sk_ref[2 * h + p], q[:, c:c + D_HALF_KEY])
            top.append(_topk_rows(s, PEER_TOPK))
        (s1, i1), (s2, i2) = top
        hk = PEER_TOPK // 2
        tcol = s1.shape[1]
        brow = lambda n: lax.broadcasted_iota(jnp.int32, (n, tcol), 0).astype(F32)
        cand = [s1[0:1, :] + s2]
        cidx = [i1[0:1, :] * float(N_KEYS) + i2]
        rid = [brow(PEER_TOPK)]
        for a in range(1, hk):
            cand.append(s1[a:a + 1, :] + s2[0:hk, :])
            cidx.append(i1[a:a + 1, :] * float(N_KEYS) + i2[0:hk, :])
            rid.append(brow(hk) + float(a * PEER_TOPK))
        cand.append(s1[hk:, :] + s2[0:1, :])
        cidx.append(i1[hk:, :] * float(N_KEYS) + i2[0:1, :])
        rid.append((brow(hk) + float(hk)) * float(PEER_TOPK))
        cand, cidx, rid = (jnp.concatenate(x, axis=0) for x in (cand, cidx, rid))
        top_s, pos = _topk_rows(cand, PEER_TOPK, rid)
        slot = lax.broadcasted_iota(jnp.int32, top_s.shape, 0)
        ids = jnp.zeros(top_s.shape, F32)
        for r in range(PEER_TOPK):
            e = jnp.max(jnp.where(rid == pos[r:r + 1, :], cidx, -1.0), axis=0, keepdims=True)
            ids = jnp.where(slot == r, e, ids)
        w = jnp.exp(top_s - top_s[0:1, :])
        gates.append(w / jnp.sum(w, axis=0, keepdims=True))
        experts.append(ids)
    gate_ref[...] = jnp.concatenate(gates, axis=0).T
    idx_ref[...] = jnp.concatenate(experts, axis=0).T.astype(jnp.int32)


def _peer_route(n2, wq, sk, tm=256):
    n = n2.shape[0]
    tm = min(tm, n)
    return pl.pallas_call(
        _peer_route_kernel,
        out_shape=(jax.ShapeDtypeStruct((n, N_PICKS), jnp.int32), jax.ShapeDtypeStruct((n, N_PICKS), F32)),
        grid=(n // tm,),
        in_specs=[pl.BlockSpec((tm, D_MODEL), lambda i: (i, 0)),
                  pl.BlockSpec((D_MODEL, PEER_HEADS * D_KEY), lambda i: (0, 0)),
                  pl.BlockSpec((2 * PEER_HEADS, N_KEYS, D_HALF_KEY), lambda i: (0, 0, 0))],
        out_specs=(pl.BlockSpec((tm, N_PICKS), lambda i: (i, 0)), pl.BlockSpec((tm, N_PICKS), lambda i: (i, 0))),
        compiler_params=_params("parallel"),
        name="peer_route",
    )(n2, wq, sk)


D_TILES = D_MODEL // LANES
assert D_TILES == BF16_TILE_ROWS
PICK_GROUP = 16
WAIT_GROUP = 4


def _peer_mix_kernel(idx_ref, idxn_ref, x_ref, gate_ref, exp_ref, sel_ref, tab_ref, o_ref, buf_a, buf_b,
                     sem_ref, *, tb, nsteps):
    i = pl.program_id(0)
    bufs = (buf_a, buf_b)

    group_rows = WAIT_GROUP * N_PICKS

    def start_token(ids_ref, row, half, t, picks=(0, N_PICKS)):
        for j in range(*picks):
            pltpu.make_async_copy(tab_ref.at[ids_ref[row, j]], bufs[half].at[t * N_PICKS + j],
                                  sem_ref.at[half, t // WAIT_GROUP]).start(priority=j % 2)

    def wait_group(half, g):
        pltpu.make_async_copy(tab_ref.at[pl.ds(0, group_rows)], bufs[half].at[pl.ds(g * group_rows, group_rows)],
                              sem_ref.at[half, g]).wait()

    @pl.when(i == 0)
    def _():
        for t in range(tb):
            start_token(idx_ref, t, 0, t)

    rows = N_PICKS * D_TILES
    diag = (lax.broadcasted_iota(jnp.int32, (D_TILES, rows), 1) % D_TILES
            == lax.broadcasted_iota(jnp.int32, (D_TILES, rows), 0))

    sel = sel_ref[...]
    lane = lax.broadcasted_iota(jnp.int32, (N_PICKS, LANES), 1)

    first_picks, second_picks = (0, N_PICKS // 2), (N_PICKS // 2, N_PICKS)

    def run_tile(half, start_other):
        buf = bufs[half]
        hcols = jnp.zeros((N_PICKS, LANES), F32)
        for t in range(tb):
            if t % WAIT_GROUP == 0:
                wait_group(half, t // WAIT_GROUP)
            start_other(t, first_picks)
            u3 = buf[t * N_PICKS:(t + 1) * N_PICKS, 0:D_TILES, :]
            p = (u3 * x_ref[half * tb + t][None, :, :]).reshape(rows, LANES)
            grp = PICK_GROUP * D_TILES
            r = jnp.concatenate([_dot(sel, p[c * grp:(c + 1) * grp]) for c in range(N_PICKS // PICK_GROUP)],
                                axis=0)
            hcols = jnp.where(lane == t, jnp.sum(r, axis=1, keepdims=True), hcols)
        h = hcols.T[0:tb, :]
        w = (jax.nn.gelu(h) * gate_ref[half * tb:(half + 1) * tb, :]).astype(BF16)
        wrow = _dot(w, exp_ref[...])
        for t in range(tb):
            start_other(t, second_picks)
            v = buf[t * N_PICKS:(t + 1) * N_PICKS, D_TILES:2 * D_TILES, :].reshape(rows, LANES)
            wm = jnp.where(diag, wrow[t:t + 1, :], 0.0).astype(BF16)
            o_ref[half * tb + t] = _dot(wm, v)

    run_tile(0, lambda t, picks: start_token(idx_ref, tb + t, 1, t, picks))
    run_tile(1, lambda t, picks: start_token(idxn_ref, t, 0, t, picks))

    @pl.when(i == nsteps - 1)
    def _():
        for g in range(tb // WAIT_GROUP):
            wait_group(0, g)


def _peer_mix(idx, gates, n2, table, expand, sel, tb=BF16_TILE_ROWS):
    n = idx.shape[0]
    nsteps = n // (2 * tb)
    x3 = n2.reshape(n, D_TILES, LANES)
    rows = N_PICKS * D_TILES
    out = pl.pallas_call(
        functools.partial(_peer_mix_kernel, tb=tb, nsteps=nsteps),
        out_shape=jax.ShapeDtypeStruct((n, D_TILES, LANES), F32),
        grid=(nsteps,),
        in_specs=[
            pl.BlockSpec((2 * tb, N_PICKS), lambda i: (i, 0), memory_space=pltpu.SMEM),
            pl.BlockSpec((tb, N_PICKS), lambda i: (jnp.minimum(2 * i + 2, 2 * nsteps - 2), 0),
                         memory_space=pltpu.SMEM),
            pl.BlockSpec((2 * tb, D_TILES, LANES), lambda i: (i, 0, 0)),
            pl.BlockSpec((2 * tb, N_PICKS), lambda i: (i, 0)),
            pl.BlockSpec((N_PICKS, rows), lambda i: (0, 0)),
            pl.BlockSpec((PICK_GROUP, PICK_GROUP * D_TILES), lambda i: (0, 0)),
            pl.BlockSpec(memory_space=pl.ANY),
        ],
        out_specs=pl.BlockSpec((2 * tb, D_TILES, LANES), lambda i: (i, 0, 0)),
        scratch_shapes=[pltpu.VMEM((tb * N_PICKS, 2 * D_TILES, LANES), BF16),
                        pltpu.VMEM((tb * N_PICKS, 2 * D_TILES, LANES), BF16),
                        pltpu.SemaphoreType.DMA((2, tb // WAIT_GROUP))],
        compiler_params=_params("arbitrary"),
        name="peer_mix",
    )(idx, idx, x3, gates, expand, sel, table)
    return out.reshape(n, D_MODEL)


def _pack_table_kernel(u_ref, v_ref, o_ref):
    o_ref[:, 0:D_TILES, :] = u_ref[...].astype(BF16)
    o_ref[:, D_TILES:2 * D_TILES, :] = v_ref[...].astype(BF16)


def _pack_table(u3, v3, te=256):
    e = u3.shape[0]
    te = min(te, e)
    blk = pl.BlockSpec((te, D_TILES, LANES), lambda i: (i, 0, 0))
    return pl.pallas_call(
        _pack_table_kernel,
        out_shape=jax.ShapeDtypeStruct((e, 2 * D_TILES, LANES), BF16),
        grid=(e // te,),
        in_specs=[blk, blk],
        out_specs=pl.BlockSpec((te, 2 * D_TILES, LANES), lambda i: (i, 0, 0)),
        compiler_params=_params("parallel"),
        name="pack_table",
    )(u3, v3)


def _ple_kernel(x1_ref, po_ref, ple_ref, g_ref, wp_ref, wg_ref, y_ref):
    x2 = x1_ref[...] + po_ref[...]
    n3 = _rmsnorm_rows(x2, g_ref[...]).astype(BF16)
    emb = _dot(ple_ref[...].astype(BF16), wp_ref[...])
    y_ref[...] = x2 + emb * jax.nn.sigmoid(_dot(n3, wg_ref[...]))


def _ple(x1, po, ple, g, wp, wg, tm=512):
    n = x1.shape[0]
    tm = min(tm, n)
    row = pl.BlockSpec((tm, D_MODEL), lambda i: (i, 0))
    return pl.pallas_call(
        _ple_kernel,
        out_shape=jax.ShapeDtypeStruct((n, D_MODEL), F32),
        grid=(n // tm,),
        in_specs=[row, row, pl.BlockSpec((tm, D_PLE), lambda i: (i, 0)),
                  pl.BlockSpec((1, D_MODEL), lambda i: (0, 0)),
                  pl.BlockSpec((D_PLE, D_MODEL), lambda i: (0, 0)),
                  pl.BlockSpec((D_MODEL, D_MODEL), lambda i: (0, 0))],
        out_specs=row,
        compiler_params=_params("parallel"),
        name="ple",
    )(x1, po, ple, g, wp, wg)


def _rope_tables(pos):
    half = ROT_DIM // 2
    inv = ROPE_THETA ** (-jnp.arange(0, ROT_DIM, 2, dtype=F32) / ROT_DIM)
    ang = pos.astype(F32)[:, None] * inv[None, :]
    cos, sin = jnp.cos(ang), jnp.sin(ang)
    n = pos.shape[0]
    pad = jnp.zeros((n, HEAD_DIM - ROT_DIM), F32)
    zh = jnp.zeros((n, half), F32)
    cos_h = jnp.concatenate([cos, cos, pad + 1.0], axis=1)
    up_h = jnp.concatenate([-sin, zh, pad], axis=1)
    dn_h = jnp.concatenate([zh, sin, pad], axis=1)
    rep = LANES // HEAD_DIM
    return tuple(jnp.tile(a, (1, rep)) for a in (cos_h, up_h, dn_h))


def _block_diag(w):
    eye = jnp.eye(RNN_BLOCKS, dtype=w.dtype)
    return jnp.einsum("ncd,nm->ncmd", w, eye).reshape(D_RNN, D_RNN)


def _head_perm():
    return np.array([j * GROUP + g for g in range(GROUP) for j in range(N_KV_HEADS)])


def _token_pipeline_tail(x, z, yr, o, w, ple):
    m = _merge(yr, o, z, w["proj_rnn"], w["proj_attn"])
    x1, n2 = _out_proj(x, m, w["out"], w["norm_ffn"])
    idx, gate = _peer_route(n2, w["peer_q"], w["sub_keys"])
    po = _peer_mix(idx, gate, n2, w["peer_table"], w["expand"], w["pick_sum"])
    return _ple(x1, po, ple, w["norm_ple"], w["ple"], w["ple_gate"])


def kernel(x_prompt, x_sample, p_prompt, p_sample, state_conv, state_rglru, cache_k, cache_v, norm_mix, w_in, conv_w, conv_b, w_rgate, b_rgate, w_igate, b_igate, lru_lambda, w_proj_rnn, q_norm, k_norm, attn_sinks, w_proj_attn, w_out, norm_ffn, w_peer_q, peer_sub_keys, peer_u, peer_v, w_ple, norm_ple, w_ple_gate):
    depth = w_in.shape[0]
    assert depth == 1
    l = 0
    B, S, _ = x_prompt.shape
    DB, DS, _ = x_sample.shape
    wbuf = cache_k.shape[2]

    hp = _head_perm()
    offs = np.cumsum([0, D_RNN, D_RNN, D_Q, D_KV, D_KV, D_MODEL, D_MODEL])
    xr_c, gr_c, q_c, k_c, v_c, ga_c, gb_c = [np.arange(offs[i], offs[i + 1]) for i in range(7)]
    cols = np.concatenate([ga_c, gb_c, xr_c, gr_c, q_c, k_c, v_c])
    row2 = lambda a: a[l].reshape(1, -1)
    w = {
        "proj_rnn": w_proj_rnn[l].astype(BF16),
        "proj_attn": w_proj_attn[l].reshape(N_Q_HEADS, HEAD_DIM, D_MODEL)[hp].reshape(D_Q, D_MODEL).astype(BF16),
        "out": w_out[l].astype(BF16),
        "norm_ffn": row2(norm_ffn),
        "peer_q": w_peer_q[l].astype(BF16),
        "sub_keys": peer_sub_keys[l].reshape(2 * PEER_HEADS, N_KEYS, D_HALF_KEY).astype(BF16),
        "peer_table": _pack_table(peer_u[l].reshape(N_EXPERTS, D_TILES, LANES),
                                  peer_v[l].reshape(N_EXPERTS, D_TILES, LANES)),
        "norm_ple": row2(norm_ple),
        "ple": w_ple[l].astype(BF16),
        "ple_gate": w_ple_gate[l].astype(BF16),
    }
    expand = np.repeat(np.eye(N_PICKS, dtype=np.float32), D_TILES, axis=1)
    w["expand"] = jnp.asarray(expand, BF16)
    w["pick_sum"] = jnp.asarray(expand[:PICK_GROUP, :PICK_GROUP * D_TILES], BF16)
    runs = np.split(cols, np.flatnonzero(np.diff(cols) != 1) + 1)
    w_in_b = jnp.concatenate([w_in[l][:, r[0]:r[-1] + 1] for r in runs], axis=1).astype(BF16)
    g_mix = row2(norm_mix)
    cw, cb = conv_w[l], row2(conv_b)
    wr, br = _block_diag(w_rgate[l]).astype(BF16), row2(b_rgate)
    wi, bi = _block_diag(w_igate[l]).astype(BF16), row2(b_igate)
    lam = row2(lru_lambda)
    rep = LANES // HEAD_DIM
    qg = jnp.tile(q_norm[l], rep).reshape(1, LANES)
    kg = jnp.tile(k_norm[l], rep).reshape(1, LANES)
    seg = lambda width: jnp.asarray(
        np.kron(np.eye(width // HEAD_DIM, dtype=np.float32), np.ones((HEAD_DIM, HEAD_DIM), np.float32)), BF16)
    ones_q, ones_k = seg(D_Q), seg(D_KV)
    sinks = attn_sinks[l]

    xp = x_prompt.reshape(B * S, D_MODEL)
    zp = _in_proj(xp, g_mix, w_in_b)
    yr_p, h_p = _rnn_prompt(zp, B, S, cw, cb, wr, br, wi, bi, lam)
    tabs_p = _rope_tables(jnp.arange(S, dtype=jnp.int32))
    q_p, kf_p, kb_p, vb_p = _qk_prep(zp, tabs_p, qg, kg, ones_q, ones_k)
    o_p = _attn_prompt(sinks, q_p, kb_p, vb_p, B, S)
    y_p = _token_pipeline_tail(xp, zp, yr_p, o_p, w, p_prompt[l].reshape(B * S, D_PLE))

    zp3 = zp.reshape(B, S, D_IN)
    keep = min(WINDOW, S)
    prompt_conv = zp3[:, S - (CONV_W - 1):, OFF_XR:OFF_XR + D_RNN]
    prompt_k = kf_p.reshape(B, S, N_KV_HEADS, HEAD_DIM)[:, S - keep:]
    prompt_v = zp3[:, S - keep:, OFF_V:OFF_V + D_KV].reshape(B, keep, N_KV_HEADS, HEAD_DIM)

    ns = DB * DS
    xs = x_sample.reshape(ns, D_MODEL)
    zs = _in_proj(xs, g_mix, w_in_b)
    zs3 = zs.reshape(DB, DS, D_IN)
    tmaj = lambda a: jnp.transpose(a, (1, 0, 2))
    yr_s_t, h_s = _rnn_sample(tmaj(zs3[:, :, OFF_XR:OFF_XR + D_RNN]), tmaj(zs3[:, :, OFF_GR:OFF_GR + D_RNN]),
                              tmaj(state_conv[l]), state_rglru[l], cw, cb, wr, br, wi, bi, lam)
    yr_s = tmaj(yr_s_t).reshape(ns, D_RNN)
    pos_s = PAST_LEN + jnp.arange(DS, dtype=jnp.int32)
    tabs_s = tuple(jnp.tile(a, (DB, 1)) for a in _rope_tables(pos_s))
    q_s, kf_s, kb_s, vb_s = _qk_prep(zs, tabs_s, qg, kg, ones_q, ones_k)
    tq = BF16_TILE_ROWS
    pad_t = lambda a, rows: jnp.pad(a.reshape(DB, DS, -1), ((0, 0), (0, rows - DS), (0, 0)))
    ck = cache_k[l].reshape(DB, wbuf, D_KV)
    cv = cache_v[l].reshape(DB, wbuf, D_KV)
    o_s, sample_k, sample_v = _attn_sample(
        sinks, pad_t(q_s, tq), pad_t(kb_s, tq), pad_t(vb_s, tq), pad_t(kf_s, SUBLANES),
        pad_t(zs3[:, :, OFF_V:OFF_V + D_KV], SUBLANES), ck, cv, DS)
    o_s = o_s[:, :DS].reshape(ns, D_Q)
    y_s = _token_pipeline_tail(xs, zs, yr_s, o_s, w, p_sample[l].reshape(ns, D_PLE))

    sample_conv = jnp.concatenate([state_conv[l], zs3[:, :, OFF_XR:OFF_XR + D_RNN]], axis=1)[:, DS:]
    sample_k = sample_k.reshape(DB, wbuf, N_KV_HEADS, HEAD_DIM)
    sample_v = sample_v.reshape(DB, wbuf, N_KV_HEADS, HEAD_DIM)

    return (y_p.reshape(B, S, D_MODEL), y_s.reshape(DB, DS, D_MODEL),
            prompt_conv[None], h_p.reshape(1, B, D_RNN), prompt_k[None], prompt_v[None],
            sample_conv[None], h_s[None], sample_k[None], sample_v[None])
```

```python
import functools

import jax
import jax.numpy as jnp
import numpy as np
from jax import lax
from jax.experimental import pallas as pl
from jax.experimental.pallas import tpu as pltpu

D_MODEL = 2048
D_RNN = D_MODEL // 2
RNN_BLOCKS = 8
RNN_BLOCK = D_RNN // RNN_BLOCKS
CONV_W = 4
LRU_C = 8.0
HEAD_DIM = 64
N_Q_HEADS = D_MODEL // 2 // HEAD_DIM
N_KV_HEADS = 4
GROUP = N_Q_HEADS // N_KV_HEADS
D_Q = N_Q_HEADS * HEAD_DIM
D_KV = N_KV_HEADS * HEAD_DIM
WINDOW = 128
ROT_DIM = HEAD_DIM // 4
ROPE_THETA = 500000.0
N_KEYS = 128
N_EXPERTS = N_KEYS * N_KEYS
PEER_HEADS = 8
PEER_TOPK = 16
D_KEY = 256
D_HALF_KEY = D_KEY // 2
N_PICKS = PEER_HEADS * PEER_TOPK
D_PLE = 256
EPS = 1e-6
NEG_INF = -1e30
PAST_LEN = 16384

LANES = 128
SUBLANES = 8
BF16_TILE_ROWS = 16
VMEM_LIMIT_BYTES = 56 * 1024 * 1024

OFF_GA = 0
OFF_GB = OFF_GA + D_MODEL
OFF_XR = OFF_GB + D_MODEL
OFF_GR = OFF_XR + D_RNN
OFF_Q = OFF_GR + D_RNN
OFF_K = OFF_Q + D_Q
OFF_V = OFF_K + D_KV
D_IN = OFF_V + D_KV

BF16 = jnp.bfloat16
F32 = jnp.float32


def _params(*sem):
    return pltpu.CompilerParams(dimension_semantics=sem, vmem_limit_bytes=VMEM_LIMIT_BYTES)


def _rmsnorm_rows(x, g):
    return x * lax.rsqrt(jnp.mean(x * x, axis=-1, keepdims=True) + EPS) * g


def _dot(a, b):
    return jnp.dot(a, b, preferred_element_type=F32)


def _dot_nt(a, b):
    return lax.dot_general(a, b, (((1,), (1,)), ((), ())), preferred_element_type=F32)


def _in_proj_kernel(x_ref, g_ref, w_ref, z_ref, xn_ref):
    @pl.when(pl.program_id(1) == 0)
    def _():
        xn_ref[...] = _rmsnorm_rows(x_ref[...], g_ref[...]).astype(BF16)

    z_ref[...] = _dot(xn_ref[...], w_ref[...])


def _in_proj(x, g, w, tm=1024, tn=1536):
    n = x.shape[0]
    tm = min(tm, n)
    return pl.pallas_call(
        _in_proj_kernel,
        out_shape=jax.ShapeDtypeStruct((n, D_IN), F32),
        grid=(n // tm, D_IN // tn),
        in_specs=[
            pl.BlockSpec((tm, D_MODEL), lambda i, j: (i, 0)),
            pl.BlockSpec((1, D_MODEL), lambda i, j: (0, 0)),
            pl.BlockSpec((D_MODEL, tn), lambda i, j: (0, j)),
        ],
        out_specs=pl.BlockSpec((tm, tn), lambda i, j: (i, j)),
        scratch_shapes=[pltpu.VMEM((tm, D_MODEL), BF16)],
        compiler_params=_params("parallel", "arbitrary"),
        name="in_proj",
    )(x, g, w)


def _softplus(x):
    return jnp.maximum(x, 0.0) + jnp.log(1.0 + jnp.exp(-jnp.abs(x)))


def _neg_expm1(x):
    t = jnp.tanh(-0.5 * x)
    return 2.0 * t / (1.0 + t)


def _lru_coeffs(xc, wr, br, wi, bi, lam, first_is_pos0):
    xb = xc.astype(BF16)
    r = jax.nn.sigmoid(_dot(xb, wr) + br)
    i = jax.nn.sigmoid(_dot(xb, wi) + bi)
    log_a = -LRU_C * r * _softplus(-lam)
    a = jnp.exp(log_a)
    mult = jnp.sqrt(_neg_expm1(2.0 * log_a))
    if first_is_pos0 is not None:
        row = lax.broadcasted_iota(jnp.int32, xc.shape, 0)
        mult = jnp.where(jnp.logical_and(first_is_pos0, row == 0), 1.0, mult)
    return a, mult * i * xc


def _rnn_prompt_kernel(xr_ref, gr_ref, cw_ref, cb_ref, wr_ref, br_ref, wi_ref, bi_ref, lam_ref,
                       yr_ref, hlast_ref, tail_ref, h_ref, *, tb):
    t = pl.program_id(1)

    @pl.when(t == 0)
    def _():
        tail_ref[...] = jnp.zeros_like(tail_ref)
        h_ref[...] = jnp.zeros_like(h_ref)

    x = xr_ref[...]
    tail = tail_ref[...]
    row8 = lax.broadcasted_iota(jnp.int32, (SUBLANES, D_RNN), 0)
    cw = cw_ref[...]
    xc = cb_ref[...] + cw[CONV_W - 1:CONV_W, :] * x
    for k in range(1, CONV_W):
        xs = pltpu.roll(x, k, axis=0)
        head = jnp.where(row8 < k, pltpu.roll(tail, k, axis=0), xs[:SUBLANES])
        xs = jnp.concatenate([head, xs[SUBLANES:]], axis=0)
        xc = xc + cw[CONV_W - 1 - k:CONV_W - k, :] * xs
    tail_ref[...] = x[tb - SUBLANES:, :]

    a, b = _lru_coeffs(xc, wr_ref[...], br_ref[...], wi_ref[...], bi_ref[...], lam_ref[...], t == 0)

    ng = tb // SUBLANES
    a3 = a.reshape(ng, SUBLANES, D_RNN)
    b3 = b.reshape(ng, SUBLANES, D_RNN)
    sub = lax.broadcasted_iota(jnp.int32, (ng, SUBLANES, D_RNN), 1)
    d = 1
    while d < SUBLANES:
        a_sh = pltpu.roll(a3, d, axis=1)
        b_sh = pltpu.roll(b3, d, axis=1)
        keep = sub < d
        b3 = jnp.where(keep, b3, a3 * b_sh + b3)
        a3 = jnp.where(keep, a3, a3 * a_sh)
        d *= 2
    carry = h_ref[...]
    hs = []
    for g in range(ng):
        hs.append(b3[g] + a3[g] * carry)
        carry = hs[-1][SUBLANES - 1:SUBLANES, :]
    h_ref[...] = carry
    hlast_ref[0] = carry
    yr_ref[...] = (jnp.concatenate(hs, axis=0) * jax.nn.gelu(gr_ref[...])).astype(BF16)


def _rnn_prompt(z, batch, seq, cw, cb, wr, br, wi, bi, lam, tb=256):
    nt = seq // tb
    vec = pl.BlockSpec((1, D_RNN), lambda b, t: (0, 0))
    mat = pl.BlockSpec((D_RNN, D_RNN), lambda b, t: (0, 0))
    return pl.pallas_call(
        functools.partial(_rnn_prompt_kernel, tb=tb),
        out_shape=(jax.ShapeDtypeStruct((batch * seq, D_RNN), BF16),
                   jax.ShapeDtypeStruct((batch, 1, D_RNN), F32)),
        grid=(batch, nt),
        in_specs=[
            pl.BlockSpec((tb, D_RNN), lambda b, t: (b * nt + t, OFF_XR // D_RNN)),
            pl.BlockSpec((tb, D_RNN), lambda b, t: (b * nt + t, OFF_GR // D_RNN)),
            pl.BlockSpec((CONV_W, D_RNN), lambda b, t: (0, 0)),
            vec, mat, vec, mat, vec, vec,
        ],
        out_specs=(pl.BlockSpec((tb, D_RNN), lambda b, t: (b * nt + t, 0)),
                   pl.BlockSpec((1, 1, D_RNN), lambda b, t: (b, 0, 0))),
        scratch_shapes=[pltpu.VMEM((SUBLANES, D_RNN), F32), pltpu.VMEM((1, D_RNN), F32)],
        compiler_params=_params("parallel", "arbitrary"),
        name="rnn_prompt",
    )(z, z, cw, cb, wr, br, wi, bi, lam)


def _rnn_sample_kernel(xr_ref, gr_ref, buf_ref, h0_ref, cw_ref, cb_ref, wr_ref, br_ref, wi_ref,
                       bi_ref, lam_ref, yr_ref, hlast_ref, *, steps):
    cw = cw_ref[...]
    xp = [buf_ref[k] for k in range(CONV_W - 1)] + [xr_ref[s] for s in range(steps)]
    h = h0_ref[...]
    for s in range(steps):
        xc = cb_ref[...] + sum(cw[k:k + 1, :] * xp[s + k] for k in range(CONV_W))
        a, b = _lru_coeffs(xc, wr_ref[...], br_ref[...], wi_ref[...], bi_ref[...], lam_ref[...], None)
        h = a * h + b
        yr_ref[s] = (h * jax.nn.gelu(gr_ref[s])).astype(BF16)
    hlast_ref[...] = h


def _rnn_sample(xr_t, gr_t, buf_t, h0, cw, cb, wr, br, wi, bi, lam):
    steps, db, _ = xr_t.shape
    return pl.pallas_call(
        functools.partial(_rnn_sample_kernel, steps=steps),
        out_shape=(jax.ShapeDtypeStruct((steps, db, D_RNN), BF16),
                   jax.ShapeDtypeStruct((db, D_RNN), F32)),
        compiler_params=pltpu.CompilerParams(vmem_limit_bytes=VMEM_LIMIT_BYTES),
        name="rnn_sample",
    )(xr_t, gr_t, buf_t, h0, cw, cb, wr, br, wi, bi, lam)


def _headnorm_rope(x, gain, seg_ones, cos_t, sin_up_t, sin_dn_t, scale):
    w = x.shape[1]
    sq = x * x
    hi = sq.astype(BF16)
    lo = (sq - hi.astype(F32)).astype(BF16)
    ms = (_dot(hi, seg_ones) + _dot(lo, seg_ones)) * (1.0 / HEAD_DIM)
    xn = x * lax.rsqrt(ms + EPS)
    outs = []
    for c in range(w // LANES):
        xt = xn[:, c * LANES:(c + 1) * LANES] * gain
        up = pltpu.roll(xt, LANES - ROT_DIM // 2, axis=1)
        dn = pltpu.roll(xt, ROT_DIM // 2, axis=1)
        outs.append((xt * cos_t + up * sin_up_t + dn * sin_dn_t) * scale)
    return jnp.concatenate(outs, axis=1)


def _qk_prep_kernel(q_ref, k_ref, v_ref, cos_ref, sup_ref, sdn_ref, qg_ref, kg_ref, oq_ref, ok_ref,
                    qo_ref, kf_ref, kb_ref, vb_ref):
    cos_t, sup, sdn = cos_ref[...], sup_ref[...], sdn_ref[...]
    q = _headnorm_rope(q_ref[...], qg_ref[...], oq_ref[...], cos_t, sup, sdn, HEAD_DIM ** -0.5)
    q = jnp.concatenate([q[:, h * HEAD_DIM:(h + 1) * HEAD_DIM] for h in _head_perm()], axis=1)
    qo_ref[...] = q.astype(BF16)
    k = _headnorm_rope(k_ref[...], kg_ref[...], ok_ref[...], cos_t, sup, sdn, 1.0)
    kf_ref[...] = k
    kb_ref[...] = k.astype(BF16)
    vb_ref[...] = v_ref[...].astype(BF16)


def _qk_prep(z, tabs, qg, kg, ones_q, ones_k, tm=512):
    n = z.shape[0]
    tm = min(tm, n)
    cos_t, sup_t, sdn_t = tabs
    ntab = cos_t.shape[0] // tm
    tab = pl.BlockSpec((tm, LANES), lambda i: (i % ntab, 0))
    const = lambda shape: pl.BlockSpec(shape, lambda i: (0, 0))
    return pl.pallas_call(
        _qk_prep_kernel,
        out_shape=(jax.ShapeDtypeStruct((n, D_Q), BF16), jax.ShapeDtypeStruct((n, D_KV), F32),
                   jax.ShapeDtypeStruct((n, D_KV), BF16), jax.ShapeDtypeStruct((n, D_KV), BF16)),
        grid=(n // tm,),
        in_specs=[
            pl.BlockSpec((tm, D_Q), lambda i: (i, OFF_Q // D_Q)),
            pl.BlockSpec((tm, D_KV), lambda i: (i, OFF_K // D_KV)),
            pl.BlockSpec((tm, D_KV), lambda i: (i, OFF_V // D_KV)),
            tab, tab, tab,
            const((1, LANES)), const((1, LANES)), const((D_Q, D_Q)), const((D_KV, D_KV)),
        ],
        out_specs=(pl.BlockSpec((tm, D_Q), lambda i: (i, 0)), pl.BlockSpec((tm, D_KV), lambda i: (i, 0)),
                   pl.BlockSpec((tm, D_KV), lambda i: (i, 0)), pl.BlockSpec((tm, D_KV), lambda i: (i, 0))),
        compiler_params=_params("parallel"),
        name="qk_prep",
    )(z, z, z, cos_t, sup_t, sdn_t, qg, kg, ones_q, ones_k)


def _sink_attention(q, k, v, valid, sink_ref, tq):
    lane_head = lax.broadcasted_iota(jnp.int32, (tq, D_KV), 1) // HEAD_DIM
    rowg = lax.broadcasted_iota(jnp.int32, (GROUP * tq, 1), 0) // tq
    validg = jnp.concatenate([valid] * GROUP, axis=0)
    out = [jnp.zeros((tq, D_KV), F32) for _ in range(GROUP)]
    for j in range(N_KV_HEADS):
        sel = lane_head == j
        keep = jnp.where(sel, 1.0, 0.0).astype(BF16)
        qs = jnp.concatenate([q[:, g * D_KV:(g + 1) * D_KV] * keep for g in range(GROUP)], axis=0)
        s = jnp.where(validg, _dot_nt(qs, k), NEG_INF)
        sk = jnp.zeros((GROUP * tq, 1), F32)
        for g in range(GROUP):
            sk = jnp.where(rowg == g, sink_ref[j * GROUP + g], sk)
        m = jnp.maximum(jnp.max(s, axis=-1, keepdims=True), sk)
        p = jnp.exp(s - m)
        denom = jnp.sum(p, axis=-1, keepdims=True) + jnp.exp(sk - m)
        pv = _dot(p.astype(BF16), v) / denom
        for g in range(GROUP):
            out[g] = jnp.where(sel, pv[g * tq:(g + 1) * tq], out[g])
    return jnp.concatenate(out, axis=1)


def _attn_prompt_kernel(sink_ref, q_ref, kp_ref, kc_ref, vp_ref, vc_ref, o_ref):
    nb = pl.program_id(1)
    k = jnp.concatenate([kp_ref[...], kc_ref[...]], axis=0)
    v = jnp.concatenate([vp_ref[...], vc_ref[...]], axis=0)
    i = lax.broadcasted_iota(jnp.int32, (WINDOW, 2 * WINDOW), 0)
    j = lax.broadcasted_iota(jnp.int32, (WINDOW, 2 * WINDOW), 1)
    d = WINDOW + i - j
    valid = (d >= 0) & (d < WINDOW) & ((j >= WINDOW) | (nb > 0))
    o_ref[...] = _sink_attention(q_ref[...], k, v, valid, sink_ref, WINDOW).astype(BF16)


def _attn_prompt(sinks, q, kb, vb, batch, seq):
    nblk = seq // WINDOW
    cur = lambda w: pl.BlockSpec((WINDOW, w), lambda b, t: (b * nblk + t, 0))
    prev = lambda w: pl.BlockSpec((WINDOW, w), lambda b, t: (b * nblk + jnp.maximum(t - 1, 0), 0))
    return pl.pallas_call(
        _attn_prompt_kernel,
        out_shape=jax.ShapeDtypeStruct((batch * seq, D_Q), BF16),
        grid=(batch, nblk),
        in_specs=[pl.BlockSpec(memory_space=pltpu.SMEM), cur(D_Q), prev(D_KV), cur(D_KV),
                  prev(D_KV), cur(D_KV)],
        out_specs=cur(D_Q),
        compiler_params=_params("parallel", "arbitrary"),
        name="attn_prompt",
    )(sinks, q, kb, kb, vb, vb)


def _shift_in(cache, new8, steps):
    wbuf = cache.shape[0]
    rolled = pltpu.roll(cache, wbuf - steps, axis=0)
    row8 = lax.broadcasted_iota(jnp.int32, new8.shape, 0)
    tail = jnp.where(row8 >= SUBLANES - steps, pltpu.roll(new8, SUBLANES - steps, axis=0),
                     rolled[wbuf - SUBLANES:])
    return jnp.concatenate([rolled[:wbuf - SUBLANES], tail], axis=0)


def _attn_sample_kernel(sink_ref, q_ref, kn_ref, vn_ref, knf_ref, vnf_ref, ck_ref, cv_ref, o_ref, ok_ref, ov_ref,
                        *, bs, tq, wbuf, steps):
    t = lax.broadcasted_iota(jnp.int32, (tq, wbuf + tq), 0)
    c = lax.broadcasted_iota(jnp.int32, (tq, wbuf + tq), 1)
    d = wbuf + t - c
    valid = (d >= 0) & (d < WINDOW)
    for s in range(bs):
        ck, cv = ck_ref[s], cv_ref[s]
        k = jnp.concatenate([ck.astype(BF16), kn_ref[s]], axis=0)
        v = jnp.concatenate([cv.astype(BF16), vn_ref[s]], axis=0)
        o_ref[s] = _sink_attention(q_ref[s], k, v, valid, sink_ref, tq).astype(BF16)
        ok_ref[s] = _shift_in(ck, knf_ref[s], steps)
        ov_ref[s] = _shift_in(cv, vnf_ref[s], steps)


def _attn_sample(sinks, q3, kn3, vn3, knf3, vnf3, cache_k, cache_v, steps, bs=8):
    db, tq, _ = q3.shape
    wbuf = cache_k.shape[1]
    assert steps <= SUBLANES and knf3.shape[1] == SUBLANES and db % bs == 0
    blk = lambda r, w: pl.BlockSpec((bs, r, w), lambda b: (b, 0, 0))
    cache = jax.ShapeDtypeStruct((db, wbuf, D_KV), F32)
    return pl.pallas_call(
        functools.partial(_attn_sample_kernel, bs=bs, tq=tq, wbuf=wbuf, steps=steps),
        out_shape=(jax.ShapeDtypeStruct((db, tq, D_Q), BF16), cache, cache),
        grid=(db // bs,),
        in_specs=[pl.BlockSpec(memory_space=pltpu.SMEM), blk(tq, D_Q), blk(tq, D_KV), blk(tq, D_KV),
                  blk(SUBLANES, D_KV), blk(SUBLANES, D_KV), blk(wbuf, D_KV), blk(wbuf, D_KV)],
        out_specs=(blk(tq, D_Q), blk(wbuf, D_KV), blk(wbuf, D_KV)),
        compiler_params=_params("parallel"),
        name="attn_sample",
    )(sinks, q3, kn3, vn3, knf3, vnf3, cache_k, cache_v)


def _merge_kernel(yr_ref, o_ref, ga_ref, gb_ref, wr_ref, wa_ref, m_ref):
    a = _dot(yr_ref[...], wr_ref[...])
    b = _dot(o_ref[...], wa_ref[...])
    m_ref[...] = (jax.nn.sigmoid(ga_ref[...]) * a + jax.nn.sigmoid(gb_ref[...]) * b).astype(BF16)


def _merge(yr, o, z, wr, wa, tm=512, tn=1024):
    n = yr.shape[0]
    tm = min(tm, n)
    nj = D_MODEL // tn
    return pl.pallas_call(
        _merge_kernel,
        out_shape=jax.ShapeDtypeStruct((n, D_MODEL), BF16),
        grid=(n // tm, nj),
        in_specs=[
            pl.BlockSpec((tm, D_RNN), lambda i, j: (i, 0)),
            pl.BlockSpec((tm, D_Q), lambda i, j: (i, 0)),
            pl.BlockSpec((tm, tn), lambda i, j: (i, OFF_GA // tn + j)),
            pl.BlockSpec((tm, tn), lambda i, j: (i, OFF_GB // tn + j)),
            pl.BlockSpec((D_RNN, tn), lambda i, j: (0, j)),
            pl.BlockSpec((D_Q, tn), lambda i, j: (0, j)),
        ],
        out_specs=pl.BlockSpec((tm, tn), lambda i, j: (i, j)),
        compiler_params=_params("parallel", "arbitrary"),
        name="merge",
    )(yr, o, z, z, wr, wa)


def _out_proj_kernel(x_ref, m_ref, w_ref, g_ref, x1_ref, n2_ref):
    x1 = x_ref[...] + _dot(m_ref[...], w_ref[...])
    x1_ref[...] = x1
    n2_ref[...] = _rmsnorm_rows(x1, g_ref[...]).astype(BF16)


def _out_proj(x, m, w, g, tm=512):
    n = x.shape[0]
    tm = min(tm, n)
    row = lambda dt: pl.BlockSpec((tm, D_MODEL), lambda i: (i, 0))
    return pl.pallas_call(
        _out_proj_kernel,
        out_shape=(jax.ShapeDtypeStruct((n, D_MODEL), F32), jax.ShapeDtypeStruct((n, D_MODEL), BF16)),
        grid=(n // tm,),
        in_specs=[row(F32), row(BF16), pl.BlockSpec((D_MODEL, D_MODEL), lambda i: (0, 0)),
                  pl.BlockSpec((1, D_MODEL), lambda i: (0, 0))],
        out_specs=(row(F32), row(BF16)),
        compiler_params=_params("parallel"),
        name="out_proj",
    )(x, m, w, g)


def _topk_rows(s, k, rid=None):
    rows, t = s.shape
    if rid is None:
        rid = lax.broadcasted_iota(jnp.int32, (rows, t), 0).astype(F32)
    slot = lax.broadcasted_iota(jnp.int32, (k, t), 0)
    vals = jnp.zeros((k, t), F32)
    ids = jnp.zeros((k, t), F32)
    for r in range(k):
        m = jnp.max(s, axis=0, keepdims=True)
        i = jnp.min(jnp.where(s == m, rid, jnp.inf), axis=0, keepdims=True)
        vals = jnp.where(slot == r, m, vals)
        ids = jnp.where(slot == r, i, ids)
        s = jnp.where(rid == i, -jnp.inf, s)
    return vals, ids


def _route_tokens(n2, wq, sk_ref):
    q = _dot(n2, wq).astype(BF16)
    gates, experts = [], []
    for h in range(PEER_HEADS):
        top = []
        for p in range(2):
            c = (2 * h + p) * D_HALF_KEY
            s = _dot_nt(sk_ref[2 * h + p], q[:, c:c + D_HALF_KEY])
            top.append(_topk_rows(s, PEER_TOPK))
        (s1, i1), (s2, i2) = top
        hk = PEER_TOPK // 2
        tcol = s1.shape[1]
        brow = lambda n: lax.broadcasted_iota(jnp.int32, (n, tcol), 0).astype(F32)
        cand = [s1[0:1, :] + s2]
        cidx = [i1[0:1, :] * float(N_KEYS) + i2]
        rid = [brow(PEER_TOPK)]
        for a in range(1, hk):
            cand.append(s1[a:a + 1, :] + s2[0:hk, :])
            cidx.append(i1[a:a + 1, :] * float(N_KEYS) + i2[0:hk, :])
            rid.append(brow(hk) + float(a * PEER_TOPK))
        cand.append(s1[hk:, :] + s2[0:1, :])
        cidx.append(i1[hk:, :] * float(N_KEYS) + i2[0:1, :])
        rid.append((brow(hk) + float(hk)) * float(PEER_TOPK))
        cand, cidx, rid = (jnp.concatenate(x, axis=0) for x in (cand, cidx, rid))
        top_s, pos = _topk_rows(cand, PEER_TOPK, rid)
        slot = lax.broadcasted_iota(jnp.int32, top_s.shape, 0)
        ids = jnp.zeros(top_s.shape, F32)
        for r in range(PEER_TOPK):
            e = jnp.max(jnp.where(rid == pos[r:r + 1, :], cidx, -1.0), axis=0, keepdims=True)
            ids = jnp.where(slot == r, e, ids)
        w = jnp.exp(top_s - top_s[0:1, :])
        gates.append(w / jnp.sum(w, axis=0, keepdims=True))
        experts.append(ids)
    return jnp.concatenate(experts, axis=0).T.astype(jnp.int32), jnp.concatenate(gates, axis=0).T


def _peer_route_kernel(n2_ref, wq_ref, sk_ref, idx_ref, gate_ref):
    idx_ref[...], gate_ref[...] = _route_tokens(n2_ref[...], wq_ref[...], sk_ref)


def _peer_route(n2, wq, sk, tm=256):
    n = n2.shape[0]
    tm = min(tm, n)
    return pl.pallas_call(
        _peer_route_kernel,
        out_shape=(jax.ShapeDtypeStruct((n, N_PICKS), jnp.int32), jax.ShapeDtypeStruct((n, N_PICKS), F32)),
        grid=(n // tm,),
        in_specs=[pl.BlockSpec((tm, D_MODEL), lambda i: (i, 0)),
                  pl.BlockSpec((D_MODEL, PEER_HEADS * D_KEY), lambda i: (0, 0)),
                  pl.BlockSpec((2 * PEER_HEADS, N_KEYS, D_HALF_KEY), lambda i: (0, 0, 0))],
        out_specs=(pl.BlockSpec((tm, N_PICKS), lambda i: (i, 0)), pl.BlockSpec((tm, N_PICKS), lambda i: (i, 0))),
        compiler_params=_params("parallel"),
        name="peer_route",
    )(n2, wq, sk)


D_TILES = D_MODEL // LANES
assert D_TILES == BF16_TILE_ROWS
PICK_GROUP = 16
WAIT_GROUP = 4


def _peer_mix_kernel(idx_ref, idxn_ref, x_ref, gate_ref, exp_ref, sel_ref, tab_ref, o_ref, buf_a, buf_b,
                     sem_ref, *, tb, nsteps):
    i = pl.program_id(0)
    bufs = (buf_a, buf_b)

    group_rows = WAIT_GROUP * N_PICKS

    def start_token(ids_ref, row, half, t, picks=(0, N_PICKS)):
        for j in range(*picks):
            pltpu.make_async_copy(tab_ref.at[ids_ref[row, j]], bufs[half].at[t * N_PICKS + j],
                                  sem_ref.at[half, t // WAIT_GROUP]).start(priority=j % 2)

    def wait_group(half, g):
        pltpu.make_async_copy(tab_ref.at[pl.ds(0, group_rows)], bufs[half].at[pl.ds(g * group_rows, group_rows)],
                              sem_ref.at[half, g]).wait()

    @pl.when(i == 0)
    def _():
        for t in range(tb):
            start_token(idx_ref, t, 0, t)

    rows = N_PICKS * D_TILES
    diag = (lax.broadcasted_iota(jnp.int32, (D_TILES, rows), 1) % D_TILES
            == lax.broadcasted_iota(jnp.int32, (D_TILES, rows), 0))

    sel = sel_ref[...]
    lane = lax.broadcasted_iota(jnp.int32, (N_PICKS, LANES), 1)

    first_picks, second_picks = (0, N_PICKS // 2), (N_PICKS // 2, N_PICKS)

    def run_tile(half, start_other):
        buf = bufs[half]
        hcols = jnp.zeros((N_PICKS, LANES), F32)
        for t in range(tb):
            if t % WAIT_GROUP == 0:
                wait_group(half, t // WAIT_GROUP)
            start_other(t, first_picks)
            u3 = buf[t * N_PICKS:(t + 1) * N_PICKS, 0:D_TILES, :]
            p = (u3 * x_ref[half * tb + t][None, :, :]).reshape(rows, LANES)
            grp = PICK_GROUP * D_TILES
            r = jnp.concatenate([_dot(sel, p[c * grp:(c + 1) * grp]) for c in range(N_PICKS // PICK_GROUP)],
                                axis=0)
            hcols = jnp.where(lane == t, jnp.sum(r, axis=1, keepdims=True), hcols)
        h = hcols.T[0:tb, :]
        w = (jax.nn.gelu(h) * gate_ref[half * tb:(half + 1) * tb, :]).astype(BF16)
        wrow = _dot(w, exp_ref[...])
        for t in range(tb):
            start_other(t, second_picks)
            v = buf[t * N_PICKS:(t + 1) * N_PICKS, D_TILES:2 * D_TILES, :].reshape(rows, LANES)
            wm = jnp.where(diag, wrow[t:t + 1, :], 0.0).astype(BF16)
            o_ref[half * tb + t] = _dot(wm, v)

    run_tile(0, lambda t, picks: start_token(idx_ref, tb + t, 1, t, picks))
    run_tile(1, lambda t, picks: start_token(idxn_ref, t, 0, t, picks))

    @pl.when(i == nsteps - 1)
    def _():
        for g in range(tb // WAIT_GROUP):
            wait_group(0, g)


def _peer_mix(idx, gates, n2, table, expand, sel, tb=BF16_TILE_ROWS):
    n = idx.shape[0]
    nsteps = n // (2 * tb)
    x3 = n2.reshape(n, D_TILES, LANES)
    rows = N_PICKS * D_TILES
    out = pl.pallas_call(
        functools.partial(_peer_mix_kernel, tb=tb, nsteps=nsteps),
        out_shape=jax.ShapeDtypeStruct((n, D_TILES, LANES), F32),
        grid=(nsteps,),
        in_specs=[
            pl.BlockSpec((2 * tb, N_PICKS), lambda i: (i, 0), memory_space=pltpu.SMEM),
            pl.BlockSpec((tb, N_PICKS), lambda i: (jnp.minimum(2 * i + 2, 2 * nsteps - 2), 0),
                         memory_space=pltpu.SMEM),
            pl.BlockSpec((2 * tb, D_TILES, LANES), lambda i: (i, 0, 0)),
            pl.BlockSpec((2 * tb, N_PICKS), lambda i: (i, 0)),
            pl.BlockSpec((N_PICKS, rows), lambda i: (0, 0)),
            pl.BlockSpec((PICK_GROUP, PICK_GROUP * D_TILES), lambda i: (0, 0)),
            pl.BlockSpec(memory_space=pl.ANY),
        ],
        out_specs=pl.BlockSpec((2 * tb, D_TILES, LANES), lambda i: (i, 0, 0)),
        scratch_shapes=[pltpu.VMEM((tb * N_PICKS, 2 * D_TILES, LANES), BF16),
                        pltpu.VMEM((tb * N_PICKS, 2 * D_TILES, LANES), BF16),
                        pltpu.SemaphoreType.DMA((2, tb // WAIT_GROUP))],
        compiler_params=_params("arbitrary"),
        name="peer_mix",
    )(idx, idx, x3, gates, expand, sel, table)
    return out.reshape(n, D_MODEL)


ROUTE_CHUNK = 128


def _peer_fused_kernel(idx0_ref, gate0_ref, wq_hbm, x_ref, n2r_ref, sk_ref, exp_ref, sel_ref, tab_ref, o_ref,
                       buf_a, buf_b, wq_ref, idx_smem, idx_vmem, gate_ring, sem_ref, aux_sem, *, tb, nsteps, nchunks):
    i = pl.program_id(0)
    bufs = (buf_a, buf_b)
    spc = ROUTE_CHUNK // (2 * tb)
    group_rows = WAIT_GROUP * N_PICKS

    def ring_pos(step):
        return (step // spc) % 2, (step % spc) * (2 * tb)

    def start_token(slot, row, half, t, picks=(0, N_PICKS)):
        for j in range(*picks):
            pltpu.make_async_copy(tab_ref.at[idx_smem[slot, row, j]], bufs[half].at[t * N_PICKS + j],
                                  sem_ref.at[half, t // WAIT_GROUP]).start(priority=j % 2)

    def wait_group(half, g):
        pltpu.make_async_copy(tab_ref.at[pl.ds(0, group_rows)], bufs[half].at[pl.ds(g * group_rows, group_rows)],
                              sem_ref.at[half, g]).wait()

    def blocking_copy(src, dst):
        cp = pltpu.make_async_copy(src, dst, aux_sem)
        cp.start()
        cp.wait()

    @pl.when(i == 0)
    def _():
        blocking_copy(idx0_ref, idx_smem.at[0])
        blocking_copy(gate0_ref, gate_ring.at[0])
        blocking_copy(wq_hbm, wq_ref)
        for t in range(tb):
            start_token(0, t, 0, t)

    @pl.when(jnp.logical_and(i % spc == 0, i // spc + 1 < nchunks))
    def _():
        nslot = (i // spc + 1) % 2
        ids, gates = _route_tokens(n2r_ref[...], wq_ref[...], sk_ref)
        gate_ring[nslot] = gates
        idx_vmem[...] = ids
        blocking_copy(idx_vmem, idx_smem.at[nslot])

    rows = N_PICKS * D_TILES
    diag = (lax.broadcasted_iota(jnp.int32, (D_TILES, rows), 1) % D_TILES
            == lax.broadcasted_iota(jnp.int32, (D_TILES, rows), 0))
    sel = sel_ref[...]
    lane = lax.broadcasted_iota(jnp.int32, (N_PICKS, LANES), 1)

    first_picks, second_picks = (0, N_PICKS // 2), (N_PICKS // 2, N_PICKS)
    slot, row0 = ring_pos(i)
    nslot, nrow0 = ring_pos(jnp.minimum(i + 1, nsteps - 1))

    def run_tile(half, start_other):
        buf = bufs[half]
        hcols = jnp.zeros((N_PICKS, LANES), F32)
        for t in range(tb):
            if t % WAIT_GROUP == 0:
                wait_group(half, t // WAIT_GROUP)
            start_other(t, first_picks)
            u3 = buf[t * N_PICKS:(t + 1) * N_PICKS, 0:D_TILES, :]
            p = (u3 * x_ref[half * tb + t][None, :, :]).reshape(rows, LANES)
            grp = PICK_GROUP * D_TILES
            r = jnp.concatenate([_dot(sel, p[c * grp:(c + 1) * grp]) for c in range(N_PICKS // PICK_GROUP)],
                                axis=0)
            hcols = jnp.where(lane == t, jnp.sum(r, axis=1, keepdims=True), hcols)
        h = hcols.T[0:tb, :]
        gate = gate_ring[slot, pl.ds(pl.multiple_of(row0 + half * tb, tb), tb), :]
        w = (jax.nn.gelu(h) * gate).astype(BF16)
        wrow = _dot(w, exp_ref[...])
        for t in range(tb):
            start_other(t, second_picks)
            v = buf[t * N_PICKS:(t + 1) * N_PICKS, D_TILES:2 * D_TILES, :].reshape(rows, LANES)
            wm = jnp.where(diag, wrow[t:t + 1, :], 0.0).astype(BF16)
            o_ref[half * tb + t] = _dot(wm, v)

    run_tile(0, lambda t, picks: start_token(slot, row0 + tb + t, 1, t, picks))
    run_tile(1, lambda t, picks: start_token(nslot, nrow0 + t, 0, t, picks))

    @pl.when(i == nsteps - 1)
    def _():
        for g in range(tb // WAIT_GROUP):
            wait_group(0, g)


def _peer_fused(n2, wq, sk, table, expand, sel, tb=BF16_TILE_ROWS):
    n = n2.shape[0]
    assert n % ROUTE_CHUNK == 0 and ROUTE_CHUNK % (2 * tb) == 0
    nsteps, nchunks = n // (2 * tb), n // ROUTE_CHUNK
    spc = ROUTE_CHUNK // (2 * tb)
    idx0, gate0 = _peer_route(n2[:ROUTE_CHUNK], wq, sk, tm=ROUTE_CHUNK)
    x3 = n2.reshape(n, D_TILES, LANES)
    rows = N_PICKS * D_TILES
    out = pl.pallas_call(
        functools.partial(_peer_fused_kernel, tb=tb, nsteps=nsteps, nchunks=nchunks),
        out_shape=jax.ShapeDtypeStruct((n, D_TILES, LANES), F32),
        grid=(nsteps,),
        in_specs=[
            pl.BlockSpec(memory_space=pl.ANY),
            pl.BlockSpec(memory_space=pl.ANY),
            pl.BlockSpec(memory_space=pl.ANY),
            pl.BlockSpec((2 * tb, D_TILES, LANES), lambda i: (i, 0, 0)),
            pl.BlockSpec((ROUTE_CHUNK, D_MODEL), lambda i: (jnp.minimum(i // spc + 1, nchunks - 1), 0)),
            pl.BlockSpec((2 * PEER_HEADS, N_KEYS, D_HALF_KEY), lambda i: (0, 0, 0)),
            pl.BlockSpec((N_PICKS, rows), lambda i: (0, 0)),
            pl.BlockSpec((PICK_GROUP, PICK_GROUP * D_TILES), lambda i: (0, 0)),
            pl.BlockSpec(memory_space=pl.ANY),
        ],
        out_specs=pl.BlockSpec((2 * tb, D_TILES, LANES), lambda i: (i, 0, 0)),
        scratch_shapes=[pltpu.VMEM((tb * N_PICKS, 2 * D_TILES, LANES), BF16),
                        pltpu.VMEM((tb * N_PICKS, 2 * D_TILES, LANES), BF16),
                        pltpu.VMEM((D_MODEL, PEER_HEADS * D_KEY), BF16),
                        pltpu.SMEM((2, ROUTE_CHUNK, N_PICKS), jnp.int32),
                        pltpu.VMEM((ROUTE_CHUNK, N_PICKS), jnp.int32),
                        pltpu.VMEM((2, ROUTE_CHUNK, N_PICKS), F32),
                        pltpu.SemaphoreType.DMA((2, tb // WAIT_GROUP)),
                        pltpu.SemaphoreType.DMA],
        compiler_params=_params("arbitrary"),
        name="peer_fused",
    )(idx0, gate0, wq, x3, n2, sk, expand, sel, table)
    return out.reshape(n, D_MODEL)


def _pack_table_kernel(u_ref, v_ref, o_ref):
    o_ref[:, 0:D_TILES, :] = u_ref[...].astype(BF16)
    o_ref[:, D_TILES:2 * D_TILES, :] = v_ref[...].astype(BF16)


def _pack_table(u3, v3, te=256):
    e = u3.shape[0]
    te = min(te, e)
    blk = pl.BlockSpec((te, D_TILES, LANES), lambda i: (i, 0, 0))
    return pl.pallas_call(
        _pack_table_kernel,
        out_shape=jax.ShapeDtypeStruct((e, 2 * D_TILES, LANES), BF16),
        grid=(e // te,),
        in_specs=[blk, blk],
        out_specs=pl.BlockSpec((te, 2 * D_TILES, LANES), lambda i: (i, 0, 0)),
        compiler_params=_params("parallel"),
        name="pack_table",
    )(u3, v3)


def _ple_kernel(x1_ref, po_ref, ple_ref, g_ref, wp_ref, wg_ref, y_ref):
    x2 = x1_ref[...] + po_ref[...]
    n3 = _rmsnorm_rows(x2, g_ref[...]).astype(BF16)
    emb = _dot(ple_ref[...].astype(BF16), wp_ref[...])
    y_ref[...] = x2 + emb * jax.nn.sigmoid(_dot(n3, wg_ref[...]))


def _ple(x1, po, ple, g, wp, wg, tm=512):
    n = x1.shape[0]
    tm = min(tm, n)
    row = pl.BlockSpec((tm, D_MODEL), lambda i: (i, 0))
    return pl.pallas_call(
        _ple_kernel,
        out_shape=jax.ShapeDtypeStruct((n, D_MODEL), F32),
        grid=(n // tm,),
        in_specs=[row, row, pl.BlockSpec((tm, D_PLE), lambda i: (i, 0)),
                  pl.BlockSpec((1, D_MODEL), lambda i: (0, 0)),
                  pl.BlockSpec((D_PLE, D_MODEL), lambda i: (0, 0)),
                  pl.BlockSpec((D_MODEL, D_MODEL), lambda i: (0, 0))],
        out_specs=row,
        compiler_params=_params("parallel"),
        name="ple",
    )(x1, po, ple, g, wp, wg)


def _rope_tables(pos):
    half = ROT_DIM // 2
    inv = ROPE_THETA ** (-jnp.arange(0, ROT_DIM, 2, dtype=F32) / ROT_DIM)
    ang = pos.astype(F32)[:, None] * inv[None, :]
    cos, sin = jnp.cos(ang), jnp.sin(ang)
    n = pos.shape[0]
    pad = jnp.zeros((n, HEAD_DIM - ROT_DIM), F32)
    zh = jnp.zeros((n, half), F32)
    cos_h = jnp.concatenate([cos, cos, pad + 1.0], axis=1)
    up_h = jnp.concatenate([-sin, zh, pad], axis=1)
    dn_h = jnp.concatenate([zh, sin, pad], axis=1)
    rep = LANES // HEAD_DIM
    return tuple(jnp.tile(a, (1, rep)) for a in (cos_h, up_h, dn_h))


def _block_diag(w):
    eye = jnp.eye(RNN_BLOCKS, dtype=w.dtype)
    return jnp.einsum("ncd,nm->ncmd", w, eye).reshape(D_RNN, D_RNN)


def _head_perm():
    return np.array([j * GROUP + g for g in range(GROUP) for j in range(N_KV_HEADS)])


def _token_pipeline_tail(x, z, yr, o, w, ple):
    m = _merge(yr, o, z, w["proj_rnn"], w["proj_attn"])
    x1, n2 = _out_proj(x, m, w["out"], w["norm_ffn"])
    po = _peer_fused(n2, w["peer_q"], w["sub_keys"], w["peer_table"], w["expand"], w["pick_sum"])
    return _ple(x1, po, ple, w["norm_ple"], w["ple"], w["ple_gate"])


def kernel(x_prompt, x_sample, p_prompt, p_sample, state_conv, state_rglru, cache_k, cache_v, norm_mix, w_in, conv_w, conv_b, w_rgate, b_rgate, w_igate, b_igate, lru_lambda, w_proj_rnn, q_norm, k_norm, attn_sinks, w_proj_attn, w_out, norm_ffn, w_peer_q, peer_sub_keys, peer_u, peer_v, w_ple, norm_ple, w_ple_gate):
    depth = w_in.shape[0]
    assert depth == 1
    l = 0
    B, S, _ = x_prompt.shape
    DB, DS, _ = x_sample.shape
    wbuf = cache_k.shape[2]

    hp = _head_perm()
    offs = np.cumsum([0, D_RNN, D_RNN, D_Q, D_KV, D_KV, D_MODEL, D_MODEL])
    xr_c, gr_c, q_c, k_c, v_c, ga_c, gb_c = [np.arange(offs[i], offs[i + 1]) for i in range(7)]
    cols = np.concatenate([ga_c, gb_c, xr_c, gr_c, q_c, k_c, v_c])
    row2 = lambda a: a[l].reshape(1, -1)
    w = {
        "proj_rnn": w_proj_rnn[l].astype(BF16),
        "proj_attn": w_proj_attn[l].reshape(N_Q_HEADS, HEAD_DIM, D_MODEL)[hp].reshape(D_Q, D_MODEL).astype(BF16),
        "out": w_out[l].astype(BF16),
        "norm_ffn": row2(norm_ffn),
        "peer_q": w_peer_q[l].astype(BF16),
        "sub_keys": peer_sub_keys[l].reshape(2 * PEER_HEADS, N_KEYS, D_HALF_KEY).astype(BF16),
        "peer_table": _pack_table(peer_u[l].reshape(N_EXPERTS, D_TILES, LANES),
                                  peer_v[l].reshape(N_EXPERTS, D_TILES, LANES)),
        "norm_ple": row2(norm_ple),
        "ple": w_ple[l].astype(BF16),
        "ple_gate": w_ple_gate[l].astype(BF16),
    }
    expand = np.repeat(np.eye(N_PICKS, dtype=np.float32), D_TILES, axis=1)
    w["expand"] = jnp.asarray(expand, BF16)
    w["pick_sum"] = jnp.asarray(expand[:PICK_GROUP, :PICK_GROUP * D_TILES], BF16)
    runs = np.split(cols, np.flatnonzero(np.diff(cols) != 1) + 1)
    w_in_b = jnp.concatenate([w_in[l][:, r[0]:r[-1] + 1] for r in runs], axis=1).astype(BF16)
    g_mix = row2(norm_mix)
    cw, cb = conv_w[l], row2(conv_b)
    wr, br = _block_diag(w_rgate[l]).astype(BF16), row2(b_rgate)
    wi, bi = _block_diag(w_igate[l]).astype(BF16), row2(b_igate)
    lam = row2(lru_lambda)
    rep = LANES // HEAD_DIM
    qg = jnp.tile(q_norm[l], rep).reshape(1, LANES)
    kg = jnp.tile(k_norm[l], rep).reshape(1, LANES)
    seg = lambda width: jnp.asarray(
        np.kron(np.eye(width // HEAD_DIM, dtype=np.float32), np.ones((HEAD_DIM, HEAD_DIM), np.float32)), BF16)
    ones_q, ones_k = seg(D_Q), seg(D_KV)
    sinks = attn_sinks[l]

    xp = x_prompt.reshape(B * S, D_MODEL)
    zp = _in_proj(xp, g_mix, w_in_b)
    yr_p, h_p = _rnn_prompt(zp, B, S, cw, cb, wr, br, wi, bi, lam)
    tabs_p = _rope_tables(jnp.arange(S, dtype=jnp.int32))
    q_p, kf_p, kb_p, vb_p = _qk_prep(zp, tabs_p, qg, kg, ones_q, ones_k)
    o_p = _attn_prompt(sinks, q_p, kb_p, vb_p, B, S)
    y_p = _token_pipeline_tail(xp, zp, yr_p, o_p, w, p_prompt[l].reshape(B * S, D_PLE))

    zp3 = zp.reshape(B, S, D_IN)
    keep = min(WINDOW, S)
    prompt_conv = zp3[:, S - (CONV_W - 1):, OFF_XR:OFF_XR + D_RNN]
    prompt_k = kf_p.reshape(B, S, N_KV_HEADS, HEAD_DIM)[:, S - keep:]
    prompt_v = zp3[:, S - keep:, OFF_V:OFF_V + D_KV].reshape(B, keep, N_KV_HEADS, HEAD_DIM)

    ns = DB * DS
    xs = x_sample.reshape(ns, D_MODEL)
    zs = _in_proj(xs, g_mix, w_in_b)
    zs3 = zs.reshape(DB, DS, D_IN)
    tmaj = lambda a: jnp.transpose(a, (1, 0, 2))
    yr_s_t, h_s = _rnn_sample(tmaj(zs3[:, :, OFF_XR:OFF_XR + D_RNN]), tmaj(zs3[:, :, OFF_GR:OFF_GR + D_RNN]),
                              tmaj(state_conv[l]), state_rglru[l], cw, cb, wr, br, wi, bi, lam)
    yr_s = tmaj(yr_s_t).reshape(ns, D_RNN)
    pos_s = PAST_LEN + jnp.arange(DS, dtype=jnp.int32)
    tabs_s = tuple(jnp.tile(a, (DB, 1)) for a in _rope_tables(pos_s))
    q_s, kf_s, kb_s, vb_s = _qk_prep(zs, tabs_s, qg, kg, ones_q, ones_k)
    tq = BF16_TILE_ROWS
    pad_t = lambda a, rows: jnp.pad(a.reshape(DB, DS, -1), ((0, 0), (0, rows - DS), (0, 0)))
    ck = cache_k[l].reshape(DB, wbuf, D_KV)
    cv = cache_v[l].reshape(DB, wbuf, D_KV)
    o_s, sample_k, sample_v = _attn_sample(
        sinks, pad_t(q_s, tq), pad_t(kb_s, tq), pad_t(vb_s, tq), pad_t(kf_s, SUBLANES),
        pad_t(zs3[:, :, OFF_V:OFF_V + D_KV], SUBLANES), ck, cv, DS)
    o_s = o_s[:, :DS].reshape(ns, D_Q)
    y_s = _token_pipeline_tail(xs, zs, yr_s, o_s, w, p_sample[l].reshape(ns, D_PLE))

    sample_conv = jnp.concatenate([state_conv[l], zs3[:, :, OFF_XR:OFF_XR + D_RNN]], axis=1)[:, DS:]
    sample_k = sample_k.reshape(DB, wbuf, N_KV_HEADS, HEAD_DIM)
    sample_v = sample_v.reshape(DB, wbuf, N_KV_HEADS, HEAD_DIM)

    return (y_p.reshape(B, S, D_MODEL), y_s.reshape(DB, DS, D_MODEL),
            prompt_conv[None], h_p.reshape(1, B, D_RNN), prompt_k[None], prompt_v[None],
            sample_conv[None], h_s[None], sample_k[None], sample_v[None])
```

```python
import functools

import jax
import jax.numpy as jnp
import numpy as np
from jax import lax
from jax.experimental import pallas as pl
from jax.experimental.pallas import tpu as pltpu

D_MODEL = 2048
D_RNN = D_MODEL // 2
RNN_BLOCKS = 8
RNN_BLOCK = D_RNN // RNN_BLOCKS
CONV_W = 4
LRU_C = 8.0
HEAD_DIM = 64
N_Q_HEADS = D_MODEL // 2 // HEAD_DIM
N_KV_HEADS = 4
GROUP = N_Q_HEADS // N_KV_HEADS
D_Q = N_Q_HEADS * HEAD_DIM
D_KV = N_KV_HEADS * HEAD_DIM
WINDOW = 128
ROT_DIM = HEAD_DIM // 4
ROPE_THETA = 500000.0
N_KEYS = 128
N_EXPERTS = N_KEYS * N_KEYS
PEER_HEADS = 8
PEER_TOPK = 16
D_KEY = 256
D_HALF_KEY = D_KEY // 2
N_PICKS = PEER_HEADS * PEER_TOPK
D_PLE = 256
EPS = 1e-6
NEG_INF = -1e30
PAST_LEN = 16384

LANES = 128
SUBLANES = 8
BF16_TILE_ROWS = 16
VMEM_LIMIT_BYTES = 56 * 1024 * 1024

OFF_GA = 0
OFF_GB = OFF_GA + D_MODEL
OFF_XR = OFF_GB + D_MODEL
OFF_GR = OFF_XR + D_RNN
OFF_Q = OFF_GR + D_RNN
OFF_K = OFF_Q + D_Q
OFF_V = OFF_K + D_KV
D_IN = OFF_V + D_KV

BF16 = jnp.bfloat16
F32 = jnp.float32


def _params(*sem):
    return pltpu.CompilerParams(dimension_semantics=sem, vmem_limit_bytes=VMEM_LIMIT_BYTES)


def _rmsnorm_rows(x, g):
    return x * lax.rsqrt(jnp.mean(x * x, axis=-1, keepdims=True) + EPS) * g


def _dot(a, b):
    return jnp.dot(a, b, preferred_element_type=F32)


def _dot_nt(a, b):
    return lax.dot_general(a, b, (((1,), (1,)), ((), ())), preferred_element_type=F32)


def _in_proj_kernel(x_ref, g_ref, w_ref, z_ref, xn_ref):
    @pl.when(pl.program_id(1) == 0)
    def _():
        xn_ref[...] = _rmsnorm_rows(x_ref[...], g_ref[...]).astype(BF16)

    z_ref[...] = _dot(xn_ref[...], w_ref[...])


def _in_proj(x, g, w, tm=1024, tn=1536):
    n = x.shape[0]
    tm = min(tm, n)
    return pl.pallas_call(
        _in_proj_kernel,
        out_shape=jax.ShapeDtypeStruct((n, D_IN), F32),
        grid=(n // tm, D_IN // tn),
        in_specs=[
            pl.BlockSpec((tm, D_MODEL), lambda i, j: (i, 0)),
            pl.BlockSpec((1, D_MODEL), lambda i, j: (0, 0)),
            pl.BlockSpec((D_MODEL, tn), lambda i, j: (0, j)),
        ],
        out_specs=pl.BlockSpec((tm, tn), lambda i, j: (i, j)),
        scratch_shapes=[pltpu.VMEM((tm, D_MODEL), BF16)],
        compiler_params=_params("parallel", "arbitrary"),
        name="in_proj",
    )(x, g, w)


def _softplus(x):
    return jnp.maximum(x, 0.0) + jnp.log(1.0 + jnp.exp(-jnp.abs(x)))


def _neg_expm1(x):
    t = jnp.tanh(-0.5 * x)
    return 2.0 * t / (1.0 + t)


def _lru_coeffs(xc, wr, br, wi, bi, lam, first_is_pos0):
    xb = xc.astype(BF16)
    r = jax.nn.sigmoid(_dot(xb, wr) + br)
    i = jax.nn.sigmoid(_dot(xb, wi) + bi)
    log_a = -LRU_C * r * _softplus(-lam)
    a = jnp.exp(log_a)
    mult = jnp.sqrt(_neg_expm1(2.0 * log_a))
    if first_is_pos0 is not None:
        row = lax.broadcasted_iota(jnp.int32, xc.shape, 0)
        mult = jnp.where(jnp.logical_and(first_is_pos0, row == 0), 1.0, mult)
    return a, mult * i * xc


def _rnn_prompt_kernel(xr_ref, gr_ref, cw_ref, cb_ref, wr_ref, br_ref, wi_ref, bi_ref, lam_ref,
                       yr_ref, hlast_ref, tail_ref, h_ref, *, tb):
    t = pl.program_id(1)

    @pl.when(t == 0)
    def _():
        tail_ref[...] = jnp.zeros_like(tail_ref)
        h_ref[...] = jnp.zeros_like(h_ref)

    x = xr_ref[...]
    tail = tail_ref[...]
    row8 = lax.broadcasted_iota(jnp.int32, (SUBLANES, D_RNN), 0)
    cw = cw_ref[...]
    xc = cb_ref[...] + cw[CONV_W - 1:CONV_W, :] * x
    for k in range(1, CONV_W):
        xs = pltpu.roll(x, k, axis=0)
        head = jnp.where(row8 < k, pltpu.roll(tail, k, axis=0), xs[:SUBLANES])
        xs = jnp.concatenate([head, xs[SUBLANES:]], axis=0)
        xc = xc + cw[CONV_W - 1 - k:CONV_W - k, :] * xs
    tail_ref[...] = x[tb - SUBLANES:, :]

    a, b = _lru_coeffs(xc, wr_ref[...], br_ref[...], wi_ref[...], bi_ref[...], lam_ref[...], t == 0)

    ng = tb // SUBLANES
    a3 = a.reshape(ng, SUBLANES, D_RNN)
    b3 = b.reshape(ng, SUBLANES, D_RNN)
    sub = lax.broadcasted_iota(jnp.int32, (ng, SUBLANES, D_RNN), 1)
    d = 1
    while d < SUBLANES:
        a_sh = pltpu.roll(a3, d, axis=1)
        b_sh = pltpu.roll(b3, d, axis=1)
        keep = sub < d
        b3 = jnp.where(keep, b3, a3 * b_sh + b3)
        a3 = jnp.where(keep, a3, a3 * a_sh)
        d *= 2
    carry = h_ref[...]
    hs = []
    for g in range(ng):
        hs.append(b3[g] + a3[g] * carry)
        carry = hs[-1][SUBLANES - 1:SUBLANES, :]
    h_ref[...] = carry
    hlast_ref[0] = carry
    yr_ref[...] = (jnp.concatenate(hs, axis=0) * jax.nn.gelu(gr_ref[...])).astype(BF16)


def _rnn_prompt(z, batch, seq, cw, cb, wr, br, wi, bi, lam, tb=256):
    nt = seq // tb
    vec = pl.BlockSpec((1, D_RNN), lambda b, t: (0, 0))
    mat = pl.BlockSpec((D_RNN, D_RNN), lambda b, t: (0, 0))
    return pl.pallas_call(
        functools.partial(_rnn_prompt_kernel, tb=tb),
        out_shape=(jax.ShapeDtypeStruct((batch * seq, D_RNN), BF16),
                   jax.ShapeDtypeStruct((batch, 1, D_RNN), F32)),
        grid=(batch, nt),
        in_specs=[
            pl.BlockSpec((tb, D_RNN), lambda b, t: (b * nt + t, OFF_XR // D_RNN)),
            pl.BlockSpec((tb, D_RNN), lambda b, t: (b * nt + t, OFF_GR // D_RNN)),
            pl.BlockSpec((CONV_W, D_RNN), lambda b, t: (0, 0)),
            vec, mat, vec, mat, vec, vec,
        ],
        out_specs=(pl.BlockSpec((tb, D_RNN), lambda b, t: (b * nt + t, 0)),
                   pl.BlockSpec((1, 1, D_RNN), lambda b, t: (b, 0, 0))),
        scratch_shapes=[pltpu.VMEM((SUBLANES, D_RNN), F32), pltpu.VMEM((1, D_RNN), F32)],
        compiler_params=_params("parallel", "arbitrary"),
        name="rnn_prompt",
    )(z, z, cw, cb, wr, br, wi, bi, lam)


def _rnn_sample_kernel(xr_ref, gr_ref, buf_ref, h0_ref, cw_ref, cb_ref, wr_ref, br_ref, wi_ref,
                       bi_ref, lam_ref, yr_ref, hlast_ref, *, steps):
    cw = cw_ref[...]
    xp = [buf_ref[k] for k in range(CONV_W - 1)] + [xr_ref[s] for s in range(steps)]
    h = h0_ref[...]
    for s in range(steps):
        xc = cb_ref[...] + sum(cw[k:k + 1, :] * xp[s + k] for k in range(CONV_W))
        a, b = _lru_coeffs(xc, wr_ref[...], br_ref[...], wi_ref[...], bi_ref[...], lam_ref[...], None)
        h = a * h + b
        yr_ref[s] = (h * jax.nn.gelu(gr_ref[s])).astype(BF16)
    hlast_ref[...] = h


def _rnn_sample(xr_t, gr_t, buf_t, h0, cw, cb, wr, br, wi, bi, lam):
    steps, db, _ = xr_t.shape
    return pl.pallas_call(
        functools.partial(_rnn_sample_kernel, steps=steps),
        out_shape=(jax.ShapeDtypeStruct((steps, db, D_RNN), BF16),
                   jax.ShapeDtypeStruct((db, D_RNN), F32)),
        compiler_params=pltpu.CompilerParams(vmem_limit_bytes=VMEM_LIMIT_BYTES),
        name="rnn_sample",
    )(xr_t, gr_t, buf_t, h0, cw, cb, wr, br, wi, bi, lam)


def _headnorm_rope(x, gain, seg_ones, cos_t, sin_up_t, sin_dn_t, scale):
    w = x.shape[1]
    sq = x * x
    hi = sq.astype(BF16)
    lo = (sq - hi.astype(F32)).astype(BF16)
    ms = (_dot(hi, seg_ones) + _dot(lo, seg_ones)) * (1.0 / HEAD_DIM)
    xn = x * lax.rsqrt(ms + EPS)
    outs = []
    for c in range(w // LANES):
        xt = xn[:, c * LANES:(c + 1) * LANES] * gain
        up = pltpu.roll(xt, LANES - ROT_DIM // 2, axis=1)
        dn = pltpu.roll(xt, ROT_DIM // 2, axis=1)
        outs.append((xt * cos_t + up * sin_up_t + dn * sin_dn_t) * scale)
    return jnp.concatenate(outs, axis=1)


def _qk_prep_kernel(q_ref, k_ref, v_ref, cos_ref, sup_ref, sdn_ref, qg_ref, kg_ref, oq_ref, ok_ref,
                    qo_ref, kf_ref, kb_ref, vb_ref):
    cos_t, sup, sdn = cos_ref[...], sup_ref[...], sdn_ref[...]
    q = _headnorm_rope(q_ref[...], qg_ref[...], oq_ref[...], cos_t, sup, sdn, HEAD_DIM ** -0.5)
    q = jnp.concatenate([q[:, h * HEAD_DIM:(h + 1) * HEAD_DIM] for h in _head_perm()], axis=1)
    qo_ref[...] = q.astype(BF16)
    k = _headnorm_rope(k_ref[...], kg_ref[...], ok_ref[...], cos_t, sup, sdn, 1.0)
    kf_ref[...] = k
    kb_ref[...] = k.astype(BF16)
    vb_ref[...] = v_ref[...].astype(BF16)


def _qk_prep(z, tabs, qg, kg, ones_q, ones_k, tm=512):
    n = z.shape[0]
    tm = min(tm, n)
    cos_t, sup_t, sdn_t = tabs
    ntab = cos_t.shape[0] // tm
    tab = pl.BlockSpec((tm, LANES), lambda i: (i % ntab, 0))
    const = lambda shape: pl.BlockSpec(shape, lambda i: (0, 0))
    return pl.pallas_call(
        _qk_prep_kernel,
        out_shape=(jax.ShapeDtypeStruct((n, D_Q), BF16), jax.ShapeDtypeStruct((n, D_KV), F32),
                   jax.ShapeDtypeStruct((n, D_KV), BF16), jax.ShapeDtypeStruct((n, D_KV), BF16)),
        grid=(n // tm,),
        in_specs=[
            pl.BlockSpec((tm, D_Q), lambda i: (i, OFF_Q // D_Q)),
            pl.BlockSpec((tm, D_KV), lambda i: (i, OFF_K // D_KV)),
            pl.BlockSpec((tm, D_KV), lambda i: (i, OFF_V // D_KV)),
            tab, tab, tab,
            const((1, LANES)), const((1, LANES)), const((D_Q, D_Q)), const((D_KV, D_KV)),
        ],
        out_specs=(pl.BlockSpec((tm, D_Q), lambda i: (i, 0)), pl.BlockSpec((tm, D_KV), lambda i: (i, 0)),
                   pl.BlockSpec((tm, D_KV), lambda i: (i, 0)), pl.BlockSpec((tm, D_KV), lambda i: (i, 0))),
        compiler_params=_params("parallel"),
        name="qk_prep",
    )(z, z, z, cos_t, sup_t, sdn_t, qg, kg, ones_q, ones_k)


def _sink_attention(q, k, v, valid, sink_ref, tq):
    lane_head = lax.broadcasted_iota(jnp.int32, (tq, D_KV), 1) // HEAD_DIM
    rowg = lax.broadcasted_iota(jnp.int32, (GROUP * tq, 1), 0) // tq
    validg = jnp.concatenate([valid] * GROUP, axis=0)
    out = [jnp.zeros((tq, D_KV), F32) for _ in range(GROUP)]
    for j in range(N_KV_HEADS):
        sel = lane_head == j
        keep = jnp.where(sel, 1.0, 0.0).astype(BF16)
        qs = jnp.concatenate([q[:, g * D_KV:(g + 1) * D_KV] * keep for g in range(GROUP)], axis=0)
        s = jnp.where(validg, _dot_nt(qs, k), NEG_INF)
        sk = jnp.zeros((GROUP * tq, 1), F32)
        for g in range(GROUP):
            sk = jnp.where(rowg == g, sink_ref[j * GROUP + g], sk)
        m = jnp.maximum(jnp.max(s, axis=-1, keepdims=True), sk)
        p = jnp.exp(s - m)
        denom = jnp.sum(p, axis=-1, keepdims=True) + jnp.exp(sk - m)
        pv = _dot(p.astype(BF16), v) / denom
        for g in range(GROUP):
            out[g] = jnp.where(sel, pv[g * tq:(g + 1) * tq], out[g])
    return jnp.concatenate(out, axis=1)


def _attn_prompt_kernel(sink_ref, q_ref, kp_ref, kc_ref, vp_ref, vc_ref, o_ref):
    nb = pl.program_id(1)
    k = jnp.concatenate([kp_ref[...], kc_ref[...]], axis=0)
    v = jnp.concatenate([vp_ref[...], vc_ref[...]], axis=0)
    i = lax.broadcasted_iota(jnp.int32, (WINDOW, 2 * WINDOW), 0)
    j = lax.broadcasted_iota(jnp.int32, (WINDOW, 2 * WINDOW), 1)
    d = WINDOW + i - j
    valid = (d >= 0) & (d < WINDOW) & ((j >= WINDOW) | (nb > 0))
    o_ref[...] = _sink_attention(q_ref[...], k, v, valid, sink_ref, WINDOW).astype(BF16)


def _attn_prompt(sinks, q, kb, vb, batch, seq):
    nblk = seq // WINDOW
    cur = lambda w: pl.BlockSpec((WINDOW, w), lambda b, t: (b * nblk + t, 0))
    prev = lambda w: pl.BlockSpec((WINDOW, w), lambda b, t: (b * nblk + jnp.maximum(t - 1, 0), 0))
    return pl.pallas_call(
        _attn_prompt_kernel,
        out_shape=jax.ShapeDtypeStruct((batch * seq, D_Q), BF16),
        grid=(batch, nblk),
        in_specs=[pl.BlockSpec(memory_space=pltpu.SMEM), cur(D_Q), prev(D_KV), cur(D_KV),
                  prev(D_KV), cur(D_KV)],
        out_specs=cur(D_Q),
        compiler_params=_params("parallel", "arbitrary"),
        name="attn_prompt",
    )(sinks, q, kb, kb, vb, vb)


def _shift_in(cache, new8, steps):
    wbuf = cache.shape[0]
    rolled = pltpu.roll(cache, wbuf - steps, axis=0)
    row8 = lax.broadcasted_iota(jnp.int32, new8.shape, 0)
    tail = jnp.where(row8 >= SUBLANES - steps, pltpu.roll(new8, SUBLANES - steps, axis=0),
                     rolled[wbuf - SUBLANES:])
    return jnp.concatenate([rolled[:wbuf - SUBLANES], tail], axis=0)


def _attn_sample_kernel(sink_ref, q_ref, kn_ref, vn_ref, knf_ref, vnf_ref, ck_ref, cv_ref, o_ref, ok_ref, ov_ref,
                        *, bs, tq, wbuf, steps):
    t = lax.broadcasted_iota(jnp.int32, (tq, wbuf + tq), 0)
    c = lax.broadcasted_iota(jnp.int32, (tq, wbuf + tq), 1)
    d = wbuf + t - c
    valid = (d >= 0) & (d < WINDOW)
    for s in range(bs):
        ck, cv = ck_ref[s], cv_ref[s]
        k = jnp.concatenate([ck.astype(BF16), kn_ref[s]], axis=0)
        v = jnp.concatenate([cv.astype(BF16), vn_ref[s]], axis=0)
        o_ref[s] = _sink_attention(q_ref[s], k, v, valid, sink_ref, tq).astype(BF16)
        ok_ref[s] = _shift_in(ck, knf_ref[s], steps)
        ov_ref[s] = _shift_in(cv, vnf_ref[s], steps)


def _attn_sample(sinks, q3, kn3, vn3, knf3, vnf3, cache_k, cache_v, steps, bs=8):
    db, tq, _ = q3.shape
    wbuf = cache_k.shape[1]
    assert steps <= SUBLANES and knf3.shape[1] == SUBLANES and db % bs == 0
    blk = lambda r, w: pl.BlockSpec((bs, r, w), lambda b: (b, 0, 0))
    cache = jax.ShapeDtypeStruct((db, wbuf, D_KV), F32)
    return pl.pallas_call(
        functools.partial(_attn_sample_kernel, bs=bs, tq=tq, wbuf=wbuf, steps=steps),
        out_shape=(jax.ShapeDtypeStruct((db, tq, D_Q), BF16), cache, cache),
        grid=(db // bs,),
        in_specs=[pl.BlockSpec(memory_space=pltpu.SMEM), blk(tq, D_Q), blk(tq, D_KV), blk(tq, D_KV),
                  blk(SUBLANES, D_KV), blk(SUBLANES, D_KV), blk(wbuf, D_KV), blk(wbuf, D_KV)],
        out_specs=(blk(tq, D_Q), blk(wbuf, D_KV), blk(wbuf, D_KV)),
        compiler_params=_params("parallel"),
        name="attn_sample",
    )(sinks, q3, kn3, vn3, knf3, vnf3, cache_k, cache_v)


def _merge_kernel(yr_ref, o_ref, ga_ref, gb_ref, wr_ref, wa_ref, m_ref):
    a = _dot(yr_ref[...], wr_ref[...])
    b = _dot(o_ref[...], wa_ref[...])
    m_ref[...] = (jax.nn.sigmoid(ga_ref[...]) * a + jax.nn.sigmoid(gb_ref[...]) * b).astype(BF16)


def _merge(yr, o, z, wr, wa, tm=512, tn=1024):
    n = yr.shape[0]
    tm = min(tm, n)
    nj = D_MODEL // tn
    return pl.pallas_call(
        _merge_kernel,
        out_shape=jax.ShapeDtypeStruct((n, D_MODEL), BF16),
        grid=(n // tm, nj),
        in_specs=[
            pl.BlockSpec((tm, D_RNN), lambda i, j: (i, 0)),
            pl.BlockSpec((tm, D_Q), lambda i, j: (i, 0)),
            pl.BlockSpec((tm, tn), lambda i, j: (i, OFF_GA // tn + j)),
            pl.BlockSpec((tm, tn), lambda i, j: (i, OFF_GB // tn + j)),
            pl.BlockSpec((D_RNN, tn), lambda i, j: (0, j)),
            pl.BlockSpec((D_Q, tn), lambda i, j: (0, j)),
        ],
        out_specs=pl.BlockSpec((tm, tn), lambda i, j: (i, j)),
        compiler_params=_params("parallel", "arbitrary"),
        name="merge",
    )(yr, o, z, z, wr, wa)


def _out_proj_kernel(x_ref, m_ref, w_ref, g_ref, x1_ref, n2_ref):
    x1 = x_ref[...] + _dot(m_ref[...], w_ref[...])
    x1_ref[...] = x1
    n2_ref[...] = _rmsnorm_rows(x1, g_ref[...]).astype(BF16)


def _out_proj(x, m, w, g, tm=512):
    n = x.shape[0]
    tm = min(tm, n)
    row = lambda dt: pl.BlockSpec((tm, D_MODEL), lambda i: (i, 0))
    return pl.pallas_call(
        _out_proj_kernel,
        out_shape=(jax.ShapeDtypeStruct((n, D_MODEL), F32), jax.ShapeDtypeStruct((n, D_MODEL), BF16)),
        grid=(n // tm,),
        in_specs=[row(F32), row(BF16), pl.BlockSpec((D_MODEL, D_MODEL), lambda i: (0, 0)),
                  pl.BlockSpec((1, D_MODEL), lambda i: (0, 0))],
        out_specs=(row(F32), row(BF16)),
        compiler_params=_params("parallel"),
        name="out_proj",
    )(x, m, w, g)


def _topk_rows(s, k, rid=None):
    rows, t = s.shape
    if rid is None:
        rid = lax.broadcasted_iota(jnp.int32, (rows, t), 0).astype(F32)
    slot = lax.broadcasted_iota(jnp.int32, (k, t), 0)
    vals = jnp.zeros((k, t), F32)
    ids = jnp.zeros((k, t), F32)
    for r in range(k):
        m = jnp.max(s, axis=0, keepdims=True)
        i = jnp.min(jnp.where(s == m, rid, jnp.inf), axis=0, keepdims=True)
        vals = jnp.where(slot == r, m, vals)
        ids = jnp.where(slot == r, i, ids)
        s = jnp.where(rid == i, -jnp.inf, s)
    return vals, ids


def _route_tokens(n2, wq, sk_ref):
    q = _dot(n2, wq).astype(BF16)
    gates, experts = [], []
    for h in range(PEER_HEADS):
        top = []
        for p in range(2):
            c = (2 * h + p) * D_HALF_KEY
            s = _dot_nt(sk_ref[2 * h + p], q[:, c:c + D_HALF_KEY])
            top.append(_topk_rows(s, PEER_TOPK))
        (s1, i1), (s2, i2) = top
        hk = PEER_TOPK // 2
        tcol = s1.shape[1]
        brow = lambda n: lax.broadcasted_iota(jnp.int32, (n, tcol), 0).astype(F32)
        cand = [s1[0:1, :] + s2]
        cidx = [i1[0:1, :] * float(N_KEYS) + i2]
        rid = [brow(PEER_TOPK)]
        for a in range(1, hk):
            cand.append(s1[a:a + 1, :] + s2[0:hk, :])
            cidx.append(i1[a:a + 1, :] * float(N_KEYS) + i2[0:hk, :])
            rid.append(brow(hk) + float(a * PEER_TOPK))
        cand.append(s1[hk:, :] + s2[0:1, :])
        cidx.append(i1[hk:, :] * float(N_KEYS) + i2[0:1, :])
        rid.append((brow(hk) + float(hk)) * float(PEER_TOPK))
        cand, cidx, rid = (jnp.concatenate(x, axis=0) for x in (cand, cidx, rid))
        top_s, pos = _topk_rows(cand, PEER_TOPK, rid)
        slot = lax.broadcasted_iota(jnp.int32, top_s.shape, 0)
        ids = jnp.zeros(top_s.shape, F32)
        for r in range(PEER_TOPK):
            e = jnp.max(jnp.where(rid == pos[r:r + 1, :], cidx, -1.0), axis=0, keepdims=True)
            ids = jnp.where(slot == r, e, ids)
        w = jnp.exp(top_s - top_s[0:1, :])
        gates.append(w / jnp.sum(w, axis=0, keepdims=True))
        experts.append(ids)
    return jnp.concatenate(experts, axis=0).T.astype(jnp.int32), jnp.concatenate(gates, axis=0).T


def _peer_route_kernel(n2_ref, wq_ref, sk_ref, idx_ref, gate_ref):
    idx_ref[...], gate_ref[...] = _route_tokens(n2_ref[...], wq_ref[...], sk_ref)


def _peer_route(n2, wq, sk, tm=256):
    n = n2.shape[0]
    tm = min(tm, n)
    return pl.pallas_call(
        _peer_route_kernel,
        out_shape=(jax.ShapeDtypeStruct((n, N_PICKS), jnp.int32), jax.ShapeDtypeStruct((n, N_PICKS), F32)),
        grid=(n // tm,),
        in_specs=[pl.BlockSpec((tm, D_MODEL), lambda i: (i, 0)),
                  pl.BlockSpec((D_MODEL, PEER_HEADS * D_KEY), lambda i: (0, 0)),
                  pl.BlockSpec((2 * PEER_HEADS, N_KEYS, D_HALF_KEY), lambda i: (0, 0, 0))],
        out_specs=(pl.BlockSpec((tm, N_PICKS), lambda i: (i, 0)), pl.BlockSpec((tm, N_PICKS), lambda i: (i, 0))),
        compiler_params=_params("parallel"),
        name="peer_route",
    )(n2, wq, sk)


D_TILES = D_MODEL // LANES
assert D_TILES == BF16_TILE_ROWS
PICK_GROUP = 16
WAIT_GROUP = 4


def _peer_mix_kernel(idx_ref, idxn_ref, x_ref, gate_ref, exp_ref, sel_ref, tab_ref, o_ref, buf_a, buf_b,
                     sem_ref, *, tb, nsteps):
    i = pl.program_id(0)
    bufs = (buf_a, buf_b)

    group_rows = WAIT_GROUP * N_PICKS

    def start_token(ids_ref, row, half, t, picks=(0, N_PICKS)):
        for j in range(*picks):
            pltpu.make_async_copy(tab_ref.at[ids_ref[row, j]], bufs[half].at[t * N_PICKS + j],
                                  sem_ref.at[half, t // WAIT_GROUP]).start(priority=j % 2)

    def wait_group(half, g):
        pltpu.make_async_copy(tab_ref.at[pl.ds(0, group_rows)], bufs[half].at[pl.ds(g * group_rows, group_rows)],
                              sem_ref.at[half, g]).wait()

    @pl.when(i == 0)
    def _():
        for t in range(tb):
            start_token(idx_ref, t, 0, t)

    rows = N_PICKS * D_TILES
    diag = (lax.broadcasted_iota(jnp.int32, (D_TILES, rows), 1) % D_TILES
            == lax.broadcasted_iota(jnp.int32, (D_TILES, rows), 0))

    sel = sel_ref[...]
    lane = lax.broadcasted_iota(jnp.int32, (N_PICKS, LANES), 1)

    first_picks, second_picks = (0, N_PICKS // 2), (N_PICKS // 2, N_PICKS)

    def run_tile(half, start_other):
        buf = bufs[half]
        hcols = jnp.zeros((N_PICKS, LANES), F32)
        for t in range(tb):
            if t % WAIT_GROUP == 0:
                wait_group(half, t // WAIT_GROUP)
            start_other(t, first_picks)
            u3 = buf[t * N_PICKS:(t + 1) * N_PICKS, 0:D_TILES, :]
            p = (u3 * x_ref[half * tb + t][None, :, :]).reshape(rows, LANES)
            grp = PICK_GROUP * D_TILES
            r = jnp.concatenate([_dot(sel, p[c * grp:(c + 1) * grp]) for c in range(N_PICKS // PICK_GROUP)],
                                axis=0)
            hcols = jnp.where(lane == t, jnp.sum(r, axis=1, keepdims=True), hcols)
        h = hcols.T[0:tb, :]
        w = (jax.nn.gelu(h) * gate_ref[half * tb:(half + 1) * tb, :]).astype(BF16)
        wrow = _dot(w, exp_ref[...])
        for t in range(tb):
            start_other(t, second_picks)
            v = buf[t * N_PICKS:(t + 1) * N_PICKS, D_TILES:2 * D_TILES, :].reshape(rows, LANES)
            wm = jnp.where(diag, wrow[t:t + 1, :], 0.0).astype(BF16)
            o_ref[half * tb + t] = _dot(wm, v)

    run_tile(0, lambda t, picks: start_token(idx_ref, tb + t, 1, t, picks))
    run_tile(1, lambda t, picks: start_token(idxn_ref, t, 0, t, picks))

    @pl.when(i == nsteps - 1)
    def _():
        for g in range(tb // WAIT_GROUP):
            wait_group(0, g)


def _peer_mix(idx, gates, n2, table, expand, sel, tb=BF16_TILE_ROWS):
    n = idx.shape[0]
    nsteps = n // (2 * tb)
    x3 = n2.reshape(n, D_TILES, LANES)
    rows = N_PICKS * D_TILES
    out = pl.pallas_call(
        functools.partial(_peer_mix_kernel, tb=tb, nsteps=nsteps),
        out_shape=jax.ShapeDtypeStruct((n, D_TILES, LANES), F32),
        grid=(nsteps,),
        in_specs=[
            pl.BlockSpec((2 * tb, N_PICKS), lambda i: (i, 0), memory_space=pltpu.SMEM),
            pl.BlockSpec((tb, N_PICKS), lambda i: (jnp.minimum(2 * i + 2, 2 * nsteps - 2), 0),
                         memory_space=pltpu.SMEM),
            pl.BlockSpec((2 * tb, D_TILES, LANES), lambda i: (i, 0, 0)),
            pl.BlockSpec((2 * tb, N_PICKS), lambda i: (i, 0)),
            pl.BlockSpec((N_PICKS, rows), lambda i: (0, 0)),
            pl.BlockSpec((PICK_GROUP, PICK_GROUP * D_TILES), lambda i: (0, 0)),
            pl.BlockSpec(memory_space=pl.ANY),
        ],
        out_specs=pl.BlockSpec((2 * tb, D_TILES, LANES), lambda i: (i, 0, 0)),
        scratch_shapes=[pltpu.VMEM((tb * N_PICKS, 2 * D_TILES, LANES), BF16),
                        pltpu.VMEM((tb * N_PICKS, 2 * D_TILES, LANES), BF16),
                        pltpu.SemaphoreType.DMA((2, tb // WAIT_GROUP))],
        compiler_params=_params("arbitrary"),
        name="peer_mix",
    )(idx, idx, x3, gates, expand, sel, table)
    return out.reshape(n, D_MODEL)


ROUTE_CHUNK = 128


def _peer_fused_kernel(idx0_ref, gate0_ref, wq_hbm, x_ref, n2r_ref, sk_ref, exp_ref, sel_ref, tab_ref, o_ref,
                       buf_a, buf_b, wq_ref, idx_smem, idx_vmem, gate_ring, sem_ref, aux_sem, *, tb, nsteps, nchunks):
    i = pl.program_id(0)
    bufs = (buf_a, buf_b)
    spc = ROUTE_CHUNK // (2 * tb)
    group_rows = WAIT_GROUP * N_PICKS

    def ring_pos(step):
        return (step // spc) % 2, (step % spc) * (2 * tb)

    def start_token(slot, row, half, t, picks=(0, N_PICKS)):
        for j in range(*picks):
            pltpu.make_async_copy(tab_ref.at[idx_smem[slot, row, j]], bufs[half].at[t * N_PICKS + j],
                                  sem_ref.at[half, t // WAIT_GROUP]).start(priority=j % 2)

    def wait_group(half, g):
        pltpu.make_async_copy(tab_ref.at[pl.ds(0, group_rows)], bufs[half].at[pl.ds(g * group_rows, group_rows)],
                              sem_ref.at[half, g]).wait()

    def blocking_copy(src, dst):
        cp = pltpu.make_async_copy(src, dst, aux_sem)
        cp.start()
        cp.wait()

    @pl.when(i == 0)
    def _():
        blocking_copy(idx0_ref, idx_smem.at[0])
        blocking_copy(gate0_ref, gate_ring.at[0])
        blocking_copy(wq_hbm, wq_ref)
        for t in range(tb):
            start_token(0, t, 0, t)

    has_next = i // spc + 1 < nchunks
    ids_copy = pltpu.make_async_copy(idx_vmem, idx_smem.at[(i // spc + 1) % 2], aux_sem)

    @pl.when(jnp.logical_and(i % spc == 0, has_next))
    def _():
        ids, gates = _route_tokens(n2r_ref[...], wq_ref[...], sk_ref)
        gate_ring[(i // spc + 1) % 2] = gates
        idx_vmem[...] = ids
        ids_copy.start()

    @pl.when(jnp.logical_and(i % spc == 1, has_next))
    def _():
        ids_copy.wait()

    rows = N_PICKS * D_TILES
    diag = (lax.broadcasted_iota(jnp.int32, (D_TILES, rows), 1) % D_TILES
            == lax.broadcasted_iota(jnp.int32, (D_TILES, rows), 0))
    sel = sel_ref[...]
    lane = lax.broadcasted_iota(jnp.int32, (N_PICKS, LANES), 1)

    first_picks, second_picks = (0, N_PICKS // 2), (N_PICKS // 2, N_PICKS)
    slot, row0 = ring_pos(i)
    nslot, nrow0 = ring_pos(jnp.minimum(i + 1, nsteps - 1))

    def run_tile(half, start_other):
        buf = bufs[half]
        hcols = jnp.zeros((N_PICKS, LANES), F32)
        for t in range(tb):
            if t % WAIT_GROUP == 0:
                wait_group(half, t // WAIT_GROUP)
            start_other(t, first_picks)
            u3 = buf[t * N_PICKS:(t + 1) * N_PICKS, 0:D_TILES, :]
            p = (u3 * x_ref[half * tb + t][None, :, :]).reshape(rows, LANES)
            grp = PICK_GROUP * D_TILES
            r = jnp.concatenate([_dot(sel, p[c * grp:(c + 1) * grp]) for c in range(N_PICKS // PICK_GROUP)],
                                axis=0)
            hcols = jnp.where(lane == t, jnp.sum(r, axis=1, keepdims=True), hcols)
        h = hcols.T[0:tb, :]
        gate = gate_ring[slot, pl.ds(pl.multiple_of(row0 + half * tb, tb), tb), :]
        w = (jax.nn.gelu(h) * gate).astype(BF16)
        wrow = _dot(w, exp_ref[...])
        for t in range(tb):
            start_other(t, second_picks)
            v = buf[t * N_PICKS:(t + 1) * N_PICKS, D_TILES:2 * D_TILES, :].reshape(rows, LANES)
            wm = jnp.where(diag, wrow[t:t + 1, :], 0.0).astype(BF16)
            o_ref[half * tb + t] = _dot(wm, v)

    run_tile(0, lambda t, picks: start_token(slot, row0 + tb + t, 1, t, picks))
    run_tile(1, lambda t, picks: start_token(nslot, nrow0 + t, 0, t, picks))

    @pl.when(i == nsteps - 1)
    def _():
        for g in range(tb // WAIT_GROUP):
            wait_group(0, g)


def _peer_fused(n2, wq, sk, table, expand, sel, tb=BF16_TILE_ROWS):
    n = n2.shape[0]
    assert n % ROUTE_CHUNK == 0 and ROUTE_CHUNK % (2 * tb) == 0
    nsteps, nchunks = n // (2 * tb), n // ROUTE_CHUNK
    spc = ROUTE_CHUNK // (2 * tb)
    idx0, gate0 = _peer_route(n2[:ROUTE_CHUNK], wq, sk, tm=ROUTE_CHUNK)
    x3 = n2.reshape(n, D_TILES, LANES)
    rows = N_PICKS * D_TILES
    out = pl.pallas_call(
        functools.partial(_peer_fused_kernel, tb=tb, nsteps=nsteps, nchunks=nchunks),
        out_shape=jax.ShapeDtypeStruct((n, D_TILES, LANES), F32),
        grid=(nsteps,),
        in_specs=[
            pl.BlockSpec(memory_space=pl.ANY),
            pl.BlockSpec(memory_space=pl.ANY),
            pl.BlockSpec(memory_space=pl.ANY),
            pl.BlockSpec((2 * tb, D_TILES, LANES), lambda i: (i, 0, 0)),
            pl.BlockSpec((ROUTE_CHUNK, D_MODEL), lambda i: (jnp.minimum(i // spc + 1, nchunks - 1), 0)),
            pl.BlockSpec((2 * PEER_HEADS, N_KEYS, D_HALF_KEY), lambda i: (0, 0, 0)),
            pl.BlockSpec((N_PICKS, rows), lambda i: (0, 0)),
            pl.BlockSpec((PICK_GROUP, PICK_GROUP * D_TILES), lambda i: (0, 0)),
            pl.BlockSpec(memory_space=pl.ANY),
        ],
        out_specs=pl.BlockSpec((2 * tb, D_TILES, LANES), lambda i: (i, 0, 0)),
        scratch_shapes=[pltpu.VMEM((tb * N_PICKS, 2 * D_TILES, LANES), BF16),
                        pltpu.VMEM((tb * N_PICKS, 2 * D_TILES, LANES), BF16),
                        pltpu.VMEM((D_MODEL, PEER_HEADS * D_KEY), BF16),
                        pltpu.SMEM((2, ROUTE_CHUNK, N_PICKS), jnp.int32),
                        pltpu.VMEM((ROUTE_CHUNK, N_PICKS), jnp.int32),
                        pltpu.VMEM((2, ROUTE_CHUNK, N_PICKS), F32),
                        pltpu.SemaphoreType.DMA((2, tb // WAIT_GROUP)),
                        pltpu.SemaphoreType.DMA],
        compiler_params=_params("arbitrary"),
        name="peer_fused",
    )(idx0, gate0, wq, x3, n2, sk, expand, sel, table)
    return out.reshape(n, D_MODEL)


def _pack_table_kernel(u_ref, v_ref, o_ref):
    o_ref[:, 0:D_TILES, :] = u_ref[...].astype(BF16)
    o_ref[:, D_TILES:2 * D_TILES, :] = v_ref[...].astype(BF16)


def _pack_table(u3, v3, te=256):
    e = u3.shape[0]
    te = min(te, e)
    blk = pl.BlockSpec((te, D_TILES, LANES), lambda i: (i, 0, 0))
    return pl.pallas_call(
        _pack_table_kernel,
        out_shape=jax.ShapeDtypeStruct((e, 2 * D_TILES, LANES), BF16),
        grid=(e // te,),
        in_specs=[blk, blk],
        out_specs=pl.BlockSpec((te, 2 * D_TILES, LANES), lambda i: (i, 0, 0)),
        compiler_params=_params("parallel"),
        name="pack_table",
    )(u3, v3)


def _ple_kernel(x1_ref, po_ref, ple_ref, g_ref, wp_ref, wg_ref, y_ref):
    x2 = x1_ref[...] + po_ref[...]
    n3 = _rmsnorm_rows(x2, g_ref[...]).astype(BF16)
    emb = _dot(ple_ref[...].astype(BF16), wp_ref[...])
    y_ref[...] = x2 + emb * jax.nn.sigmoid(_dot(n3, wg_ref[...]))


def _ple(x1, po, ple, g, wp, wg, tm=512):
    n = x1.shape[0]
    tm = min(tm, n)
    row = pl.BlockSpec((tm, D_MODEL), lambda i: (i, 0))
    return pl.pallas_call(
        _ple_kernel,
        out_shape=jax.ShapeDtypeStruct((n, D_MODEL), F32),
        grid=(n // tm,),
        in_specs=[row, row, pl.BlockSpec((tm, D_PLE), lambda i: (i, 0)),
                  pl.BlockSpec((1, D_MODEL), lambda i: (0, 0)),
                  pl.BlockSpec((D_PLE, D_MODEL), lambda i: (0, 0)),
                  pl.BlockSpec((D_MODEL, D_MODEL), lambda i: (0, 0))],
        out_specs=row,
        compiler_params=_params("parallel"),
        name="ple",
    )(x1, po, ple, g, wp, wg)


def _rope_tables(pos):
    half = ROT_DIM // 2
    inv = ROPE_THETA ** (-jnp.arange(0, ROT_DIM, 2, dtype=F32) / ROT_DIM)
    ang = pos.astype(F32)[:, None] * inv[None, :]
    cos, sin = jnp.cos(ang), jnp.sin(ang)
    n = pos.shape[0]
    pad = jnp.zeros((n, HEAD_DIM - ROT_DIM), F32)
    zh = jnp.zeros((n, half), F32)
    cos_h = jnp.concatenate([cos, cos, pad + 1.0], axis=1)
    up_h = jnp.concatenate([-sin, zh, pad], axis=1)
    dn_h = jnp.concatenate([zh, sin, pad], axis=1)
    rep = LANES // HEAD_DIM
    return tuple(jnp.tile(a, (1, rep)) for a in (cos_h, up_h, dn_h))


def _block_diag(w):
    eye = jnp.eye(RNN_BLOCKS, dtype=w.dtype)
    return jnp.einsum("ncd,nm->ncmd", w, eye).reshape(D_RNN, D_RNN)


def _head_perm():
    return np.array([j * GROUP + g for g in range(GROUP) for j in range(N_KV_HEADS)])


def _token_pipeline_tail(x, z, yr, o, w, ple):
    m = _merge(yr, o, z, w["proj_rnn"], w["proj_attn"])
    x1, n2 = _out_proj(x, m, w["out"], w["norm_ffn"])
    po = _peer_fused(n2, w["peer_q"], w["sub_keys"], w["peer_table"], w["expand"], w["pick_sum"])
    return _ple(x1, po, ple, w["norm_ple"], w["ple"], w["ple_gate"])


def kernel(x_prompt, x_sample, p_prompt, p_sample, state_conv, state_rglru, cache_k, cache_v, norm_mix, w_in, conv_w, conv_b, w_rgate, b_rgate, w_igate, b_igate, lru_lambda, w_proj_rnn, q_norm, k_norm, attn_sinks, w_proj_attn, w_out, norm_ffn, w_peer_q, peer_sub_keys, peer_u, peer_v, w_ple, norm_ple, w_ple_gate):
    depth = w_in.shape[0]
    assert depth == 1
    l = 0
    B, S, _ = x_prompt.shape
    DB, DS, _ = x_sample.shape
    wbuf = cache_k.shape[2]

    hp = _head_perm()
    offs = np.cumsum([0, D_RNN, D_RNN, D_Q, D_KV, D_KV, D_MODEL, D_MODEL])
    xr_c, gr_c, q_c, k_c, v_c, ga_c, gb_c = [np.arange(offs[i], offs[i + 1]) for i in range(7)]
    cols = np.concatenate([ga_c, gb_c, xr_c, gr_c, q_c, k_c, v_c])
    row2 = lambda a: a[l].reshape(1, -1)
    w = {
        "proj_rnn": w_proj_rnn[l].astype(BF16),
        "proj_attn": w_proj_attn[l].reshape(N_Q_HEADS, HEAD_DIM, D_MODEL)[hp].reshape(D_Q, D_MODEL).astype(BF16),
        "out": w_out[l].astype(BF16),
        "norm_ffn": row2(norm_ffn),
        "peer_q": w_peer_q[l].astype(BF16),
        "sub_keys": peer_sub_keys[l].reshape(2 * PEER_HEADS, N_KEYS, D_HALF_KEY).astype(BF16),
        "peer_table": _pack_table(peer_u[l].reshape(N_EXPERTS, D_TILES, LANES),
                                  peer_v[l].reshape(N_EXPERTS, D_TILES, LANES)),
        "norm_ple": row2(norm_ple),
        "ple": w_ple[l].astype(BF16),
        "ple_gate": w_ple_gate[l].astype(BF16),
    }
    expand = np.repeat(np.eye(N_PICKS, dtype=np.float32), D_TILES, axis=1)
    w["expand"] = jnp.asarray(expand, BF16)
    w["pick_sum"] = jnp.asarray(expand[:PICK_GROUP, :PICK_GROUP * D_TILES], BF16)
    runs = np.split(cols, np.flatnonzero(np.diff(cols) != 1) + 1)
    w_in_b = jnp.concatenate([w_in[l][:, r[0]:r[-1] + 1] for r in runs], axis=1).astype(BF16)
    g_mix = row2(norm_mix)
    cw, cb = conv_w[l], row2(conv_b)
    wr, br = _block_diag(w_rgate[l]).astype(BF16), row2(b_rgate)
    wi, bi = _block_diag(w_igate[l]).astype(BF16), row2(b_igate)
    lam = row2(lru_lambda)
    rep = LANES // HEAD_DIM
    qg = jnp.tile(q_norm[l], rep).reshape(1, LANES)
    kg = jnp.tile(k_norm[l], rep).reshape(1, LANES)
    seg = lambda width: jnp.asarray(
        np.kron(np.eye(width // HEAD_DIM, dtype=np.float32), np.ones((HEAD_DIM, HEAD_DIM), np.float32)), BF16)
    ones_q, ones_k = seg(D_Q), seg(D_KV)
    sinks = attn_sinks[l]

    xp = x_prompt.reshape(B * S, D_MODEL)
    zp = _in_proj(xp, g_mix, w_in_b)
    yr_p, h_p = _rnn_prompt(zp, B, S, cw, cb, wr, br, wi, bi, lam)
    tabs_p = _rope_tables(jnp.arange(S, dtype=jnp.int32))
    q_p, kf_p, kb_p, vb_p = _qk_prep(zp, tabs_p, qg, kg, ones_q, ones_k)
    o_p = _attn_prompt(sinks, q_p, kb_p, vb_p, B, S)
    y_p = _token_pipeline_tail(xp, zp, yr_p, o_p, w, p_prompt[l].reshape(B * S, D_PLE))

    zp3 = zp.reshape(B, S, D_IN)
    keep = min(WINDOW, S)
    prompt_conv = zp3[:, S - (CONV_W - 1):, OFF_XR:OFF_XR + D_RNN]
    prompt_k = kf_p.reshape(B, S, N_KV_HEADS, HEAD_DIM)[:, S - keep:]
    prompt_v = zp3[:, S - keep:, OFF_V:OFF_V + D_KV].reshape(B, keep, N_KV_HEADS, HEAD_DIM)

    ns = DB * DS
    xs = x_sample.reshape(ns, D_MODEL)
    zs = _in_proj(xs, g_mix, w_in_b)
    zs3 = zs.reshape(DB, DS, D_IN)
    tmaj = lambda a: jnp.transpose(a, (1, 0, 2))
    yr_s_t, h_s = _rnn_sample(tmaj(zs3[:, :, OFF_XR:OFF_XR + D_RNN]), tmaj(zs3[:, :, OFF_GR:OFF_GR + D_RNN]),
                              tmaj(state_conv[l]), state_rglru[l], cw, cb, wr, br, wi, bi, lam)
    yr_s = tmaj(yr_s_t).reshape(ns, D_RNN)
    pos_s = PAST_LEN + jnp.arange(DS, dtype=jnp.int32)
    tabs_s = tuple(jnp.tile(a, (DB, 1)) for a in _rope_tables(pos_s))
    q_s, kf_s, kb_s, vb_s = _qk_prep(zs, tabs_s, qg, kg, ones_q, ones_k)
    tq = BF16_TILE_ROWS
    pad_t = lambda a, rows: jnp.pad(a.reshape(DB, DS, -1), ((0, 0), (0, rows - DS), (0, 0)))
    ck = cache_k[l].reshape(DB, wbuf, D_KV)
    cv = cache_v[l].reshape(DB, wbuf, D_KV)
    o_s, sample_k, sample_v = _attn_sample(
        sinks, pad_t(q_s, tq), pad_t(kb_s, tq), pad_t(vb_s, tq), pad_t(kf_s, SUBLANES),
        pad_t(zs3[:, :, OFF_V:OFF_V + D_KV], SUBLANES), ck, cv, DS)
    o_s = o_s[:, :DS].reshape(ns, D_Q)
    y_s = _token_pipeline_tail(xs, zs, yr_s, o_s, w, p_sample[l].reshape(ns, D_PLE))

    sample_conv = jnp.concatenate([state_conv[l], zs3[:, :, OFF_XR:OFF_XR + D_RNN]], axis=1)[:, DS:]
    sample_k = sample_k.reshape(DB, wbuf, N_KV_HEADS, HEAD_DIM)
    sample_v = sample_v.reshape(DB, wbuf, N_KV_HEADS, HEAD_DIM)

    return (y_p.reshape(B, S, D_MODEL), y_s.reshape(DB, DS, D_MODEL),
            prompt_conv[None], h_p.reshape(1, B, D_RNN), prompt_k[None], prompt_v[None],
            sample_conv[None], h_s[None], sample_k[None], sample_v[None])
```

```python
import functools

import jax
import jax.numpy as jnp
import numpy as np
from jax import lax
from jax.experimental import pallas as pl
from jax.experimental.pallas import tpu as pltpu

D_MODEL = 2048
D_RNN = D_MODEL // 2
RNN_BLOCKS = 8
RNN_BLOCK = D_RNN // RNN_BLOCKS
CONV_W = 4
LRU_C = 8.0
HEAD_DIM = 64
N_Q_HEADS = D_MODEL // 2 // HEAD_DIM
N_KV_HEADS = 4
GROUP = N_Q_HEADS // N_KV_HEADS
D_Q = N_Q_HEADS * HEAD_DIM
D_KV = N_KV_HEADS * HEAD_DIM
WINDOW = 128
ROT_DIM = HEAD_DIM // 4
ROPE_THETA = 500000.0
N_KEYS = 128
N_EXPERTS = N_KEYS * N_KEYS
PEER_HEADS = 8
PEER_TOPK = 16
D_KEY = 256
D_HALF_KEY = D_KEY // 2
N_PICKS = PEER_HEADS * PEER_TOPK
D_PLE = 256
EPS = 1e-6
NEG_INF = -1e30
PAST_LEN = 16384

LANES = 128
SUBLANES = 8
BF16_TILE_ROWS = 16
VMEM_LIMIT_BYTES = 56 * 1024 * 1024

OFF_GA = 0
OFF_GB = OFF_GA + D_MODEL
OFF_XR = OFF_GB + D_MODEL
OFF_GR = OFF_XR + D_RNN
OFF_Q = OFF_GR + D_RNN
OFF_K = OFF_Q + D_Q
OFF_V = OFF_K + D_KV
D_IN = OFF_V + D_KV

BF16 = jnp.bfloat16
F32 = jnp.float32


def _params(*sem):
    return pltpu.CompilerParams(dimension_semantics=sem, vmem_limit_bytes=VMEM_LIMIT_BYTES)


def _rmsnorm_rows(x, g):
    return x * lax.rsqrt(jnp.mean(x * x, axis=-1, keepdims=True) + EPS) * g


def _dot(a, b):
    return jnp.dot(a, b, preferred_element_type=F32)


def _dot_nt(a, b):
    return lax.dot_general(a, b, (((1,), (1,)), ((), ())), preferred_element_type=F32)


def _in_proj_kernel(x_ref, g_ref, w_ref, z_ref, xn_ref):
    @pl.when(pl.program_id(1) == 0)
    def _():
        xn_ref[...] = _rmsnorm_rows(x_ref[...], g_ref[...]).astype(BF16)

    z_ref[...] = _dot(xn_ref[...], w_ref[...])


def _in_proj(x, g, w, tm=1024, tn=1536):
    n = x.shape[0]
    tm = min(tm, n)
    return pl.pallas_call(
        _in_proj_kernel,
        out_shape=jax.ShapeDtypeStruct((n, D_IN), F32),
        grid=(n // tm, D_IN // tn),
        in_specs=[
            pl.BlockSpec((tm, D_MODEL), lambda i, j: (i, 0)),
            pl.BlockSpec((1, D_MODEL), lambda i, j: (0, 0)),
            pl.BlockSpec((D_MODEL, tn), lambda i, j: (0, j)),
        ],
        out_specs=pl.BlockSpec((tm, tn), lambda i, j: (i, j)),
        scratch_shapes=[pltpu.VMEM((tm, D_MODEL), BF16)],
        compiler_params=_params("parallel", "arbitrary"),
        name="in_proj",
    )(x, g, w)


def _softplus(x):
    return jnp.maximum(x, 0.0) + jnp.log(1.0 + jnp.exp(-jnp.abs(x)))


def _neg_expm1(x):
    t = jnp.tanh(-0.5 * x)
    return 2.0 * t / (1.0 + t)


def _lru_coeffs(xc, wr, br, wi, bi, lam, first_is_pos0):
    xb = xc.astype(BF16)
    r = jax.nn.sigmoid(_dot(xb, wr) + br)
    i = jax.nn.sigmoid(_dot(xb, wi) + bi)
    log_a = -LRU_C * r * _softplus(-lam)
    a = jnp.exp(log_a)
    mult = jnp.sqrt(_neg_expm1(2.0 * log_a))
    if first_is_pos0 is not None:
        row = lax.broadcasted_iota(jnp.int32, xc.shape, 0)
        mult = jnp.where(jnp.logical_and(first_is_pos0, row == 0), 1.0, mult)
    return a, mult * i * xc


def _rnn_prompt_kernel(xr_ref, gr_ref, cw_ref, cb_ref, wr_ref, br_ref, wi_ref, bi_ref, lam_ref,
                       yr_ref, hlast_ref, tail_ref, h_ref, *, tb):
    t = pl.program_id(1)

    @pl.when(t == 0)
    def _():
        tail_ref[...] = jnp.zeros_like(tail_ref)
        h_ref[...] = jnp.zeros_like(h_ref)

    x = xr_ref[...]
    tail = tail_ref[...]
    row8 = lax.broadcasted_iota(jnp.int32, (SUBLANES, D_RNN), 0)
    cw = cw_ref[...]
    xc = cb_ref[...] + cw[CONV_W - 1:CONV_W, :] * x
    for k in range(1, CONV_W):
        xs = pltpu.roll(x, k, axis=0)
        head = jnp.where(row8 < k, pltpu.roll(tail, k, axis=0), xs[:SUBLANES])
        xs = jnp.concatenate([head, xs[SUBLANES:]], axis=0)
        xc = xc + cw[CONV_W - 1 - k:CONV_W - k, :] * xs
    tail_ref[...] = x[tb - SUBLANES:, :]

    a, b = _lru_coeffs(xc, wr_ref[...], br_ref[...], wi_ref[...], bi_ref[...], lam_ref[...], t == 0)

    ng = tb // SUBLANES
    a3 = a.reshape(ng, SUBLANES, D_RNN)
    b3 = b.reshape(ng, SUBLANES, D_RNN)
    sub = lax.broadcasted_iota(jnp.int32, (ng, SUBLANES, D_RNN), 1)
    d = 1
    while d < SUBLANES:
        a_sh = pltpu.roll(a3, d, axis=1)
        b_sh = pltpu.roll(b3, d, axis=1)
        keep = sub < d
        b3 = jnp.where(keep, b3, a3 * b_sh + b3)
        a3 = jnp.where(keep, a3, a3 * a_sh)
        d *= 2
    carry = h_ref[...]
    hs = []
    for g in range(ng):
        hs.append(b3[g] + a3[g] * carry)
        carry = hs[-1][SUBLANES - 1:SUBLANES, :]
    h_ref[...] = carry
    hlast_ref[0] = carry
    yr_ref[...] = (jnp.concatenate(hs, axis=0) * jax.nn.gelu(gr_ref[...])).astype(BF16)


def _rnn_prompt(z, batch, seq, cw, cb, wr, br, wi, bi, lam, tb=256):
    nt = seq // tb
    vec = pl.BlockSpec((1, D_RNN), lambda b, t: (0, 0))
    mat = pl.BlockSpec((D_RNN, D_RNN), lambda b, t: (0, 0))
    return pl.pallas_call(
        functools.partial(_rnn_prompt_kernel, tb=tb),
        out_shape=(jax.ShapeDtypeStruct((batch * seq, D_RNN), BF16),
                   jax.ShapeDtypeStruct((batch, 1, D_RNN), F32)),
        grid=(batch, nt),
        in_specs=[
            pl.BlockSpec((tb, D_RNN), lambda b, t: (b * nt + t, OFF_XR // D_RNN)),
            pl.BlockSpec((tb, D_RNN), lambda b, t: (b * nt + t, OFF_GR // D_RNN)),
            pl.BlockSpec((CONV_W, D_RNN), lambda b, t: (0, 0)),
            vec, mat, vec, mat, vec, vec,
        ],
        out_specs=(pl.BlockSpec((tb, D_RNN), lambda b, t: (b * nt + t, 0)),
                   pl.BlockSpec((1, 1, D_RNN), lambda b, t: (b, 0, 0))),
        scratch_shapes=[pltpu.VMEM((SUBLANES, D_RNN), F32), pltpu.VMEM((1, D_RNN), F32)],
        compiler_params=_params("parallel", "arbitrary"),
        name="rnn_prompt",
    )(z, z, cw, cb, wr, br, wi, bi, lam)


def _rnn_sample_kernel(xr_ref, gr_ref, buf_ref, h0_ref, cw_ref, cb_ref, wr_ref, br_ref, wi_ref,
                       bi_ref, lam_ref, yr_ref, hlast_ref, *, steps):
    cw = cw_ref[...]
    xp = [buf_ref[k] for k in range(CONV_W - 1)] + [xr_ref[s] for s in range(steps)]
    h = h0_ref[...]
    for s in range(steps):
        xc = cb_ref[...] + sum(cw[k:k + 1, :] * xp[s + k] for k in range(CONV_W))
        a, b = _lru_coeffs(xc, wr_ref[...], br_ref[...], wi_ref[...], bi_ref[...], lam_ref[...], None)
        h = a * h + b
        yr_ref[s] = (h * jax.nn.gelu(gr_ref[s])).astype(BF16)
    hlast_ref[...] = h


def _rnn_sample(xr_t, gr_t, buf_t, h0, cw, cb, wr, br, wi, bi, lam):
    steps, db, _ = xr_t.shape
    return pl.pallas_call(
        functools.partial(_rnn_sample_kernel, steps=steps),
        out_shape=(jax.ShapeDtypeStruct((steps, db, D_RNN), BF16),
                   jax.ShapeDtypeStruct((db, D_RNN), F32)),
        compiler_params=pltpu.CompilerParams(vmem_limit_bytes=VMEM_LIMIT_BYTES),
        name="rnn_sample",
    )(xr_t, gr_t, buf_t, h0, cw, cb, wr, br, wi, bi, lam)


def _headnorm_rope(x, gain, seg_ones, cos_t, sin_up_t, sin_dn_t, scale):
    w = x.shape[1]
    sq = x * x
    hi = sq.astype(BF16)
    lo = (sq - hi.astype(F32)).astype(BF16)
    ms = (_dot(hi, seg_ones) + _dot(lo, seg_ones)) * (1.0 / HEAD_DIM)
    xn = x * lax.rsqrt(ms + EPS)
    outs = []
    for c in range(w // LANES):
        xt = xn[:, c * LANES:(c + 1) * LANES] * gain
        up = pltpu.roll(xt, LANES - ROT_DIM // 2, axis=1)
        dn = pltpu.roll(xt, ROT_DIM // 2, axis=1)
        outs.append((xt * cos_t + up * sin_up_t + dn * sin_dn_t) * scale)
    return jnp.concatenate(outs, axis=1)


def _qk_prep_kernel(q_ref, k_ref, v_ref, cos_ref, sup_ref, sdn_ref, qg_ref, kg_ref, oq_ref, ok_ref,
                    qo_ref, kf_ref, kb_ref, vb_ref):
    cos_t, sup, sdn = cos_ref[...], sup_ref[...], sdn_ref[...]
    q = _headnorm_rope(q_ref[...], qg_ref[...], oq_ref[...], cos_t, sup, sdn, HEAD_DIM ** -0.5)
    q = jnp.concatenate([q[:, h * HEAD_DIM:(h + 1) * HEAD_DIM] for h in _head_perm()], axis=1)
    qo_ref[...] = q.astype(BF16)
    k = _headnorm_rope(k_ref[...], kg_ref[...], ok_ref[...], cos_t, sup, sdn, 1.0)
    kf_ref[...] = k
    kb_ref[...] = k.astype(BF16)
    vb_ref[...] = v_ref[...].astype(BF16)


def _qk_prep(z, tabs, qg, kg, ones_q, ones_k, tm=512):
    n = z.shape[0]
    tm = min(tm, n)
    cos_t, sup_t, sdn_t = tabs
    ntab = cos_t.shape[0] // tm
    tab = pl.BlockSpec((tm, LANES), lambda i: (i % ntab, 0))
    const = lambda shape: pl.BlockSpec(shape, lambda i: (0, 0))
    return pl.pallas_call(
        _qk_prep_kernel,
        out_shape=(jax.ShapeDtypeStruct((n, D_Q), BF16), jax.ShapeDtypeStruct((n, D_KV), F32),
                   jax.ShapeDtypeStruct((n, D_KV), BF16), jax.ShapeDtypeStruct((n, D_KV), BF16)),
        grid=(n // tm,),
        in_specs=[
            pl.BlockSpec((tm, D_Q), lambda i: (i, OFF_Q // D_Q)),
            pl.BlockSpec((tm, D_KV), lambda i: (i, OFF_K // D_KV)),
            pl.BlockSpec((tm, D_KV), lambda i: (i, OFF_V // D_KV)),
            tab, tab, tab,
            const((1, LANES)), const((1, LANES)), const((D_Q, D_Q)), const((D_KV, D_KV)),
        ],
        out_specs=(pl.BlockSpec((tm, D_Q), lambda i: (i, 0)), pl.BlockSpec((tm, D_KV), lambda i: (i, 0)),
                   pl.BlockSpec((tm, D_KV), lambda i: (i, 0)), pl.BlockSpec((tm, D_KV), lambda i: (i, 0))),
        compiler_params=_params("parallel"),
        name="qk_prep",
    )(z, z, z, cos_t, sup_t, sdn_t, qg, kg, ones_q, ones_k)


def _sink_attention(q, k, v, valid, sink_ref, tq):
    lane_head = lax.broadcasted_iota(jnp.int32, (tq, D_KV), 1) // HEAD_DIM
    rowg = lax.broadcasted_iota(jnp.int32, (GROUP * tq, 1), 0) // tq
    validg = jnp.concatenate([valid] * GROUP, axis=0)
    out = [jnp.zeros((tq, D_KV), F32) for _ in range(GROUP)]
    for j in range(N_KV_HEADS):
        sel = lane_head == j
        keep = jnp.where(sel, 1.0, 0.0).astype(BF16)
        qs = jnp.concatenate([q[:, g * D_KV:(g + 1) * D_KV] * keep for g in range(GROUP)], axis=0)
        s = jnp.where(validg, _dot_nt(qs, k), NEG_INF)
        sk = jnp.zeros((GROUP * tq, 1), F32)
        for g in range(GROUP):
            sk = jnp.where(rowg == g, sink_ref[j * GROUP + g], sk)
        m = jnp.maximum(jnp.max(s, axis=-1, keepdims=True), sk)
        p = jnp.exp(s - m)
        denom = jnp.sum(p, axis=-1, keepdims=True) + jnp.exp(sk - m)
        pv = _dot(p.astype(BF16), v) / denom
        for g in range(GROUP):
            out[g] = jnp.where(sel, pv[g * tq:(g + 1) * tq], out[g])
    return jnp.concatenate(out, axis=1)


def _attn_prompt_kernel(sink_ref, q_ref, kp_ref, kc_ref, vp_ref, vc_ref, o_ref):
    nb = pl.program_id(1)
    k = jnp.concatenate([kp_ref[...], kc_ref[...]], axis=0)
    v = jnp.concatenate([vp_ref[...], vc_ref[...]], axis=0)
    i = lax.broadcasted_iota(jnp.int32, (WINDOW, 2 * WINDOW), 0)
    j = lax.broadcasted_iota(jnp.int32, (WINDOW, 2 * WINDOW), 1)
    d = WINDOW + i - j
    valid = (d >= 0) & (d < WINDOW) & ((j >= WINDOW) | (nb > 0))
    o_ref[...] = _sink_attention(q_ref[...], k, v, valid, sink_ref, WINDOW).astype(BF16)


def _attn_prompt(sinks, q, kb, vb, batch, seq):
    nblk = seq // WINDOW
    cur = lambda w: pl.BlockSpec((WINDOW, w), lambda b, t: (b * nblk + t, 0))
    prev = lambda w: pl.BlockSpec((WINDOW, w), lambda b, t: (b * nblk + jnp.maximum(t - 1, 0), 0))
    return pl.pallas_call(
        _attn_prompt_kernel,
        out_shape=jax.ShapeDtypeStruct((batch * seq, D_Q), BF16),
        grid=(batch, nblk),
        in_specs=[pl.BlockSpec(memory_space=pltpu.SMEM), cur(D_Q), prev(D_KV), cur(D_KV),
                  prev(D_KV), cur(D_KV)],
        out_specs=cur(D_Q),
        compiler_params=_params("parallel", "arbitrary"),
        name="attn_prompt",
    )(sinks, q, kb, kb, vb, vb)


def _shift_in(cache, new8, steps):
    wbuf = cache.shape[0]
    rolled = pltpu.roll(cache, wbuf - steps, axis=0)
    row8 = lax.broadcasted_iota(jnp.int32, new8.shape, 0)
    tail = jnp.where(row8 >= SUBLANES - steps, pltpu.roll(new8, SUBLANES - steps, axis=0),
                     rolled[wbuf - SUBLANES:])
    return jnp.concatenate([rolled[:wbuf - SUBLANES], tail], axis=0)


def _attn_sample_kernel(sink_ref, q_ref, kn_ref, vn_ref, knf_ref, vnf_ref, ck_ref, cv_ref, o_ref, ok_ref, ov_ref,
                        *, bs, tq, wbuf, steps):
    t = lax.broadcasted_iota(jnp.int32, (tq, wbuf + tq), 0)
    c = lax.broadcasted_iota(jnp.int32, (tq, wbuf + tq), 1)
    d = wbuf + t - c
    valid = (d >= 0) & (d < WINDOW)
    for s in range(bs):
        ck, cv = ck_ref[s], cv_ref[s]
        k = jnp.concatenate([ck.astype(BF16), kn_ref[s]], axis=0)
        v = jnp.concatenate([cv.astype(BF16), vn_ref[s]], axis=0)
        o_ref[s] = _sink_attention(q_ref[s], k, v, valid, sink_ref, tq).astype(BF16)
        ok_ref[s] = _shift_in(ck, knf_ref[s], steps)
        ov_ref[s] = _shift_in(cv, vnf_ref[s], steps)


def _attn_sample(sinks, q3, kn3, vn3, knf3, vnf3, cache_k, cache_v, steps, bs=8):
    db, tq, _ = q3.shape
    wbuf = cache_k.shape[1]
    assert steps <= SUBLANES and knf3.shape[1] == SUBLANES and db % bs == 0
    blk = lambda r, w: pl.BlockSpec((bs, r, w), lambda b: (b, 0, 0))
    cache = jax.ShapeDtypeStruct((db, wbuf, D_KV), F32)
    return pl.pallas_call(
        functools.partial(_attn_sample_kernel, bs=bs, tq=tq, wbuf=wbuf, steps=steps),
        out_shape=(jax.ShapeDtypeStruct((db, tq, D_Q), BF16), cache, cache),
        grid=(db // bs,),
        in_specs=[pl.BlockSpec(memory_space=pltpu.SMEM), blk(tq, D_Q), blk(tq, D_KV), blk(tq, D_KV),
                  blk(SUBLANES, D_KV), blk(SUBLANES, D_KV), blk(wbuf, D_KV), blk(wbuf, D_KV)],
        out_specs=(blk(tq, D_Q), blk(wbuf, D_KV), blk(wbuf, D_KV)),
        compiler_params=_params("parallel"),
        name="attn_sample",
    )(sinks, q3, kn3, vn3, knf3, vnf3, cache_k, cache_v)


def _merge_kernel(yr_ref, o_ref, ga_ref, gb_ref, wr_ref, wa_ref, m_ref):
    a = _dot(yr_ref[...], wr_ref[...])
    b = _dot(o_ref[...], wa_ref[...])
    m_ref[...] = (jax.nn.sigmoid(ga_ref[...]) * a + jax.nn.sigmoid(gb_ref[...]) * b).astype(BF16)


def _merge(yr, o, z, wr, wa, tm=512, tn=D_MODEL):
    n = yr.shape[0]
    tm = min(tm, n)
    nj = D_MODEL // tn
    return pl.pallas_call(
        _merge_kernel,
        out_shape=jax.ShapeDtypeStruct((n, D_MODEL), BF16),
        grid=(n // tm, nj),
        in_specs=[
            pl.BlockSpec((tm, D_RNN), lambda i, j: (i, 0)),
            pl.BlockSpec((tm, D_Q), lambda i, j: (i, 0)),
            pl.BlockSpec((tm, tn), lambda i, j: (i, OFF_GA // tn + j)),
            pl.BlockSpec((tm, tn), lambda i, j: (i, OFF_GB // tn + j)),
            pl.BlockSpec((D_RNN, tn), lambda i, j: (0, j)),
            pl.BlockSpec((D_Q, tn), lambda i, j: (0, j)),
        ],
        out_specs=pl.BlockSpec((tm, tn), lambda i, j: (i, j)),
        compiler_params=_params("parallel", "arbitrary"),
        name="merge",
    )(yr, o, z, z, wr, wa)


def _out_proj_kernel(x_ref, m_ref, w_ref, g_ref, x1_ref, n2_ref):
    x1 = x_ref[...] + _dot(m_ref[...], w_ref[...])
    x1_ref[...] = x1
    n2_ref[...] = _rmsnorm_rows(x1, g_ref[...]).astype(BF16)


def _out_proj(x, m, w, g, tm=512):
    n = x.shape[0]
    tm = min(tm, n)
    row = lambda dt: pl.BlockSpec((tm, D_MODEL), lambda i: (i, 0))
    return pl.pallas_call(
        _out_proj_kernel,
        out_shape=(jax.ShapeDtypeStruct((n, D_MODEL), F32), jax.ShapeDtypeStruct((n, D_MODEL), BF16)),
        grid=(n // tm,),
        in_specs=[row(F32), row(BF16), pl.BlockSpec((D_MODEL, D_MODEL), lambda i: (0, 0)),
                  pl.BlockSpec((1, D_MODEL), lambda i: (0, 0))],
        out_specs=(row(F32), row(BF16)),
        compiler_params=_params("parallel"),
        name="out_proj",
    )(x, m, w, g)


def _topk_rows(s, k, rid=None):
    rows, t = s.shape
    if rid is None:
        rid = lax.broadcasted_iota(jnp.int32, (rows, t), 0).astype(F32)
    slot = lax.broadcasted_iota(jnp.int32, (k, t), 0)
    vals = jnp.zeros((k, t), F32)
    ids = jnp.zeros((k, t), F32)
    for r in range(k):
        m = jnp.max(s, axis=0, keepdims=True)
        i = jnp.min(jnp.where(s == m, rid, jnp.inf), axis=0, keepdims=True)
        vals = jnp.where(slot == r, m, vals)
        ids = jnp.where(slot == r, i, ids)
        s = jnp.where(rid == i, -jnp.inf, s)
    return vals, ids


def _route_tokens(n2, wq, sk_ref):
    q = _dot(n2, wq).astype(BF16)
    gates, experts = [], []
    for h in range(PEER_HEADS):
        top = []
        for p in range(2):
            c = (2 * h + p) * D_HALF_KEY
            s = _dot_nt(sk_ref[2 * h + p], q[:, c:c + D_HALF_KEY])
            top.append(_topk_rows(s, PEER_TOPK))
        (s1, i1), (s2, i2) = top
        hk = PEER_TOPK // 2
        tcol = s1.shape[1]
        brow = lambda n: lax.broadcasted_iota(jnp.int32, (n, tcol), 0).astype(F32)
        cand = [s1[0:1, :] + s2]
        cidx = [i1[0:1, :] * float(N_KEYS) + i2]
        rid = [brow(PEER_TOPK)]
        for a in range(1, hk):
            cand.append(s1[a:a + 1, :] + s2[0:hk, :])
            cidx.append(i1[a:a + 1, :] * float(N_KEYS) + i2[0:hk, :])
            rid.append(brow(hk) + float(a * PEER_TOPK))
        cand.append(s1[hk:, :] + s2[0:1, :])
        cidx.append(i1[hk:, :] * float(N_KEYS) + i2[0:1, :])
        rid.append((brow(hk) + float(hk)) * float(PEER_TOPK))
        cand, cidx, rid = (jnp.concatenate(x, axis=0) for x in (cand, cidx, rid))
        top_s, pos = _topk_rows(cand, PEER_TOPK, rid)
        slot = lax.broadcasted_iota(jnp.int32, top_s.shape, 0)
        ids = jnp.zeros(top_s.shape, F32)
        for r in range(PEER_TOPK):
            e = jnp.max(jnp.where(rid == pos[r:r + 1, :], cidx, -1.0), axis=0, keepdims=True)
            ids = jnp.where(slot == r, e, ids)
        w = jnp.exp(top_s - top_s[0:1, :])
        gates.append(w / jnp.sum(w, axis=0, keepdims=True))
        experts.append(ids)
    return jnp.concatenate(experts, axis=0).T.astype(jnp.int32), jnp.concatenate(gates, axis=0).T


def _peer_route_kernel(n2_ref, wq_ref, sk_ref, idx_ref, gate_ref):
    idx_ref[...], gate_ref[...] = _route_tokens(n2_ref[...], wq_ref[...], sk_ref)


def _peer_route(n2, wq, sk, tm=256):
    n = n2.shape[0]
    tm = min(tm, n)
    return pl.pallas_call(
        _peer_route_kernel,
        out_shape=(jax.ShapeDtypeStruct((n, N_PICKS), jnp.int32), jax.ShapeDtypeStruct((n, N_PICKS), F32)),
        grid=(n // tm,),
        in_specs=[pl.BlockSpec((tm, D_MODEL), lambda i: (i, 0)),
                  pl.BlockSpec((D_MODEL, PEER_HEADS * D_KEY), lambda i: (0, 0)),
                  pl.BlockSpec((2 * PEER_HEADS, N_KEYS, D_HALF_KEY), lambda i: (0, 0, 0))],
        out_specs=(pl.BlockSpec((tm, N_PICKS), lambda i: (i, 0)), pl.BlockSpec((tm, N_PICKS), lambda i: (i, 0))),
        compiler_params=_params("parallel"),
        name="peer_route",
    )(n2, wq, sk)


D_TILES = D_MODEL // LANES
assert D_TILES == BF16_TILE_ROWS
PICK_GROUP = 16
WAIT_GROUP = 4


def _peer_mix_kernel(idx_ref, idxn_ref, x_ref, gate_ref, exp_ref, sel_ref, tab_ref, o_ref, buf_a, buf_b,
                     sem_ref, *, tb, nsteps):
    i = pl.program_id(0)
    bufs = (buf_a, buf_b)

    group_rows = WAIT_GROUP * N_PICKS

    def start_token(ids_ref, row, half, t, picks=(0, N_PICKS)):
        for j in range(*picks):
            pltpu.make_async_copy(tab_ref.at[ids_ref[row, j]], bufs[half].at[t * N_PICKS + j],
                                  sem_ref.at[half, t // WAIT_GROUP]).start(priority=j % 2)

    def wait_group(half, g):
        pltpu.make_async_copy(tab_ref.at[pl.ds(0, group_rows)], bufs[half].at[pl.ds(g * group_rows, group_rows)],
                              sem_ref.at[half, g]).wait()

    @pl.when(i == 0)
    def _():
        for t in range(tb):
            start_token(idx_ref, t, 0, t)

    rows = N_PICKS * D_TILES
    diag = (lax.broadcasted_iota(jnp.int32, (D_TILES, rows), 1) % D_TILES
            == lax.broadcasted_iota(jnp.int32, (D_TILES, rows), 0))

    sel = sel_ref[...]
    lane = lax.broadcasted_iota(jnp.int32, (N_PICKS, LANES), 1)

    first_picks, second_picks = (0, N_PICKS // 2), (N_PICKS // 2, N_PICKS)

    def run_tile(half, start_other):
        buf = bufs[half]
        hcols = jnp.zeros((N_PICKS, LANES), F32)
        for t in range(tb):
            if t % WAIT_GROUP == 0:
                wait_group(half, t // WAIT_GROUP)
            start_other(t, first_picks)
            u3 = buf[t * N_PICKS:(t + 1) * N_PICKS, 0:D_TILES, :]
            p = (u3 * x_ref[half * tb + t][None, :, :]).reshape(rows, LANES)
            grp = PICK_GROUP * D_TILES
            r = jnp.concatenate([_dot(sel, p[c * grp:(c + 1) * grp]) for c in range(N_PICKS // PICK_GROUP)],
                                axis=0)
            hcols = jnp.where(lane == t, jnp.sum(r, axis=1, keepdims=True), hcols)
        h = hcols.T[0:tb, :]
        w = (jax.nn.gelu(h) * gate_ref[half * tb:(half + 1) * tb, :]).astype(BF16)
        wrow = _dot(w, exp_ref[...])
        for t in range(tb):
            start_other(t, second_picks)
            v = buf[t * N_PICKS:(t + 1) * N_PICKS, D_TILES:2 * D_TILES, :].reshape(rows, LANES)
            wm = jnp.where(diag, wrow[t:t + 1, :], 0.0).astype(BF16)
            o_ref[half * tb + t] = _dot(wm, v)

    run_tile(0, lambda t, picks: start_token(idx_ref, tb + t, 1, t, picks))
    run_tile(1, lambda t, picks: start_token(idxn_ref, t, 0, t, picks))

    @pl.when(i == nsteps - 1)
    def _():
        for g in range(tb // WAIT_GROUP):
            wait_group(0, g)


def _peer_mix(idx, gates, n2, table, expand, sel, tb=BF16_TILE_ROWS):
    n = idx.shape[0]
    nsteps = n // (2 * tb)
    x3 = n2.reshape(n, D_TILES, LANES)
    rows = N_PICKS * D_TILES
    out = pl.pallas_call(
        functools.partial(_peer_mix_kernel, tb=tb, nsteps=nsteps),
        out_shape=jax.ShapeDtypeStruct((n, D_TILES, LANES), F32),
        grid=(nsteps,),
        in_specs=[
            pl.BlockSpec((2 * tb, N_PICKS), lambda i: (i, 0), memory_space=pltpu.SMEM),
            pl.BlockSpec((tb, N_PICKS), lambda i: (jnp.minimum(2 * i + 2, 2 * nsteps - 2), 0),
                         memory_space=pltpu.SMEM),
            pl.BlockSpec((2 * tb, D_TILES, LANES), lambda i: (i, 0, 0)),
            pl.BlockSpec((2 * tb, N_PICKS), lambda i: (i, 0)),
            pl.BlockSpec((N_PICKS, rows), lambda i: (0, 0)),
            pl.BlockSpec((PICK_GROUP, PICK_GROUP * D_TILES), lambda i: (0, 0)),
            pl.BlockSpec(memory_space=pl.ANY),
        ],
        out_specs=pl.BlockSpec((2 * tb, D_TILES, LANES), lambda i: (i, 0, 0)),
        scratch_shapes=[pltpu.VMEM((tb * N_PICKS, 2 * D_TILES, LANES), BF16),
                        pltpu.VMEM((tb * N_PICKS, 2 * D_TILES, LANES), BF16),
                        pltpu.SemaphoreType.DMA((2, tb // WAIT_GROUP))],
        compiler_params=_params("arbitrary"),
        name="peer_mix",
    )(idx, idx, x3, gates, expand, sel, table)
    return out.reshape(n, D_MODEL)


def _pack_table_kernel(u_ref, v_ref, o_ref):
    o_ref[:, 0:D_TILES, :] = u_ref[...].astype(BF16)
    o_ref[:, D_TILES:2 * D_TILES, :] = v_ref[...].astype(BF16)


def _pack_table(u3, v3, te=256):
    e = u3.shape[0]
    te = min(te, e)
    blk = pl.BlockSpec((te, D_TILES, LANES), lambda i: (i, 0, 0))
    return pl.pallas_call(
        _pack_table_kernel,
        out_shape=jax.ShapeDtypeStruct((e, 2 * D_TILES, LANES), BF16),
        grid=(e // te,),
        in_specs=[blk, blk],
        out_specs=pl.BlockSpec((te, 2 * D_TILES, LANES), lambda i: (i, 0, 0)),
        compiler_params=_params("parallel"),
        name="pack_table",
    )(u3, v3)


def _ple_kernel(x1_ref, po_ref, ple_ref, g_ref, wp_ref, wg_ref, y_ref):
    x2 = x1_ref[...] + po_ref[...]
    n3 = _rmsnorm_rows(x2, g_ref[...]).astype(BF16)
    emb = _dot(ple_ref[...].astype(BF16), wp_ref[...])
    y_ref[...] = x2 + emb * jax.nn.sigmoid(_dot(n3, wg_ref[...]))


def _ple(x1, po, ple, g, wp, wg, tm=512):
    n = x1.shape[0]
    tm = min(tm, n)
    row = pl.BlockSpec((tm, D_MODEL), lambda i: (i, 0))
    return pl.pallas_call(
        _ple_kernel,
        out_shape=jax.ShapeDtypeStruct((n, D_MODEL), F32),
        grid=(n // tm,),
        in_specs=[row, row, pl.BlockSpec((tm, D_PLE), lambda i: (i, 0)),
                  pl.BlockSpec((1, D_MODEL), lambda i: (0, 0)),
                  pl.BlockSpec((D_PLE, D_MODEL), lambda i: (0, 0)),
                  pl.BlockSpec((D_MODEL, D_MODEL), lambda i: (0, 0))],
        out_specs=row,
        compiler_params=_params("parallel"),
        name="ple",
    )(x1, po, ple, g, wp, wg)


def _rope_tables(pos):
    half = ROT_DIM // 2
    inv = ROPE_THETA ** (-jnp.arange(0, ROT_DIM, 2, dtype=F32) / ROT_DIM)
    ang = pos.astype(F32)[:, None] * inv[None, :]
    cos, sin = jnp.cos(ang), jnp.sin(ang)
    n = pos.shape[0]
    pad = jnp.zeros((n, HEAD_DIM - ROT_DIM), F32)
    zh = jnp.zeros((n, half), F32)
    cos_h = jnp.concatenate([cos, cos, pad + 1.0], axis=1)
    up_h = jnp.concatenate([-sin, zh, pad], axis=1)
    dn_h = jnp.concatenate([zh, sin, pad], axis=1)
    rep = LANES // HEAD_DIM
    return tuple(jnp.tile(a, (1, rep)) for a in (cos_h, up_h, dn_h))


def _block_diag(w):
    eye = jnp.eye(RNN_BLOCKS, dtype=w.dtype)
    return jnp.einsum("ncd,nm->ncmd", w, eye).reshape(D_RNN, D_RNN)


def _head_perm():
    return np.array([j * GROUP + g for g in range(GROUP) for j in range(N_KV_HEADS)])


def _token_pipeline_tail(x, z, yr, o, w, ple):
    m = _merge(yr, o, z, w["proj_rnn"], w["proj_attn"])
    x1, n2 = _out_proj(x, m, w["out"], w["norm_ffn"])
    idx, gate = _peer_route(n2, w["peer_q"], w["sub_keys"])
    po = _peer_mix(idx, gate, n2, w["peer_table"], w["expand"], w["pick_sum"])
    return _ple(x1, po, ple, w["norm_ple"], w["ple"], w["ple_gate"])


def kernel(x_prompt, x_sample, p_prompt, p_sample, state_conv, state_rglru, cache_k, cache_v, norm_mix, w_in, conv_w, conv_b, w_rgate, b_rgate, w_igate, b_igate, lru_lambda, w_proj_rnn, q_norm, k_norm, attn_sinks, w_proj_attn, w_out, norm_ffn, w_peer_q, peer_sub_keys, peer_u, peer_v, w_ple, norm_ple, w_ple_gate):
    depth = w_in.shape[0]
    assert depth == 1
    l = 0
    B, S, _ = x_prompt.shape
    DB, DS, _ = x_sample.shape
    wbuf = cache_k.shape[2]

    hp = _head_perm()
    offs = np.cumsum([0, D_RNN, D_RNN, D_Q, D_KV, D_KV, D_MODEL, D_MODEL])
    xr_c, gr_c, q_c, k_c, v_c, ga_c, gb_c = [np.arange(offs[i], offs[i + 1]) for i in range(7)]
    cols = np.concatenate([ga_c, gb_c, xr_c, gr_c, q_c, k_c, v_c])
    row2 = lambda a: a[l].reshape(1, -1)
    w = {
        "proj_rnn": w_proj_rnn[l].astype(BF16),
        "proj_attn": w_proj_attn[l].reshape(N_Q_HEADS, HEAD_DIM, D_MODEL)[hp].reshape(D_Q, D_MODEL).astype(BF16),
        "out": w_out[l].astype(BF16),
        "norm_ffn": row2(norm_ffn),
        "peer_q": w_peer_q[l].astype(BF16),
        "sub_keys": peer_sub_keys[l].reshape(2 * PEER_HEADS, N_KEYS, D_HALF_KEY).astype(BF16),
        "peer_table": _pack_table(peer_u[l].reshape(N_EXPERTS, D_TILES, LANES),
                                  peer_v[l].reshape(N_EXPERTS, D_TILES, LANES)),
        "norm_ple": row2(norm_ple),
        "ple": w_ple[l].astype(BF16),
        "ple_gate": w_ple_gate[l].astype(BF16),
    }
    expand = np.repeat(np.eye(N_PICKS, dtype=np.float32), D_TILES, axis=1)
    w["expand"] = jnp.asarray(expand, BF16)
    w["pick_sum"] = jnp.asarray(expand[:PICK_GROUP, :PICK_GROUP * D_TILES], BF16)
    runs = np.split(cols, np.flatnonzero(np.diff(cols) != 1) + 1)
    w_in_b = jnp.concatenate([w_in[l][:, r[0]:r[-1] + 1] for r in runs], axis=1).astype(BF16)
    g_mix = row2(norm_mix)
    cw, cb = conv_w[l], row2(conv_b)
    wr, br = _block_diag(w_rgate[l]).astype(BF16), row2(b_rgate)
    wi, bi = _block_diag(w_igate[l]).astype(BF16), row2(b_igate)
    lam = row2(lru_lambda)
    rep = LANES // HEAD_DIM
    qg = jnp.tile(q_norm[l], rep).reshape(1, LANES)
    kg = jnp.tile(k_norm[l], rep).reshape(1, LANES)
    seg = lambda width: jnp.asarray(
        np.kron(np.eye(width // HEAD_DIM, dtype=np.float32), np.ones((HEAD_DIM, HEAD_DIM), np.float32)), BF16)
    ones_q, ones_k = seg(D_Q), seg(D_KV)
    sinks = attn_sinks[l]

    xp = x_prompt.reshape(B * S, D_MODEL)
    zp = _in_proj(xp, g_mix, w_in_b)
    yr_p, h_p = _rnn_prompt(zp, B, S, cw, cb, wr, br, wi, bi, lam)
    tabs_p = _rope_tables(jnp.arange(S, dtype=jnp.int32))
    q_p, kf_p, kb_p, vb_p = _qk_prep(zp, tabs_p, qg, kg, ones_q, ones_k)
    o_p = _attn_prompt(sinks, q_p, kb_p, vb_p, B, S)
    y_p = _token_pipeline_tail(xp, zp, yr_p, o_p, w, p_prompt[l].reshape(B * S, D_PLE))

    zp3 = zp.reshape(B, S, D_IN)
    keep = min(WINDOW, S)
    prompt_conv = zp3[:, S - (CONV_W - 1):, OFF_XR:OFF_XR + D_RNN]
    prompt_k = kf_p.reshape(B, S, D_KV)[:, S - keep:].reshape(B, keep, N_KV_HEADS, HEAD_DIM)
    prompt_v = zp3[:, S - keep:, OFF_V:OFF_V + D_KV].reshape(B, keep, N_KV_HEADS, HEAD_DIM)

    ns = DB * DS
    xs = x_sample.reshape(ns, D_MODEL)
    zs = _in_proj(xs, g_mix, w_in_b)
    zs3 = zs.reshape(DB, DS, D_IN)
    tmaj = lambda a: jnp.transpose(a, (1, 0, 2))
    yr_s_t, h_s = _rnn_sample(tmaj(zs3[:, :, OFF_XR:OFF_XR + D_RNN]), tmaj(zs3[:, :, OFF_GR:OFF_GR + D_RNN]),
                              tmaj(state_conv[l]), state_rglru[l], cw, cb, wr, br, wi, bi, lam)
    yr_s = tmaj(yr_s_t).reshape(ns, D_RNN)
    pos_s = PAST_LEN + jnp.arange(DS, dtype=jnp.int32)
    tabs_s = tuple(jnp.tile(a, (DB, 1)) for a in _rope_tables(pos_s))
    q_s, kf_s, kb_s, vb_s = _qk_prep(zs, tabs_s, qg, kg, ones_q, ones_k)
    tq = BF16_TILE_ROWS
    pad_t = lambda a, rows: jnp.pad(a.reshape(DB, DS, -1), ((0, 0), (0, rows - DS), (0, 0)))
    ck = cache_k[l].reshape(DB, wbuf, D_KV)
    cv = cache_v[l].reshape(DB, wbuf, D_KV)
    o_s, sample_k, sample_v = _attn_sample(
        sinks, pad_t(q_s, tq), pad_t(kb_s, tq), pad_t(vb_s, tq), pad_t(kf_s, SUBLANES),
        pad_t(zs3[:, :, OFF_V:OFF_V + D_KV], SUBLANES), ck, cv, DS)
    o_s = o_s[:, :DS].reshape(ns, D_Q)
    y_s = _token_pipeline_tail(xs, zs, yr_s, o_s, w, p_sample[l].reshape(ns, D_PLE))

    sample_conv = jnp.concatenate([state_conv[l], zs3[:, :, OFF_XR:OFF_XR + D_RNN]], axis=1)[:, DS:]
    sample_k = sample_k.reshape(DB, wbuf, N_KV_HEADS, HEAD_DIM)
    sample_v = sample_v.reshape(DB, wbuf, N_KV_HEADS, HEAD_DIM)

    return (y_p.reshape(B, S, D_MODEL), y_s.reshape(DB, DS, D_MODEL),
            prompt_conv[None], h_p.reshape(1, B, D_RNN), prompt_k[None], prompt_v[None],
            sample_conv[None], h_s[None], sample_k[None], sample_v[None])
```

```python
import functools

import jax
import jax.numpy as jnp
import numpy as np
from jax import lax
from jax.experimental import pallas as pl
from jax.experimental.pallas import tpu as pltpu

D_MODEL = 2048
D_RNN = D_MODEL // 2
RNN_BLOCKS = 8
RNN_BLOCK = D_RNN // RNN_BLOCKS
CONV_W = 4
LRU_C = 8.0
HEAD_DIM = 64
N_Q_HEADS = D_MODEL // 2 // HEAD_DIM
N_KV_HEADS = 4
GROUP = N_Q_HEADS // N_KV_HEADS
D_Q = N_Q_HEADS * HEAD_DIM
D_KV = N_KV_HEADS * HEAD_DIM
WINDOW = 128
ROT_DIM = HEAD_DIM // 4
ROPE_THETA = 500000.0
N_KEYS = 128
N_EXPERTS = N_KEYS * N_KEYS
PEER_HEADS = 8
PEER_TOPK = 16
D_KEY = 256
D_HALF_KEY = D_KEY // 2
N_PICKS = PEER_HEADS * PEER_TOPK
D_PLE = 256
EPS = 1e-6
NEG_INF = -1e30
PAST_LEN = 16384

LANES = 128
SUBLANES = 8
BF16_TILE_ROWS = 16
VMEM_LIMIT_BYTES = 56 * 1024 * 1024

OFF_GA = 0
OFF_GB = OFF_GA + D_MODEL
OFF_XR = OFF_GB + D_MODEL
OFF_GR = OFF_XR + D_RNN
OFF_Q = OFF_GR + D_RNN
OFF_K = OFF_Q + D_Q
OFF_V = OFF_K + D_KV
D_IN = OFF_V + D_KV

BF16 = jnp.bfloat16
F32 = jnp.float32


def _params(*sem):
    return pltpu.CompilerParams(dimension_semantics=sem, vmem_limit_bytes=VMEM_LIMIT_BYTES)


def _rmsnorm_rows(x, g):
    return x * lax.rsqrt(jnp.mean(x * x, axis=-1, keepdims=True) + EPS) * g


def _dot(a, b):
    return jnp.dot(a, b, preferred_element_type=F32)


def _dot_nt(a, b):
    return lax.dot_general(a, b, (((1,), (1,)), ((), ())), preferred_element_type=F32)


def _in_proj_kernel(x_ref, g_ref, w_ref, z_ref, xn_ref):
    @pl.when(pl.program_id(1) == 0)
    def _():
        xn_ref[...] = _rmsnorm_rows(x_ref[...], g_ref[...]).astype(BF16)

    z_ref[...] = _dot(xn_ref[...], w_ref[...])


def _in_proj(x, g, w, tm=1024, tn=1536):
    n = x.shape[0]
    tm = min(tm, n)
    return pl.pallas_call(
        _in_proj_kernel,
        out_shape=jax.ShapeDtypeStruct((n, D_IN), F32),
        grid=(n // tm, D_IN // tn),
        in_specs=[
            pl.BlockSpec((tm, D_MODEL), lambda i, j: (i, 0)),
            pl.BlockSpec((1, D_MODEL), lambda i, j: (0, 0)),
            pl.BlockSpec((D_MODEL, tn), lambda i, j: (0, j)),
        ],
        out_specs=pl.BlockSpec((tm, tn), lambda i, j: (i, j)),
        scratch_shapes=[pltpu.VMEM((tm, D_MODEL), BF16)],
        compiler_params=_params("parallel", "arbitrary"),
        name="in_proj",
    )(x, g, w)


def _softplus(x):
    return jnp.maximum(x, 0.0) + jnp.log(1.0 + jnp.exp(-jnp.abs(x)))


def _neg_expm1(x):
    t = jnp.tanh(-0.5 * x)
    return 2.0 * t / (1.0 + t)


def _lru_coeffs(xc, wr, br, wi, bi, lam, first_is_pos0):
    xb = xc.astype(BF16)
    r = jax.nn.sigmoid(_dot(xb, wr) + br)
    i = jax.nn.sigmoid(_dot(xb, wi) + bi)
    log_a = -LRU_C * r * _softplus(-lam)
    a = jnp.exp(log_a)
    mult = jnp.sqrt(_neg_expm1(2.0 * log_a))
    if first_is_pos0 is not None:
        row = lax.broadcasted_iota(jnp.int32, xc.shape, 0)
        mult = jnp.where(jnp.logical_and(first_is_pos0, row == 0), 1.0, mult)
    return a, mult * i * xc


def _rnn_prompt_kernel(xr_ref, gr_ref, cw_ref, cb_ref, wr_ref, br_ref, wi_ref, bi_ref, lam_ref,
                       yr_ref, hlast_ref, tail_ref, h_ref, *, tb):
    t = pl.program_id(1)

    @pl.when(t == 0)
    def _():
        tail_ref[...] = jnp.zeros_like(tail_ref)
        h_ref[...] = jnp.zeros_like(h_ref)

    x = xr_ref[...]
    tail = tail_ref[...]
    row8 = lax.broadcasted_iota(jnp.int32, (SUBLANES, D_RNN), 0)
    cw = cw_ref[...]
    xc = cb_ref[...] + cw[CONV_W - 1:CONV_W, :] * x
    for k in range(1, CONV_W):
        xs = pltpu.roll(x, k, axis=0)
        head = jnp.where(row8 < k, pltpu.roll(tail, k, axis=0), xs[:SUBLANES])
        xs = jnp.concatenate([head, xs[SUBLANES:]], axis=0)
        xc = xc + cw[CONV_W - 1 - k:CONV_W - k, :] * xs
    tail_ref[...] = x[tb - SUBLANES:, :]

    a, b = _lru_coeffs(xc, wr_ref[...], br_ref[...], wi_ref[...], bi_ref[...], lam_ref[...], t == 0)

    ng = tb // SUBLANES
    a3 = a.reshape(ng, SUBLANES, D_RNN)
    b3 = b.reshape(ng, SUBLANES, D_RNN)
    sub = lax.broadcasted_iota(jnp.int32, (ng, SUBLANES, D_RNN), 1)
    d = 1
    while d < SUBLANES:
        a_sh = pltpu.roll(a3, d, axis=1)
        b_sh = pltpu.roll(b3, d, axis=1)
        keep = sub < d
        b3 = jnp.where(keep, b3, a3 * b_sh + b3)
        a3 = jnp.where(keep, a3, a3 * a_sh)
        d *= 2
    carry = h_ref[...]
    hs = []
    for g in range(ng):
        hs.append(b3[g] + a3[g] * carry)
        carry = hs[-1][SUBLANES - 1:SUBLANES, :]
    h_ref[...] = carry
    hlast_ref[0] = carry
    yr_ref[...] = (jnp.concatenate(hs, axis=0) * jax.nn.gelu(gr_ref[...])).astype(BF16)


def _rnn_prompt(z, batch, seq, cw, cb, wr, br, wi, bi, lam, tb=256):
    nt = seq // tb
    vec = pl.BlockSpec((1, D_RNN), lambda b, t: (0, 0))
    mat = pl.BlockSpec((D_RNN, D_RNN), lambda b, t: (0, 0))
    return pl.pallas_call(
        functools.partial(_rnn_prompt_kernel, tb=tb),
        out_shape=(jax.ShapeDtypeStruct((batch * seq, D_RNN), BF16),
                   jax.ShapeDtypeStruct((batch, 1, D_RNN), F32)),
        grid=(batch, nt),
        in_specs=[
            pl.BlockSpec((tb, D_RNN), lambda b, t: (b * nt + t, OFF_XR // D_RNN)),
            pl.BlockSpec((tb, D_RNN), lambda b, t: (b * nt + t, OFF_GR // D_RNN)),
            pl.BlockSpec((CONV_W, D_RNN), lambda b, t: (0, 0)),
            vec, mat, vec, mat, vec, vec,
        ],
        out_specs=(pl.BlockSpec((tb, D_RNN), lambda b, t: (b * nt + t, 0)),
                   pl.BlockSpec((1, 1, D_RNN), lambda b, t: (b, 0, 0))),
        scratch_shapes=[pltpu.VMEM((SUBLANES, D_RNN), F32), pltpu.VMEM((1, D_RNN), F32)],
        compiler_params=_params("parallel", "arbitrary"),
        name="rnn_prompt",
    )(z, z, cw, cb, wr, br, wi, bi, lam)


def _rnn_sample_kernel(xr_ref, gr_ref, buf_ref, h0_ref, cw_ref, cb_ref, wr_ref, br_ref, wi_ref,
                       bi_ref, lam_ref, yr_ref, hlast_ref, *, steps):
    cw = cw_ref[...]
    xp = [buf_ref[k] for k in range(CONV_W - 1)] + [xr_ref[s] for s in range(steps)]
    h = h0_ref[...]
    for s in range(steps):
        xc = cb_ref[...] + sum(cw[k:k + 1, :] * xp[s + k] for k in range(CONV_W))
        a, b = _lru_coeffs(xc, wr_ref[...], br_ref[...], wi_ref[...], bi_ref[...], lam_ref[...], None)
        h = a * h + b
        yr_ref[s] = (h * jax.nn.gelu(gr_ref[s])).astype(BF16)
    hlast_ref[...] = h


def _rnn_sample(xr_t, gr_t, buf_t, h0, cw, cb, wr, br, wi, bi, lam):
    steps, db, _ = xr_t.shape
    return pl.pallas_call(
        functools.partial(_rnn_sample_kernel, steps=steps),
        out_shape=(jax.ShapeDtypeStruct((steps, db, D_RNN), BF16),
                   jax.ShapeDtypeStruct((db, D_RNN), F32)),
        compiler_params=pltpu.CompilerParams(vmem_limit_bytes=VMEM_LIMIT_BYTES),
        name="rnn_sample",
    )(xr_t, gr_t, buf_t, h0, cw, cb, wr, br, wi, bi, lam)


def _headnorm_rope(x, gain, seg_ones, cos_t, sin_up_t, sin_dn_t, scale):
    w = x.shape[1]
    sq = x * x
    hi = sq.astype(BF16)
    lo = (sq - hi.astype(F32)).astype(BF16)
    ms = (_dot(hi, seg_ones) + _dot(lo, seg_ones)) * (1.0 / HEAD_DIM)
    xn = x * lax.rsqrt(ms + EPS)
    outs = []
    for c in range(w // LANES):
        xt = xn[:, c * LANES:(c + 1) * LANES] * gain
        up = pltpu.roll(xt, LANES - ROT_DIM // 2, axis=1)
        dn = pltpu.roll(xt, ROT_DIM // 2, axis=1)
        outs.append((xt * cos_t + up * sin_up_t + dn * sin_dn_t) * scale)
    return jnp.concatenate(outs, axis=1)


def _qk_prep_kernel(q_ref, k_ref, v_ref, cos_ref, sup_ref, sdn_ref, qg_ref, kg_ref, oq_ref, ok_ref,
                    qo_ref, kf_ref, kb_ref, vb_ref):
    cos_t, sup, sdn = cos_ref[...], sup_ref[...], sdn_ref[...]
    q = _headnorm_rope(q_ref[...], qg_ref[...], oq_ref[...], cos_t, sup, sdn, HEAD_DIM ** -0.5)
    q = jnp.concatenate([q[:, h * HEAD_DIM:(h + 1) * HEAD_DIM] for h in _head_perm()], axis=1)
    qo_ref[...] = q.astype(BF16)
    k = _headnorm_rope(k_ref[...], kg_ref[...], ok_ref[...], cos_t, sup, sdn, 1.0)
    kf_ref[...] = k
    kb_ref[...] = k.astype(BF16)
    vb_ref[...] = v_ref[...].astype(BF16)


def _qk_prep(z, tabs, qg, kg, ones_q, ones_k, tm=512):
    n = z.shape[0]
    tm = min(tm, n)
    cos_t, sup_t, sdn_t = tabs
    ntab = cos_t.shape[0] // tm
    tab = pl.BlockSpec((tm, LANES), lambda i: (i % ntab, 0))
    const = lambda shape: pl.BlockSpec(shape, lambda i: (0, 0))
    return pl.pallas_call(
        _qk_prep_kernel,
        out_shape=(jax.ShapeDtypeStruct((n, D_Q), BF16), jax.ShapeDtypeStruct((n, D_KV), F32),
                   jax.ShapeDtypeStruct((n, D_KV), BF16), jax.ShapeDtypeStruct((n, D_KV), BF16)),
        grid=(n // tm,),
        in_specs=[
            pl.BlockSpec((tm, D_Q), lambda i: (i, OFF_Q // D_Q)),
            pl.BlockSpec((tm, D_KV), lambda i: (i, OFF_K // D_KV)),
            pl.BlockSpec((tm, D_KV), lambda i: (i, OFF_V // D_KV)),
            tab, tab, tab,
            const((1, LANES)), const((1, LANES)), const((D_Q, D_Q)), const((D_KV, D_KV)),
        ],
        out_specs=(pl.BlockSpec((tm, D_Q), lambda i: (i, 0)), pl.BlockSpec((tm, D_KV), lambda i: (i, 0)),
                   pl.BlockSpec((tm, D_KV), lambda i: (i, 0)), pl.BlockSpec((tm, D_KV), lambda i: (i, 0))),
        compiler_params=_params("parallel"),
        name="qk_prep",
    )(z, z, z, cos_t, sup_t, sdn_t, qg, kg, ones_q, ones_k)


def _sink_attention(q, k, v, valid, sink_ref, tq):
    lane_head = lax.broadcasted_iota(jnp.int32, (tq, D_KV), 1) // HEAD_DIM
    rowg = lax.broadcasted_iota(jnp.int32, (GROUP * tq, 1), 0) // tq
    validg = jnp.concatenate([valid] * GROUP, axis=0)
    out = [jnp.zeros((tq, D_KV), F32) for _ in range(GROUP)]
    for j in range(N_KV_HEADS):
        sel = lane_head == j
        keep = jnp.where(sel, 1.0, 0.0).astype(BF16)
        qs = jnp.concatenate([q[:, g * D_KV:(g + 1) * D_KV] * keep for g in range(GROUP)], axis=0)
        s = jnp.where(validg, _dot_nt(qs, k), NEG_INF)
        sk = jnp.zeros((GROUP * tq, 1), F32)
        for g in range(GROUP):
            sk = jnp.where(rowg == g, sink_ref[j * GROUP + g], sk)
        m = jnp.maximum(jnp.max(s, axis=-1, keepdims=True), sk)
        p = jnp.exp(s - m)
        denom = jnp.sum(p, axis=-1, keepdims=True) + jnp.exp(sk - m)
        pv = _dot(p.astype(BF16), v) / denom
        for g in range(GROUP):
            out[g] = jnp.where(sel, pv[g * tq:(g + 1) * tq], out[g])
    return jnp.concatenate(out, axis=1)


def _attn_prompt_kernel(sink_ref, q_ref, kp_ref, kc_ref, vp_ref, vc_ref, o_ref):
    nb = pl.program_id(1)
    k = jnp.concatenate([kp_ref[...], kc_ref[...]], axis=0)
    v = jnp.concatenate([vp_ref[...], vc_ref[...]], axis=0)
    i = lax.broadcasted_iota(jnp.int32, (WINDOW, 2 * WINDOW), 0)
    j = lax.broadcasted_iota(jnp.int32, (WINDOW, 2 * WINDOW), 1)
    d = WINDOW + i - j
    valid = (d >= 0) & (d < WINDOW) & ((j >= WINDOW) | (nb > 0))
    o_ref[...] = _sink_attention(q_ref[...], k, v, valid, sink_ref, WINDOW).astype(BF16)


def _attn_prompt(sinks, q, kb, vb, batch, seq):
    nblk = seq // WINDOW
    cur = lambda w: pl.BlockSpec((WINDOW, w), lambda b, t: (b * nblk + t, 0))
    prev = lambda w: pl.BlockSpec((WINDOW, w), lambda b, t: (b * nblk + jnp.maximum(t - 1, 0), 0))
    return pl.pallas_call(
        _attn_prompt_kernel,
        out_shape=jax.ShapeDtypeStruct((batch * seq, D_Q), BF16),
        grid=(batch, nblk),
        in_specs=[pl.BlockSpec(memory_space=pltpu.SMEM), cur(D_Q), prev(D_KV), cur(D_KV),
                  prev(D_KV), cur(D_KV)],
        out_specs=cur(D_Q),
        compiler_params=_params("parallel", "arbitrary"),
        name="attn_prompt",
    )(sinks, q, kb, kb, vb, vb)


def _shift_in(cache, new8, steps):
    wbuf = cache.shape[0]
    rolled = pltpu.roll(cache, wbuf - steps, axis=0)
    row8 = lax.broadcasted_iota(jnp.int32, new8.shape, 0)
    tail = jnp.where(row8 >= SUBLANES - steps, pltpu.roll(new8, SUBLANES - steps, axis=0),
                     rolled[wbuf - SUBLANES:])
    return jnp.concatenate([rolled[:wbuf - SUBLANES], tail], axis=0)


def _attn_sample_kernel(sink_ref, q_ref, kn_ref, vn_ref, knf_ref, vnf_ref, ck_ref, cv_ref, o_ref, ok_ref, ov_ref,
                        *, bs, tq, wbuf, steps):
    t = lax.broadcasted_iota(jnp.int32, (tq, wbuf + tq), 0)
    c = lax.broadcasted_iota(jnp.int32, (tq, wbuf + tq), 1)
    d = wbuf + t - c
    valid = (d >= 0) & (d < WINDOW)
    for s in range(bs):
        ck, cv = ck_ref[s], cv_ref[s]
        k = jnp.concatenate([ck.astype(BF16), kn_ref[s]], axis=0)
        v = jnp.concatenate([cv.astype(BF16), vn_ref[s]], axis=0)
        o_ref[s] = _sink_attention(q_ref[s], k, v, valid, sink_ref, tq).astype(BF16)
        ok_ref[s] = _shift_in(ck, knf_ref[s], steps)
        ov_ref[s] = _shift_in(cv, vnf_ref[s], steps)


def _attn_sample(sinks, q3, kn3, vn3, knf3, vnf3, cache_k, cache_v, steps, bs=8):
    db, tq, _ = q3.shape
    wbuf = cache_k.shape[1]
    assert steps <= SUBLANES and knf3.shape[1] == SUBLANES and db % bs == 0
    blk = lambda r, w: pl.BlockSpec((bs, r, w), lambda b: (b, 0, 0))
    cache = jax.ShapeDtypeStruct((db, wbuf, D_KV), F32)
    return pl.pallas_call(
        functools.partial(_attn_sample_kernel, bs=bs, tq=tq, wbuf=wbuf, steps=steps),
        out_shape=(jax.ShapeDtypeStruct((db, tq, D_Q), BF16), cache, cache),
        grid=(db // bs,),
        in_specs=[pl.BlockSpec(memory_space=pltpu.SMEM), blk(tq, D_Q), blk(tq, D_KV), blk(tq, D_KV),
                  blk(SUBLANES, D_KV), blk(SUBLANES, D_KV), blk(wbuf, D_KV), blk(wbuf, D_KV)],
        out_specs=(blk(tq, D_Q), blk(wbuf, D_KV), blk(wbuf, D_KV)),
        compiler_params=_params("parallel"),
        name="attn_sample",
    )(sinks, q3, kn3, vn3, knf3, vnf3, cache_k, cache_v)


def _merge_kernel(yr_ref, o_ref, ga_ref, gb_ref, wr_ref, wa_ref, m_ref):
    a = _dot(yr_ref[...], wr_ref[...])
    b = _dot(o_ref[...], wa_ref[...])
    m_ref[...] = (jax.nn.sigmoid(ga_ref[...]) * a + jax.nn.sigmoid(gb_ref[...]) * b).astype(BF16)


def _merge(yr, o, z, wr, wa, tm=512, tn=D_MODEL):
    n = yr.shape[0]
    tm = min(tm, n)
    nj = D_MODEL // tn
    return pl.pallas_call(
        _merge_kernel,
        out_shape=jax.ShapeDtypeStruct((n, D_MODEL), BF16),
        grid=(n // tm, nj),
        in_specs=[
            pl.BlockSpec((tm, D_RNN), lambda i, j: (i, 0)),
            pl.BlockSpec((tm, D_Q), lambda i, j: (i, 0)),
            pl.BlockSpec((tm, tn), lambda i, j: (i, OFF_GA // tn + j)),
            pl.BlockSpec((tm, tn), lambda i, j: (i, OFF_GB // tn + j)),
            pl.BlockSpec((D_RNN, tn), lambda i, j: (0, j)),
            pl.BlockSpec((D_Q, tn), lambda i, j: (0, j)),
        ],
        out_specs=pl.BlockSpec((tm, tn), lambda i, j: (i, j)),
        compiler_params=_params("parallel", "arbitrary"),
        name="merge",
    )(yr, o, z, z, wr, wa)


def _out_proj_kernel(x_ref, m_ref, w_ref, g_ref, x1_ref, n2_ref):
    x1 = x_ref[...] + _dot(m_ref[...], w_ref[...])
    x1_ref[...] = x1
    n2_ref[...] = _rmsnorm_rows(x1, g_ref[...]).astype(BF16)


def _out_proj(x, m, w, g, tm=512):
    n = x.shape[0]
    tm = min(tm, n)
    row = lambda dt: pl.BlockSpec((tm, D_MODEL), lambda i: (i, 0))
    return pl.pallas_call(
        _out_proj_kernel,
        out_shape=(jax.ShapeDtypeStruct((n, D_MODEL), F32), jax.ShapeDtypeStruct((n, D_MODEL), BF16)),
        grid=(n // tm,),
        in_specs=[row(F32), row(BF16), pl.BlockSpec((D_MODEL, D_MODEL), lambda i: (0, 0)),
                  pl.BlockSpec((1, D_MODEL), lambda i: (0, 0))],
        out_specs=(row(F32), row(BF16)),
        compiler_params=_params("parallel"),
        name="out_proj",
    )(x, m, w, g)


def _topk_slabs(slabs, keys, k):
    t = slabs[0].shape[1]
    slot = lax.broadcasted_iota(jnp.int32, (k, t), 0)
    vals = jnp.zeros((k, t), F32)
    ids = jnp.zeros((k, t), F32)
    for r in range(k):
        best, bkey = slabs[0], keys[0]
        for sl, ky in zip(slabs[1:], keys[1:]):
            take = sl > best
            best = jnp.where(take, sl, best)
            bkey = jnp.where(take, ky, bkey)
        m = jnp.max(best, axis=0, keepdims=True)
        i = jnp.min(jnp.where(best == m, bkey, jnp.inf), axis=0, keepdims=True)
        vals = jnp.where(slot == r, m, vals)
        ids = jnp.where(slot == r, i, ids)
        slabs = [jnp.where(ky == i, -jnp.inf, sl) for sl, ky in zip(slabs, keys)]
    return vals, ids


def _peer_route_kernel(n2_ref, wq_ref, sk_ref, idx_ref, gate_ref):
    q = _dot(n2_ref[...], wq_ref[...]).astype(BF16)
    gates, experts = [], []
    tcol = q.shape[0]
    row8 = lax.broadcasted_iota(jnp.int32, (SUBLANES, tcol), 0).astype(F32)
    key_slabs = [row8 + float(a) for a in range(0, N_KEYS, SUBLANES)]
    assert PEER_TOPK == 2 * SUBLANES
    for h in range(PEER_HEADS):
        top = []
        for p in range(2):
            c = (2 * h + p) * D_HALF_KEY
            s = _dot_nt(sk_ref[2 * h + p], q[:, c:c + D_HALF_KEY])
            top.append(_topk_slabs([s[a:a + SUBLANES] for a in range(0, N_KEYS, SUBLANES)], key_slabs, PEER_TOPK))
        (s1, i1), (s2, i2) = top
        hk = SUBLANES
        lo, hi = slice(0, hk), slice(hk, 2 * hk)
        one = lambda a: slice(a, a + 1)
        groups = [(one(0), lo, row8), (one(0), hi, row8 + float(hk))]
        groups += [(one(a), lo, row8 + float(a * PEER_TOPK)) for a in range(1, hk)]
        groups += [(hi, one(0), (row8 + float(hk)) * float(PEER_TOPK))]
        cand = [s1[ra] + s2[rb] for ra, rb, _ in groups]
        cidx = [i1[ra] * float(N_KEYS) + i2[rb] for ra, rb, _ in groups]
        rid = [key for _, _, key in groups]
        top_s, pos = _topk_slabs(cand, rid, PEER_TOPK)
        slot = lax.broadcasted_iota(jnp.int32, top_s.shape, 0)
        ids = jnp.zeros(top_s.shape, F32)
        for r in range(PEER_TOPK):
            hit = [jnp.where(ky == pos[r:r + 1, :], ci, -1.0) for ky, ci in zip(rid, cidx)]
            e = jnp.max(functools.reduce(jnp.maximum, hit), axis=0, keepdims=True)
            ids = jnp.where(slot == r, e, ids)
        w = jnp.exp(top_s - top_s[0:1, :])
        gates.append(w / jnp.sum(w, axis=0, keepdims=True))
        experts.append(ids)
    gate_ref[...] = jnp.concatenate(gates, axis=0).T
    idx_ref[...] = jnp.concatenate(experts, axis=0).T.astype(jnp.int32)


def _peer_route(n2, wq, sk, tm=256):
    n = n2.shape[0]
    tm = min(tm, n)
    return pl.pallas_call(
        _peer_route_kernel,
        out_shape=(jax.ShapeDtypeStruct((n, N_PICKS), jnp.int32), jax.ShapeDtypeStruct((n, N_PICKS), F32)),
        grid=(n // tm,),
        in_specs=[pl.BlockSpec((tm, D_MODEL), lambda i: (i, 0)),
                  pl.BlockSpec((D_MODEL, PEER_HEADS * D_KEY), lambda i: (0, 0)),
                  pl.BlockSpec((2 * PEER_HEADS, N_KEYS, D_HALF_KEY), lambda i: (0, 0, 0))],
        out_specs=(pl.BlockSpec((tm, N_PICKS), lambda i: (i, 0)), pl.BlockSpec((tm, N_PICKS), lambda i: (i, 0))),
        compiler_params=_params("parallel"),
        name="peer_route",
    )(n2, wq, sk)


D_TILES = D_MODEL // LANES
assert D_TILES == BF16_TILE_ROWS
PICK_GROUP = 16
WAIT_GROUP = 4


def _peer_mix_kernel(idx_ref, idxn_ref, x_ref, gate_ref, exp_ref, sel_ref, tab_ref, o_ref, buf_a, buf_b,
                     sem_ref, *, tb, nsteps):
    i = pl.program_id(0)
    bufs = (buf_a, buf_b)

    group_rows = WAIT_GROUP * N_PICKS

    def start_token(ids_ref, row, half, t, picks=(0, N_PICKS)):
        for j in range(*picks):
            pltpu.make_async_copy(tab_ref.at[ids_ref[row, j]], bufs[half].at[t * N_PICKS + j],
                                  sem_ref.at[half, t // WAIT_GROUP]).start(priority=j % 2)

    def wait_group(half, g):
        pltpu.make_async_copy(tab_ref.at[pl.ds(0, group_rows)], bufs[half].at[pl.ds(g * group_rows, group_rows)],
                              sem_ref.at[half, g]).wait()

    @pl.when(i == 0)
    def _():
        for t in range(tb):
            start_token(idx_ref, t, 0, t)

    rows = N_PICKS * D_TILES
    diag = (lax.broadcasted_iota(jnp.int32, (D_TILES, rows), 1) % D_TILES
            == lax.broadcasted_iota(jnp.int32, (D_TILES, rows), 0))

    sel = sel_ref[...]
    lane = lax.broadcasted_iota(jnp.int32, (N_PICKS, LANES), 1)

    first_picks, second_picks = (0, N_PICKS // 2), (N_PICKS // 2, N_PICKS)

    def run_tile(half, start_other):
        buf = bufs[half]
        hcols = jnp.zeros((N_PICKS, LANES), F32)
        for t in range(tb):
            if t % WAIT_GROUP == 0:
                wait_group(half, t // WAIT_GROUP)
            start_other(t, first_picks)
            u3 = buf[t * N_PICKS:(t + 1) * N_PICKS, 0:D_TILES, :]
            p = (u3 * x_ref[half * tb + t][None, :, :]).reshape(rows, LANES)
            grp = PICK_GROUP * D_TILES
            r = jnp.concatenate([_dot(sel, p[c * grp:(c + 1) * grp]) for c in range(N_PICKS // PICK_GROUP)],
                                axis=0)
            hcols = jnp.where(lane == t, jnp.sum(r, axis=1, keepdims=True), hcols)
        h = hcols.T[0:tb, :]
        w = (jax.nn.gelu(h) * gate_ref[half * tb:(half + 1) * tb, :]).astype(BF16)
        wrow = _dot(w, exp_ref[...])
        for t in range(tb):
            start_other(t, second_picks)
            v = buf[t * N_PICKS:(t + 1) * N_PICKS, D_TILES:2 * D_TILES, :].reshape(rows, LANES)
            wm = jnp.where(diag, wrow[t:t + 1, :], 0.0).astype(BF16)
            o_ref[half * tb + t] = _dot(wm, v)

    run_tile(0, lambda t, picks: start_token(idx_ref, tb + t, 1, t, picks))
    run_tile(1, lambda t, picks: start_token(idxn_ref, t, 0, t, picks))

    @pl.when(i == nsteps - 1)
    def _():
        for g in range(tb // WAIT_GROUP):
            wait_group(0, g)


def _peer_mix(idx, gates, n2, table, expand, sel, tb=BF16_TILE_ROWS):
    n = idx.shape[0]
    nsteps = n // (2 * tb)
    x3 = n2.reshape(n, D_TILES, LANES)
    rows = N_PICKS * D_TILES
    out = pl.pallas_call(
        functools.partial(_peer_mix_kernel, tb=tb, nsteps=nsteps),
        out_shape=jax.ShapeDtypeStruct((n, D_TILES, LANES), F32),
        grid=(nsteps,),
        in_specs=[
            pl.BlockSpec((2 * tb, N_PICKS), lambda i: (i, 0), memory_space=pltpu.SMEM),
            pl.BlockSpec((tb, N_PICKS), lambda i: (jnp.minimum(2 * i + 2, 2 * nsteps - 2), 0),
                         memory_space=pltpu.SMEM),
            pl.BlockSpec((2 * tb, D_TILES, LANES), lambda i: (i, 0, 0)),
            pl.BlockSpec((2 * tb, N_PICKS), lambda i: (i, 0)),
            pl.BlockSpec((N_PICKS, rows), lambda i: (0, 0)),
            pl.BlockSpec((PICK_GROUP, PICK_GROUP * D_TILES), lambda i: (0, 0)),
            pl.BlockSpec(memory_space=pl.ANY),
        ],
        out_specs=pl.BlockSpec((2 * tb, D_TILES, LANES), lambda i: (i, 0, 0)),
        scratch_shapes=[pltpu.VMEM((tb * N_PICKS, 2 * D_TILES, LANES), BF16),
                        pltpu.VMEM((tb * N_PICKS, 2 * D_TILES, LANES), BF16),
                        pltpu.SemaphoreType.DMA((2, tb // WAIT_GROUP))],
        compiler_params=_params("arbitrary"),
        name="peer_mix",
    )(idx, idx, x3, gates, expand, sel, table)
    return out.reshape(n, D_MODEL)


def _pack_table_kernel(u_ref, v_ref, o_ref):
    o_ref[:, 0:D_TILES, :] = u_ref[...].astype(BF16)
    o_ref[:, D_TILES:2 * D_TILES, :] = v_ref[...].astype(BF16)


def _pack_table(u3, v3, te=256):
    e = u3.shape[0]
    te = min(te, e)
    blk = pl.BlockSpec((te, D_TILES, LANES), lambda i: (i, 0, 0))
    return pl.pallas_call(
        _pack_table_kernel,
        out_shape=jax.ShapeDtypeStruct((e, 2 * D_TILES, LANES), BF16),
        grid=(e // te,),
        in_specs=[blk, blk],
        out_specs=pl.BlockSpec((te, 2 * D_TILES, LANES), lambda i: (i, 0, 0)),
        compiler_params=_params("parallel"),
        name="pack_table",
    )(u3, v3)


def _ple_kernel(x1_ref, po_ref, ple_ref, g_ref, wp_ref, wg_ref, y_ref):
    x2 = x1_ref[...] + po_ref[...]
    n3 = _rmsnorm_rows(x2, g_ref[...]).astype(BF16)
    emb = _dot(ple_ref[...].astype(BF16), wp_ref[...])
    y_ref[...] = x2 + emb * jax.nn.sigmoid(_dot(n3, wg_ref[...]))


def _ple(x1, po, ple, g, wp, wg, tm=512):
    n = x1.shape[0]
    tm = min(tm, n)
    row = pl.BlockSpec((tm, D_MODEL), lambda i: (i, 0))
    return pl.pallas_call(
        _ple_kernel,
        out_shape=jax.ShapeDtypeStruct((n, D_MODEL), F32),
        grid=(n // tm,),
        in_specs=[row, row, pl.BlockSpec((tm, D_PLE), lambda i: (i, 0)),
                  pl.BlockSpec((1, D_MODEL), lambda i: (0, 0)),
                  pl.BlockSpec((D_PLE, D_MODEL), lambda i: (0, 0)),
                  pl.BlockSpec((D_MODEL, D_MODEL), lambda i: (0, 0))],
        out_specs=row,
        compiler_params=_params("parallel"),
        name="ple",
    )(x1, po, ple, g, wp, wg)


def _rope_tables(pos):
    half = ROT_DIM // 2
    inv = ROPE_THETA ** (-jnp.arange(0, ROT_DIM, 2, dtype=F32) / ROT_DIM)
    ang = pos.astype(F32)[:, None] * inv[None, :]
    cos, sin = jnp.cos(ang), jnp.sin(ang)
    n = pos.shape[0]
    pad = jnp.zeros((n, HEAD_DIM - ROT_DIM), F32)
    zh = jnp.zeros((n, half), F32)
    cos_h = jnp.concatenate([cos, cos, pad + 1.0], axis=1)
    up_h = jnp.concatenate([-sin, zh, pad], axis=1)
    dn_h = jnp.concatenate([zh, sin, pad], axis=1)
    rep = LANES // HEAD_DIM
    return tuple(jnp.tile(a, (1, rep)) for a in (cos_h, up_h, dn_h))


def _block_diag(w):
    eye = jnp.eye(RNN_BLOCKS, dtype=w.dtype)
    return jnp.einsum("ncd,nm->ncmd", w, eye).reshape(D_RNN, D_RNN)


def _head_perm():
    return np.array([j * GROUP + g for g in range(GROUP) for j in range(N_KV_HEADS)])


def _token_pipeline_tail(x, z, yr, o, w, ple):
    m = _merge(yr, o, z, w["proj_rnn"], w["proj_attn"])
    x1, n2 = _out_proj(x, m, w["out"], w["norm_ffn"])
    idx, gate = _peer_route(n2, w["peer_q"], w["sub_keys"])
    po = _peer_mix(idx, gate, n2, w["peer_table"], w["expand"], w["pick_sum"])
    return _ple(x1, po, ple, w["norm_ple"], w["ple"], w["ple_gate"])


def kernel(x_prompt, x_sample, p_prompt, p_sample, state_conv, state_rglru, cache_k, cache_v, norm_mix, w_in, conv_w, conv_b, w_rgate, b_rgate, w_igate, b_igate, lru_lambda, w_proj_rnn, q_norm, k_norm, attn_sinks, w_proj_attn, w_out, norm_ffn, w_peer_q, peer_sub_keys, peer_u, peer_v, w_ple, norm_ple, w_ple_gate):
    depth = w_in.shape[0]
    assert depth == 1
    l = 0
    B, S, _ = x_prompt.shape
    DB, DS, _ = x_sample.shape
    wbuf = cache_k.shape[2]

    hp = _head_perm()
    offs = np.cumsum([0, D_RNN, D_RNN, D_Q, D_KV, D_KV, D_MODEL, D_MODEL])
    xr_c, gr_c, q_c, k_c, v_c, ga_c, gb_c = [np.arange(offs[i], offs[i + 1]) for i in range(7)]
    cols = np.concatenate([ga_c, gb_c, xr_c, gr_c, q_c, k_c, v_c])
    row2 = lambda a: a[l].reshape(1, -1)
    w = {
        "proj_rnn": w_proj_rnn[l].astype(BF16),
        "proj_attn": w_proj_attn[l].reshape(N_Q_HEADS, HEAD_DIM, D_MODEL)[hp].reshape(D_Q, D_MODEL).astype(BF16),
        "out": w_out[l].astype(BF16),
        "norm_ffn": row2(norm_ffn),
        "peer_q": w_peer_q[l].astype(BF16),
        "sub_keys": peer_sub_keys[l].reshape(2 * PEER_HEADS, N_KEYS, D_HALF_KEY).astype(BF16),
        "peer_table": _pack_table(peer_u[l].reshape(N_EXPERTS, D_TILES, LANES),
                                  peer_v[l].reshape(N_EXPERTS, D_TILES, LANES)),
        "norm_ple": row2(norm_ple),
        "ple": w_ple[l].astype(BF16),
        "ple_gate": w_ple_gate[l].astype(BF16),
    }
    expand = np.repeat(np.eye(N_PICKS, dtype=np.float32), D_TILES, axis=1)
    w["expand"] = jnp.asarray(expand, BF16)
    w["pick_sum"] = jnp.asarray(expand[:PICK_GROUP, :PICK_GROUP * D_TILES], BF16)
    runs = np.split(cols, np.flatnonzero(np.diff(cols) != 1) + 1)
    w_in_b = jnp.concatenate([w_in[l][:, r[0]:r[-1] + 1] for r in runs], axis=1).astype(BF16)
    g_mix = row2(norm_mix)
    cw, cb = conv_w[l], row2(conv_b)
    wr, br = _block_diag(w_rgate[l]).astype(BF16), row2(b_rgate)
    wi, bi = _block_diag(w_igate[l]).astype(BF16), row2(b_igate)
    lam = row2(lru_lambda)
    rep = LANES // HEAD_DIM
    qg = jnp.tile(q_norm[l], rep).reshape(1, LANES)
    kg = jnp.tile(k_norm[l], rep).reshape(1, LANES)
    seg = lambda width: jnp.asarray(
        np.kron(np.eye(width // HEAD_DIM, dtype=np.float32), np.ones((HEAD_DIM, HEAD_DIM), np.float32)), BF16)
    ones_q, ones_k = seg(D_Q), seg(D_KV)
    sinks = attn_sinks[l]

    xp = x_prompt.reshape(B * S, D_MODEL)
    zp = _in_proj(xp, g_mix, w_in_b)
    yr_p, h_p = _rnn_prompt(zp, B, S, cw, cb, wr, br, wi, bi, lam)
    tabs_p = _rope_tables(jnp.arange(S, dtype=jnp.int32))
    q_p, kf_p, kb_p, vb_p = _qk_prep(zp, tabs_p, qg, kg, ones_q, ones_k)
    o_p = _attn_prompt(sinks, q_p, kb_p, vb_p, B, S)
    y_p = _token_pipeline_tail(xp, zp, yr_p, o_p, w, p_prompt[l].reshape(B * S, D_PLE))

    zp3 = zp.reshape(B, S, D_IN)
    keep = min(WINDOW, S)
    prompt_conv = zp3[:, S - (CONV_W - 1):, OFF_XR:OFF_XR + D_RNN]
    prompt_k = kf_p.reshape(B, S, D_KV)[:, S - keep:].reshape(B, keep, N_KV_HEADS, HEAD_DIM)
    prompt_v = zp3[:, S - keep:, OFF_V:OFF_V + D_KV].reshape(B, keep, N_KV_HEADS, HEAD_DIM)

    ns = DB * DS
    xs = x_sample.reshape(ns, D_MODEL)
    zs = _in_proj(xs, g_mix, w_in_b)
    zs3 = zs.reshape(DB, DS, D_IN)
    tmaj = lambda a: jnp.transpose(a, (1, 0, 2))
    yr_s_t, h_s = _rnn_sample(tmaj(zs3[:, :, OFF_XR:OFF_XR + D_RNN]), tmaj(zs3[:, :, OFF_GR:OFF_GR + D_RNN]),
                              tmaj(state_conv[l]), state_rglru[l], cw, cb, wr, br, wi, bi, lam)
    yr_s = tmaj(yr_s_t).reshape(ns, D_RNN)
    pos_s = PAST_LEN + jnp.arange(DS, dtype=jnp.int32)
    tabs_s = tuple(jnp.tile(a, (DB, 1)) for a in _rope_tables(pos_s))
    q_s, kf_s, kb_s, vb_s = _qk_prep(zs, tabs_s, qg, kg, ones_q, ones_k)
    tq = BF16_TILE_ROWS
    pad_t = lambda a, rows: jnp.pad(a.reshape(DB, DS, -1), ((0, 0), (0, rows - DS), (0, 0)))
    ck = cache_k[l].reshape(DB, wbuf, D_KV)
    cv = cache_v[l].reshape(DB, wbuf, D_KV)
    o_s, sample_k, sample_v = _attn_sample(
        sinks, pad_t(q_s, tq), pad_t(kb_s, tq), pad_t(vb_s, tq), pad_t(kf_s, SUBLANES),
        pad_t(zs3[:, :, OFF_V:OFF_V + D_KV], SUBLANES), ck, cv, DS)
    o_s = o_s[:, :DS].reshape(ns, D_Q)
    y_s = _token_pipeline_tail(xs, zs, yr_s, o_s, w, p_sample[l].reshape(ns, D_PLE))

    sample_conv = jnp.concatenate([state_conv[l], zs3[:, :, OFF_XR:OFF_XR + D_RNN]], axis=1)[:, DS:]
    sample_k = sample_k.reshape(DB, wbuf, N_KV_HEADS, HEAD_DIM)
    sample_v = sample_v.reshape(DB, wbuf, N_KV_HEADS, HEAD_DIM)

    return (y_p.reshape(B, S, D_MODEL), y_s.reshape(DB, DS, D_MODEL),
            prompt_conv[None], h_p.reshape(1, B, D_RNN), prompt_k[None], prompt_v[None],
            sample_conv[None], h_s[None], sample_k[None], sample_v[None])
```

```python
import functools

import jax
import jax.numpy as jnp
import numpy as np
from jax import lax
from jax.experimental import pallas as pl
from jax.experimental.pallas import tpu as pltpu

D_MODEL = 2048
D_RNN = D_MODEL // 2
RNN_BLOCKS = 8
RNN_BLOCK = D_RNN // RNN_BLOCKS
CONV_W = 4
LRU_C = 8.0
HEAD_DIM = 64
N_Q_HEADS = D_MODEL // 2 // HEAD_DIM
N_KV_HEADS = 4
GROUP = N_Q_HEADS // N_KV_HEADS
D_Q = N_Q_HEADS * HEAD_DIM
D_KV = N_KV_HEADS * HEAD_DIM
WINDOW = 128
ROT_DIM = HEAD_DIM // 4
ROPE_THETA = 500000.0
N_KEYS = 128
N_EXPERTS = N_KEYS * N_KEYS
PEER_HEADS = 8
PEER_TOPK = 16
D_KEY = 256
D_HALF_KEY = D_KEY // 2
N_PICKS = PEER_HEADS * PEER_TOPK
D_PLE = 256
EPS = 1e-6
NEG_INF = -1e30
PAST_LEN = 16384

LANES = 128
SUBLANES = 8
BF16_TILE_ROWS = 16
VMEM_LIMIT_BYTES = 56 * 1024 * 1024

OFF_GA = 0
OFF_GB = OFF_GA + D_MODEL
OFF_XR = OFF_GB + D_MODEL
OFF_GR = OFF_XR + D_RNN
OFF_Q = OFF_GR + D_RNN
OFF_K = OFF_Q + D_Q
OFF_V = OFF_K + D_KV
D_IN = OFF_V + D_KV

BF16 = jnp.bfloat16
F32 = jnp.float32


def _params(*sem):
    return pltpu.CompilerParams(dimension_semantics=sem, vmem_limit_bytes=VMEM_LIMIT_BYTES)


def _rmsnorm_rows(x, g):
    return x * lax.rsqrt(jnp.mean(x * x, axis=-1, keepdims=True) + EPS) * g


def _dot(a, b):
    return jnp.dot(a, b, preferred_element_type=F32)


def _dot_nt(a, b):
    return lax.dot_general(a, b, (((1,), (1,)), ((), ())), preferred_element_type=F32)


def _in_proj_kernel(x_ref, g_ref, w_ref, z_ref, xn_ref):
    @pl.when(pl.program_id(1) == 0)
    def _():
        xn_ref[...] = _rmsnorm_rows(x_ref[...], g_ref[...]).astype(BF16)

    z_ref[...] = _dot(xn_ref[...], w_ref[...])


def _in_proj(x, g, w, tm=1024, tn=1536):
    n = x.shape[0]
    tm = min(tm, n)
    return pl.pallas_call(
        _in_proj_kernel,
        out_shape=jax.ShapeDtypeStruct((n, D_IN), F32),
        grid=(n // tm, D_IN // tn),
        in_specs=[
            pl.BlockSpec((tm, D_MODEL), lambda i, j: (i, 0)),
            pl.BlockSpec((1, D_MODEL), lambda i, j: (0, 0)),
            pl.BlockSpec((D_MODEL, tn), lambda i, j: (0, j)),
        ],
        out_specs=pl.BlockSpec((tm, tn), lambda i, j: (i, j)),
        scratch_shapes=[pltpu.VMEM((tm, D_MODEL), BF16)],
        compiler_params=_params("parallel", "arbitrary"),
        name="in_proj",
    )(x, g, w)


def _softplus(x):
    return jnp.maximum(x, 0.0) + jnp.log(1.0 + jnp.exp(-jnp.abs(x)))


def _neg_expm1(x):
    t = jnp.tanh(-0.5 * x)
    return 2.0 * t / (1.0 + t)


def _lru_coeffs(xc, wr, br, wi, bi, lam, first_is_pos0):
    xb = xc.astype(BF16)
    r = jax.nn.sigmoid(_dot(xb, wr) + br)
    i = jax.nn.sigmoid(_dot(xb, wi) + bi)
    log_a = -LRU_C * r * _softplus(-lam)
    a = jnp.exp(log_a)
    mult = jnp.sqrt(_neg_expm1(2.0 * log_a))
    if first_is_pos0 is not None:
        row = lax.broadcasted_iota(jnp.int32, xc.shape, 0)
        mult = jnp.where(jnp.logical_and(first_is_pos0, row == 0), 1.0, mult)
    return a, mult * i * xc


def _rnn_prompt_kernel(xr_ref, gr_ref, cw_ref, cb_ref, wr_ref, br_ref, wi_ref, bi_ref, lam_ref,
                       yr_ref, hlast_ref, tail_ref, h_ref, *, tb):
    t = pl.program_id(1)

    @pl.when(t == 0)
    def _():
        tail_ref[...] = jnp.zeros_like(tail_ref)
        h_ref[...] = jnp.zeros_like(h_ref)

    x = xr_ref[...]
    tail = tail_ref[...]
    row8 = lax.broadcasted_iota(jnp.int32, (SUBLANES, D_RNN), 0)
    cw = cw_ref[...]
    xc = cb_ref[...] + cw[CONV_W - 1:CONV_W, :] * x
    for k in range(1, CONV_W):
        xs = pltpu.roll(x, k, axis=0)
        head = jnp.where(row8 < k, pltpu.roll(tail, k, axis=0), xs[:SUBLANES])
        xs = jnp.concatenate([head, xs[SUBLANES:]], axis=0)
        xc = xc + cw[CONV_W - 1 - k:CONV_W - k, :] * xs
    tail_ref[...] = x[tb - SUBLANES:, :]

    a, b = _lru_coeffs(xc, wr_ref[...], br_ref[...], wi_ref[...], bi_ref[...], lam_ref[...], t == 0)

    ng = tb // SUBLANES
    a3 = a.reshape(ng, SUBLANES, D_RNN)
    b3 = b.reshape(ng, SUBLANES, D_RNN)
    sub = lax.broadcasted_iota(jnp.int32, (ng, SUBLANES, D_RNN), 1)
    d = 1
    while d < SUBLANES:
        a_sh = pltpu.roll(a3, d, axis=1)
        b_sh = pltpu.roll(b3, d, axis=1)
        keep = sub < d
        b3 = jnp.where(keep, b3, a3 * b_sh + b3)
        a3 = jnp.where(keep, a3, a3 * a_sh)
        d *= 2
    carry = h_ref[...]
    hs = []
    for g in range(ng):
        hs.append(b3[g] + a3[g] * carry)
        carry = hs[-1][SUBLANES - 1:SUBLANES, :]
    h_ref[...] = carry
    hlast_ref[0] = carry
    yr_ref[...] = (jnp.concatenate(hs, axis=0) * jax.nn.gelu(gr_ref[...])).astype(BF16)


def _rnn_prompt(z, batch, seq, cw, cb, wr, br, wi, bi, lam, tb=256):
    nt = seq // tb
    vec = pl.BlockSpec((1, D_RNN), lambda b, t: (0, 0))
    mat = pl.BlockSpec((D_RNN, D_RNN), lambda b, t: (0, 0))
    return pl.pallas_call(
        functools.partial(_rnn_prompt_kernel, tb=tb),
        out_shape=(jax.ShapeDtypeStruct((batch * seq, D_RNN), BF16),
                   jax.ShapeDtypeStruct((batch, 1, D_RNN), F32)),
        grid=(batch, nt),
        in_specs=[
            pl.BlockSpec((tb, D_RNN), lambda b, t: (b * nt + t, OFF_XR // D_RNN)),
            pl.BlockSpec((tb, D_RNN), lambda b, t: (b * nt + t, OFF_GR // D_RNN)),
            pl.BlockSpec((CONV_W, D_RNN), lambda b, t: (0, 0)),
            vec, mat, vec, mat, vec, vec,
        ],
        out_specs=(pl.BlockSpec((tb, D_RNN), lambda b, t: (b * nt + t, 0)),
                   pl.BlockSpec((1, 1, D_RNN), lambda b, t: (b, 0, 0))),
        scratch_shapes=[pltpu.VMEM((SUBLANES, D_RNN), F32), pltpu.VMEM((1, D_RNN), F32)],
        compiler_params=_params("parallel", "arbitrary"),
        name="rnn_prompt",
    )(z, z, cw, cb, wr, br, wi, bi, lam)


def _rnn_sample_kernel(z_ref, buf_ref, h0_ref, cw_ref, cb_ref, wr_ref, br_ref, wi_ref,
                       bi_ref, lam_ref, yr_ref, hlast_ref, *, steps):
    col = lambda s, off: z_ref[:, s * D_IN + off:s * D_IN + off + D_RNN]
    cw = cw_ref[...]
    xp = [buf_ref[:, k * D_RNN:(k + 1) * D_RNN] for k in range(CONV_W - 1)] + [col(s, OFF_XR) for s in range(steps)]
    h = h0_ref[...]
    for s in range(steps):
        xc = cb_ref[...] + sum(cw[k:k + 1, :] * xp[s + k] for k in range(CONV_W))
        a, b = _lru_coeffs(xc, wr_ref[...], br_ref[...], wi_ref[...], bi_ref[...], lam_ref[...], None)
        h = a * h + b
        yr_ref[:, s * D_RNN:(s + 1) * D_RNN] = (h * jax.nn.gelu(col(s, OFF_GR))).astype(BF16)
    hlast_ref[...] = h


def _rnn_sample(z2, buf2, h0, cw, cb, wr, br, wi, bi, lam):
    db = z2.shape[0]
    steps = z2.shape[1] // D_IN
    return pl.pallas_call(
        functools.partial(_rnn_sample_kernel, steps=steps),
        out_shape=(jax.ShapeDtypeStruct((db, steps * D_RNN), BF16),
                   jax.ShapeDtypeStruct((db, D_RNN), F32)),
        compiler_params=pltpu.CompilerParams(vmem_limit_bytes=VMEM_LIMIT_BYTES),
        name="rnn_sample",
    )(z2, buf2, h0, cw, cb, wr, br, wi, bi, lam)


def _headnorm_rope(x, gain, seg_ones, cos_t, sin_up_t, sin_dn_t, scale):
    w = x.shape[1]
    outs = []
    for c in range(w // LANES):
        xc = x[:, c * LANES:(c + 1) * LANES]
        sq = xc * xc
        hi = sq.astype(BF16)
        lo = (sq - hi.astype(F32)).astype(BF16)
        ms = (_dot(hi, seg_ones) + _dot(lo, seg_ones)) * (1.0 / HEAD_DIM)
        xt = xc * lax.rsqrt(ms + EPS) * gain
        up = pltpu.roll(xt, LANES - ROT_DIM // 2, axis=1)
        dn = pltpu.roll(xt, ROT_DIM // 2, axis=1)
        outs.append((xt * cos_t + up * sin_up_t + dn * sin_dn_t) * scale)
    return jnp.concatenate(outs, axis=1)


def _qk_prep_kernel(q_ref, k_ref, v_ref, cos_ref, sup_ref, sdn_ref, qg_ref, kg_ref, seg_ref,
                    qo_ref, kf_ref, kb_ref, vb_ref):
    cos_t, sup, sdn = cos_ref[...], sup_ref[...], sdn_ref[...]
    seg = seg_ref[...]
    q = _headnorm_rope(q_ref[...], qg_ref[...], seg, cos_t, sup, sdn, HEAD_DIM ** -0.5)
    q = jnp.concatenate([q[:, h * HEAD_DIM:(h + 1) * HEAD_DIM] for h in _head_perm()], axis=1)
    qo_ref[...] = q.astype(BF16)
    k = _headnorm_rope(k_ref[...], kg_ref[...], seg, cos_t, sup, sdn, 1.0)
    kf_ref[...] = k
    kb_ref[...] = k.astype(BF16)
    vb_ref[...] = v_ref[...].astype(BF16)


def _qk_prep(z, tabs, qg, kg, seg_ones, tm=512):
    n = z.shape[0]
    tm = min(tm, n)
    cos_t, sup_t, sdn_t = tabs
    ntab = cos_t.shape[0] // tm
    tab = pl.BlockSpec((tm, LANES), lambda i: (i % ntab, 0))
    const = lambda shape: pl.BlockSpec(shape, lambda i: (0, 0))
    return pl.pallas_call(
        _qk_prep_kernel,
        out_shape=(jax.ShapeDtypeStruct((n, D_Q), BF16), jax.ShapeDtypeStruct((n, D_KV), F32),
                   jax.ShapeDtypeStruct((n, D_KV), BF16), jax.ShapeDtypeStruct((n, D_KV), BF16)),
        grid=(n // tm,),
        in_specs=[
            pl.BlockSpec((tm, D_Q), lambda i: (i, OFF_Q // D_Q)),
            pl.BlockSpec((tm, D_KV), lambda i: (i, OFF_K // D_KV)),
            pl.BlockSpec((tm, D_KV), lambda i: (i, OFF_V // D_KV)),
            tab, tab, tab,
            const((1, LANES)), const((1, LANES)), const((LANES, LANES)),
        ],
        out_specs=(pl.BlockSpec((tm, D_Q), lambda i: (i, 0)), pl.BlockSpec((tm, D_KV), lambda i: (i, 0)),
                   pl.BlockSpec((tm, D_KV), lambda i: (i, 0)), pl.BlockSpec((tm, D_KV), lambda i: (i, 0))),
        compiler_params=_params("parallel"),
        name="qk_prep",
    )(z, z, z, cos_t, sup_t, sdn_t, qg, kg, seg_ones)


def _sink_attention(q, k, v, valid, sink_ref, tq):
    lane_head = lax.broadcasted_iota(jnp.int32, (tq, D_KV), 1) // HEAD_DIM
    rowg = lax.broadcasted_iota(jnp.int32, (GROUP * tq, 1), 0) // tq
    validg = jnp.concatenate([valid] * GROUP, axis=0)
    out = [jnp.zeros((tq, D_KV), F32) for _ in range(GROUP)]
    for j in range(N_KV_HEADS):
        sel = lane_head == j
        keep = jnp.where(sel, 1.0, 0.0).astype(BF16)
        qs = jnp.concatenate([q[:, g * D_KV:(g + 1) * D_KV] * keep for g in range(GROUP)], axis=0)
        s = jnp.where(validg, _dot_nt(qs, k), NEG_INF)
        sk = jnp.zeros((GROUP * tq, 1), F32)
        for g in range(GROUP):
            sk = jnp.where(rowg == g, sink_ref[j * GROUP + g], sk)
        m = jnp.maximum(jnp.max(s, axis=-1, keepdims=True), sk)
        p = jnp.exp(s - m)
        denom = jnp.sum(p, axis=-1, keepdims=True) + jnp.exp(sk - m)
        pv = _dot(p.astype(BF16), v) / denom
        for g in range(GROUP):
            out[g] = jnp.where(sel, pv[g * tq:(g + 1) * tq], out[g])
    return jnp.concatenate(out, axis=1)


def _attn_prompt_kernel(sink_ref, q_ref, kp_ref, kc_ref, vp_ref, vc_ref, o_ref):
    nb = pl.program_id(1)
    k = jnp.concatenate([kp_ref[...], kc_ref[...]], axis=0)
    v = jnp.concatenate([vp_ref[...], vc_ref[...]], axis=0)
    i = lax.broadcasted_iota(jnp.int32, (WINDOW, 2 * WINDOW), 0)
    j = lax.broadcasted_iota(jnp.int32, (WINDOW, 2 * WINDOW), 1)
    d = WINDOW + i - j
    valid = (d >= 0) & (d < WINDOW) & ((j >= WINDOW) | (nb > 0))
    o_ref[...] = _sink_attention(q_ref[...], k, v, valid, sink_ref, WINDOW).astype(BF16)


def _attn_prompt(sinks, q, kb, vb, batch, seq):
    nblk = seq // WINDOW
    cur = lambda w: pl.BlockSpec((WINDOW, w), lambda b, t: (b * nblk + t, 0))
    prev = lambda w: pl.BlockSpec((WINDOW, w), lambda b, t: (b * nblk + jnp.maximum(t - 1, 0), 0))
    return pl.pallas_call(
        _attn_prompt_kernel,
        out_shape=jax.ShapeDtypeStruct((batch * seq, D_Q), BF16),
        grid=(batch, nblk),
        in_specs=[pl.BlockSpec(memory_space=pltpu.SMEM), cur(D_Q), prev(D_KV), cur(D_KV),
                  prev(D_KV), cur(D_KV)],
        out_specs=cur(D_Q),
        compiler_params=_params("parallel", "arbitrary"),
        name="attn_prompt",
    )(sinks, q, kb, kb, vb, vb)


def _shift_in(cache, new8, steps):
    wbuf = cache.shape[0]
    rolled = pltpu.roll(cache, wbuf - steps, axis=0)
    row8 = lax.broadcasted_iota(jnp.int32, new8.shape, 0)
    tail = jnp.where(row8 >= SUBLANES - steps, pltpu.roll(new8, SUBLANES - steps, axis=0),
                     rolled[wbuf - SUBLANES:])
    return jnp.concatenate([rolled[:wbuf - SUBLANES], tail], axis=0)


def _attn_sample_kernel(sink_ref, q_ref, kn_ref, vn_ref, knf_ref, vnf_ref, ck_ref, cv_ref, o_ref, ok_ref, ov_ref,
                        *, bs, tq, wbuf, steps):
    t = lax.broadcasted_iota(jnp.int32, (tq, wbuf + tq), 0)
    c = lax.broadcasted_iota(jnp.int32, (tq, wbuf + tq), 1)
    d = wbuf + t - c
    valid = (d >= 0) & (d < WINDOW)
    for s in range(bs):
        ck, cv = ck_ref[s], cv_ref[s]
        k = jnp.concatenate([ck.astype(BF16), kn_ref[s]], axis=0)
        v = jnp.concatenate([cv.astype(BF16), vn_ref[s]], axis=0)
        o_ref[s] = _sink_attention(q_ref[s], k, v, valid, sink_ref, tq).astype(BF16)
        ok_ref[s] = _shift_in(ck, knf_ref[s], steps)
        ov_ref[s] = _shift_in(cv, vnf_ref[s], steps)


def _attn_sample(sinks, q3, kn3, vn3, knf3, vnf3, cache_k, cache_v, steps, bs=8):
    db, tq, _ = q3.shape
    wbuf = cache_k.shape[1]
    assert steps <= SUBLANES and knf3.shape[1] == SUBLANES and db % bs == 0
    blk = lambda r, w: pl.BlockSpec((bs, r, w), lambda b: (b, 0, 0))
    cache = jax.ShapeDtypeStruct((db, wbuf, D_KV), F32)
    return pl.pallas_call(
        functools.partial(_attn_sample_kernel, bs=bs, tq=tq, wbuf=wbuf, steps=steps),
        out_shape=(jax.ShapeDtypeStruct((db, tq, D_Q), BF16), cache, cache),
        grid=(db // bs,),
        in_specs=[pl.BlockSpec(memory_space=pltpu.SMEM), blk(tq, D_Q), blk(tq, D_KV), blk(tq, D_KV),
                  blk(SUBLANES, D_KV), blk(SUBLANES, D_KV), blk(wbuf, D_KV), blk(wbuf, D_KV)],
        out_specs=(blk(tq, D_Q), blk(wbuf, D_KV), blk(wbuf, D_KV)),
        compiler_params=_params("parallel"),
        name="attn_sample",
    )(sinks, q3, kn3, vn3, knf3, vnf3, cache_k, cache_v)


def _merge_kernel(yr_ref, o_ref, ga_ref, gb_ref, wr_ref, wa_ref, m_ref):
    a = _dot(yr_ref[...], wr_ref[...])
    b = _dot(o_ref[...], wa_ref[...])
    m_ref[...] = (jax.nn.sigmoid(ga_ref[...]) * a + jax.nn.sigmoid(gb_ref[...]) * b).astype(BF16)


def _merge(yr, o, z, wr, wa, tm=512, tn=D_MODEL):
    n = yr.shape[0]
    tm = min(tm, n)
    nj = D_MODEL // tn
    return pl.pallas_call(
        _merge_kernel,
        out_shape=jax.ShapeDtypeStruct((n, D_MODEL), BF16),
        grid=(n // tm, nj),
        in_specs=[
            pl.BlockSpec((tm, D_RNN), lambda i, j: (i, 0)),
            pl.BlockSpec((tm, D_Q), lambda i, j: (i, 0)),
            pl.BlockSpec((tm, tn), lambda i, j: (i, OFF_GA // tn + j)),
            pl.BlockSpec((tm, tn), lambda i, j: (i, OFF_GB // tn + j)),
            pl.BlockSpec((D_RNN, tn), lambda i, j: (0, j)),
            pl.BlockSpec((D_Q, tn), lambda i, j: (0, j)),
        ],
        out_specs=pl.BlockSpec((tm, tn), lambda i, j: (i, j)),
        compiler_params=_params("parallel", "arbitrary"),
        name="merge",
    )(yr, o, z, z, wr, wa)


def _out_proj_kernel(x_ref, m_ref, w_ref, g_ref, x1_ref, n2_ref):
    x1 = x_ref[...] + _dot(m_ref[...], w_ref[...])
    x1_ref[...] = x1
    n2_ref[...] = _rmsnorm_rows(x1, g_ref[...]).astype(BF16)


def _out_proj(x, m, w, g, tm=512):
    n = x.shape[0]
    tm = min(tm, n)
    row = lambda dt: pl.BlockSpec((tm, D_MODEL), lambda i: (i, 0))
    return pl.pallas_call(
        _out_proj_kernel,
        out_shape=(jax.ShapeDtypeStruct((n, D_MODEL), F32), jax.ShapeDtypeStruct((n, D_MODEL), BF16)),
        grid=(n // tm,),
        in_specs=[row(F32), row(BF16), pl.BlockSpec((D_MODEL, D_MODEL), lambda i: (0, 0)),
                  pl.BlockSpec((1, D_MODEL), lambda i: (0, 0))],
        out_specs=(row(F32), row(BF16)),
        compiler_params=_params("parallel"),
        name="out_proj",
    )(x, m, w, g)


def _topk_slabs(slabs, keys, k):
    t = slabs[0].shape[1]
    slot = lax.broadcasted_iota(jnp.int32, (k, t), 0)
    vals = jnp.zeros((k, t), F32)
    ids = jnp.zeros((k, t), F32)
    for r in range(k):
        best, bkey = slabs[0], keys[0]
        for sl, ky in zip(slabs[1:], keys[1:]):
            take = sl > best
            best = jnp.where(take, sl, best)
            bkey = jnp.where(take, ky, bkey)
        m = jnp.max(best, axis=0, keepdims=True)
        i = jnp.min(jnp.where(best == m, bkey, jnp.inf), axis=0, keepdims=True)
        vals = jnp.where(slot == r, m, vals)
        ids = jnp.where(slot == r, i, ids)
        slabs = [jnp.where(ky == i, -jnp.inf, sl) for sl, ky in zip(slabs, keys)]
    return vals, ids


def _peer_route_kernel(n2_ref, wq_ref, sk_ref, idx_ref, gate_ref):
    q = _dot(n2_ref[...], wq_ref[...]).astype(BF16)
    gates, experts = [], []
    tcol = q.shape[0]
    row8 = lax.broadcasted_iota(jnp.int32, (SUBLANES, tcol), 0).astype(F32)
    key_slabs = [row8 + float(a) for a in range(0, N_KEYS, SUBLANES)]
    assert PEER_TOPK == 2 * SUBLANES
    for h in range(PEER_HEADS):
        top = []
        for p in range(2):
            c = (2 * h + p) * D_HALF_KEY
            s = _dot_nt(sk_ref[2 * h + p], q[:, c:c + D_HALF_KEY])
            top.append(_topk_slabs([s[a:a + SUBLANES] for a in range(0, N_KEYS, SUBLANES)], key_slabs, PEER_TOPK))
        (s1, i1), (s2, i2) = top
        hk = SUBLANES
        lo, hi = slice(0, hk), slice(hk, 2 * hk)
        one = lambda a: slice(a, a + 1)
        groups = [(one(0), lo, row8), (one(0), hi, row8 + float(hk))]
        groups += [(one(a), lo, row8 + float(a * PEER_TOPK)) for a in range(1, hk)]
        groups += [(hi, one(0), (row8 + float(hk)) * float(PEER_TOPK))]
        cand = [s1[ra] + s2[rb] for ra, rb, _ in groups]
        cidx = [i1[ra] * float(N_KEYS) + i2[rb] for ra, rb, _ in groups]
        rid = [key for _, _, key in groups]
        top_s, pos = _topk_slabs(cand, rid, PEER_TOPK)
        slot = lax.broadcasted_iota(jnp.int32, top_s.shape, 0)
        ids = jnp.zeros(top_s.shape, F32)
        for r in range(PEER_TOPK):
            hit = [jnp.where(ky == pos[r:r + 1, :], ci, -1.0) for ky, ci in zip(rid, cidx)]
            e = jnp.max(functools.reduce(jnp.maximum, hit), axis=0, keepdims=True)
            ids = jnp.where(slot == r, e, ids)
        w = jnp.exp(top_s - top_s[0:1, :])
        gates.append(w / jnp.sum(w, axis=0, keepdims=True))
        experts.append(ids)
    gate_ref[...] = jnp.concatenate(gates, axis=0).T
    idx_ref[...] = jnp.concatenate(experts, axis=0).T.astype(jnp.int32)


def _peer_route(n2, wq, sk, tm=256):
    n = n2.shape[0]
    tm = min(tm, n)
    return pl.pallas_call(
        _peer_route_kernel,
        out_shape=(jax.ShapeDtypeStruct((n, N_PICKS), jnp.int32), jax.ShapeDtypeStruct((n, N_PICKS), F32)),
        grid=(n // tm,),
        in_specs=[pl.BlockSpec((tm, D_MODEL), lambda i: (i, 0)),
                  pl.BlockSpec((D_MODEL, PEER_HEADS * D_KEY), lambda i: (0, 0)),
                  pl.BlockSpec((2 * PEER_HEADS, N_KEYS, D_HALF_KEY), lambda i: (0, 0, 0))],
        out_specs=(pl.BlockSpec((tm, N_PICKS), lambda i: (i, 0)), pl.BlockSpec((tm, N_PICKS), lambda i: (i, 0))),
        compiler_params=_params("parallel"),
        name="peer_route",
    )(n2, wq, sk)


D_TILES = D_MODEL // LANES
assert D_TILES == BF16_TILE_ROWS
PICK_GROUP = 16
WAIT_GROUP = 4


def _peer_mix_kernel(idx_ref, idxn_ref, x_ref, gate_ref, exp_ref, sel_ref, tab_ref, o_ref, buf_a, buf_b,
                     sem_ref, *, tb, nsteps):
    i = pl.program_id(0)
    bufs = (buf_a, buf_b)

    group_rows = WAIT_GROUP * N_PICKS

    def start_token(ids_ref, row, half, t, picks=(0, N_PICKS)):
        for j in range(*picks):
            pltpu.make_async_copy(tab_ref.at[ids_ref[row, j]], bufs[half].at[t * N_PICKS + j],
                                  sem_ref.at[half, t // WAIT_GROUP]).start(priority=j % 2)

    def wait_group(half, g):
        pltpu.make_async_copy(tab_ref.at[pl.ds(0, group_rows)], bufs[half].at[pl.ds(g * group_rows, group_rows)],
                              sem_ref.at[half, g]).wait()

    @pl.when(i == 0)
    def _():
        for t in range(tb):
            start_token(idx_ref, t, 0, t)

    rows = N_PICKS * D_TILES
    diag = (lax.broadcasted_iota(jnp.int32, (D_TILES, rows), 1) % D_TILES
            == lax.broadcasted_iota(jnp.int32, (D_TILES, rows), 0))

    sel = sel_ref[...]
    lane = lax.broadcasted_iota(jnp.int32, (N_PICKS, LANES), 1)

    first_picks, second_picks = (0, N_PICKS // 2), (N_PICKS // 2, N_PICKS)

    def run_tile(half, start_other):
        buf = bufs[half]
        hcols = jnp.zeros((N_PICKS, LANES), F32)
        for t in range(tb):
            if t % WAIT_GROUP == 0:
                wait_group(half, t // WAIT_GROUP)
            start_other(t, first_picks)
            u3 = buf[t * N_PICKS:(t + 1) * N_PICKS, 0:D_TILES, :]
            p = (u3 * x_ref[half * tb + t][None, :, :]).reshape(rows, LANES)
            grp = PICK_GROUP * D_TILES
            r = jnp.concatenate([_dot(sel, p[c * grp:(c + 1) * grp]) for c in range(N_PICKS // PICK_GROUP)],
                                axis=0)
            hcols = jnp.where(lane == t, jnp.sum(r, axis=1, keepdims=True), hcols)
        h = hcols.T[0:tb, :]
        w = (jax.nn.gelu(h) * gate_ref[half * tb:(half + 1) * tb, :]).astype(BF16)
        wrow = _dot(w, exp_ref[...])
        for t in range(tb):
            start_other(t, second_picks)
            v = buf[t * N_PICKS:(t + 1) * N_PICKS, D_TILES:2 * D_TILES, :].reshape(rows, LANES)
            wm = jnp.where(diag, wrow[t:t + 1, :], 0.0).astype(BF16)
            o_ref[half * tb + t] = _dot(wm, v)

    run_tile(0, lambda t, picks: start_token(idx_ref, tb + t, 1, t, picks))
    run_tile(1, lambda t, picks: start_token(idxn_ref, t, 0, t, picks))

    @pl.when(i == nsteps - 1)
    def _():
        for g in range(tb // WAIT_GROUP):
            wait_group(0, g)


def _peer_mix(idx, gates, x3, table, expand, sel, tb=BF16_TILE_ROWS):
    n = idx.shape[0]
    nsteps = n // (2 * tb)
    rows = N_PICKS * D_TILES
    out = pl.pallas_call(
        functools.partial(_peer_mix_kernel, tb=tb, nsteps=nsteps),
        out_shape=jax.ShapeDtypeStruct((n, D_TILES, LANES), F32),
        grid=(nsteps,),
        in_specs=[
            pl.BlockSpec((2 * tb, N_PICKS), lambda i: (i, 0), memory_space=pltpu.SMEM),
            pl.BlockSpec((tb, N_PICKS), lambda i: (jnp.minimum(2 * i + 2, 2 * nsteps - 2), 0),
                         memory_space=pltpu.SMEM),
            pl.BlockSpec((2 * tb, D_TILES, LANES), lambda i: (i, 0, 0)),
            pl.BlockSpec((2 * tb, N_PICKS), lambda i: (i, 0)),
            pl.BlockSpec((N_PICKS, rows), lambda i: (0, 0)),
            pl.BlockSpec((PICK_GROUP, PICK_GROUP * D_TILES), lambda i: (0, 0)),
            pl.BlockSpec(memory_space=pl.ANY),
        ],
        out_specs=pl.BlockSpec((2 * tb, D_TILES, LANES), lambda i: (i, 0, 0)),
        scratch_shapes=[pltpu.VMEM((tb * N_PICKS, 2 * D_TILES, LANES), BF16),
                        pltpu.VMEM((tb * N_PICKS, 2 * D_TILES, LANES), BF16),
                        pltpu.SemaphoreType.DMA((2, tb // WAIT_GROUP))],
        compiler_params=_params("arbitrary"),
        name="peer_mix",
    )(idx, idx, x3, gates, expand, sel, table)
    return out.reshape(n, D_MODEL)


def _pack_table_kernel(u_ref, v_ref, o_ref):
    o_ref[:, 0:D_TILES, :] = u_ref[...].astype(BF16)
    o_ref[:, D_TILES:2 * D_TILES, :] = v_ref[...].astype(BF16)


def _pack_table(u3, v3, te=256):
    e = u3.shape[0]
    te = min(te, e)
    blk = pl.BlockSpec((te, D_TILES, LANES), lambda i: (i, 0, 0))
    return pl.pallas_call(
        _pack_table_kernel,
        out_shape=jax.ShapeDtypeStruct((e, 2 * D_TILES, LANES), BF16),
        grid=(e // te,),
        in_specs=[blk, blk],
        out_specs=pl.BlockSpec((te, 2 * D_TILES, LANES), lambda i: (i, 0, 0)),
        compiler_params=_params("parallel"),
        name="pack_table",
    )(u3, v3)


def _ple_kernel(x1_ref, po_ref, ple_ref, g_ref, wp_ref, wg_ref, y_ref):
    x2 = x1_ref[...] + po_ref[...]
    n3 = _rmsnorm_rows(x2, g_ref[...]).astype(BF16)
    emb = _dot(ple_ref[...].astype(BF16), wp_ref[...])
    y_ref[...] = x2 + emb * jax.nn.sigmoid(_dot(n3, wg_ref[...]))


def _ple(x1, po, ple, g, wp, wg, tm=512):
    n = x1.shape[0]
    tm = min(tm, n)
    row = pl.BlockSpec((tm, D_MODEL), lambda i: (i, 0))
    return pl.pallas_call(
        _ple_kernel,
        out_shape=jax.ShapeDtypeStruct((n, D_MODEL), F32),
        grid=(n // tm,),
        in_specs=[row, row, pl.BlockSpec((tm, D_PLE), lambda i: (i, 0)),
                  pl.BlockSpec((1, D_MODEL), lambda i: (0, 0)),
                  pl.BlockSpec((D_PLE, D_MODEL), lambda i: (0, 0)),
                  pl.BlockSpec((D_MODEL, D_MODEL), lambda i: (0, 0))],
        out_specs=row,
        compiler_params=_params("parallel"),
        name="ple",
    )(x1, po, ple, g, wp, wg)


def _rope_tables(pos):
    half = ROT_DIM // 2
    inv = ROPE_THETA ** (-jnp.arange(0, ROT_DIM, 2, dtype=F32) / ROT_DIM)
    ang = pos.astype(F32)[:, None] * inv[None, :]
    cos, sin = jnp.cos(ang), jnp.sin(ang)
    n = pos.shape[0]
    pad = jnp.zeros((n, HEAD_DIM - ROT_DIM), F32)
    zh = jnp.zeros((n, half), F32)
    cos_h = jnp.concatenate([cos, cos, pad + 1.0], axis=1)
    up_h = jnp.concatenate([-sin, zh, pad], axis=1)
    dn_h = jnp.concatenate([zh, sin, pad], axis=1)
    rep = LANES // HEAD_DIM
    return tuple(jnp.tile(a, (1, rep)) for a in (cos_h, up_h, dn_h))


def _block_diag(w):
    eye = jnp.eye(RNN_BLOCKS, dtype=w.dtype)
    return jnp.einsum("ncd,nm->ncmd", w, eye).reshape(D_RNN, D_RNN)


def _head_perm():
    return np.array([j * GROUP + g for g in range(GROUP) for j in range(N_KV_HEADS)])


def _token_pipeline_tail(x, z, yr, o, w, ple):
    m = _merge(yr, o, z, w["proj_rnn"], w["proj_attn"])
    x1, n2 = _out_proj(x, m, w["out"], w["norm_ffn"])
    idx, gate = _peer_route(n2, w["peer_q"], w["sub_keys"])
    po = _peer_mix(idx, gate, n2.reshape(-1, D_TILES, LANES), w["peer_table"], w["expand"], w["pick_sum"])
    return _ple(x1, po, ple, w["norm_ple"], w["ple"], w["ple_gate"])


def kernel(x_prompt, x_sample, p_prompt, p_sample, state_conv, state_rglru, cache_k, cache_v, norm_mix, w_in, conv_w, conv_b, w_rgate, b_rgate, w_igate, b_igate, lru_lambda, w_proj_rnn, q_norm, k_norm, attn_sinks, w_proj_attn, w_out, norm_ffn, w_peer_q, peer_sub_keys, peer_u, peer_v, w_ple, norm_ple, w_ple_gate):
    depth = w_in.shape[0]
    assert depth == 1
    l = 0
    B, S, _ = x_prompt.shape
    DB, DS, _ = x_sample.shape
    wbuf = cache_k.shape[2]

    hp = _head_perm()
    offs = np.cumsum([0, D_RNN, D_RNN, D_Q, D_KV, D_KV, D_MODEL, D_MODEL])
    xr_c, gr_c, q_c, k_c, v_c, ga_c, gb_c = [np.arange(offs[i], offs[i + 1]) for i in range(7)]
    cols = np.concatenate([ga_c, gb_c, xr_c, gr_c, q_c, k_c, v_c])
    row2 = lambda a: a[l].reshape(1, -1)
    w = {
        "proj_rnn": w_proj_rnn[l].astype(BF16),
        "proj_attn": w_proj_attn[l].reshape(N_Q_HEADS, HEAD_DIM, D_MODEL)[hp].reshape(D_Q, D_MODEL).astype(BF16),
        "out": w_out[l].astype(BF16),
        "norm_ffn": row2(norm_ffn),
        "peer_q": w_peer_q[l].astype(BF16),
        "sub_keys": peer_sub_keys[l].reshape(2 * PEER_HEADS, N_KEYS, D_HALF_KEY).astype(BF16),
        "peer_table": _pack_table(peer_u[l].reshape(N_EXPERTS, D_TILES, LANES),
                                  peer_v[l].reshape(N_EXPERTS, D_TILES, LANES)),
        "norm_ple": row2(norm_ple),
        "ple": w_ple[l].astype(BF16),
        "ple_gate": w_ple_gate[l].astype(BF16),
    }
    expand = np.repeat(np.eye(N_PICKS, dtype=np.float32), D_TILES, axis=1)
    w["expand"] = jnp.asarray(expand, BF16)
    w["pick_sum"] = jnp.asarray(expand[:PICK_GROUP, :PICK_GROUP * D_TILES], BF16)
    runs = np.split(cols, np.flatnonzero(np.diff(cols) != 1) + 1)
    w_in_b = jnp.concatenate([w_in[l][:, r[0]:r[-1] + 1] for r in runs], axis=1).astype(BF16)
    g_mix = row2(norm_mix)
    cw, cb = conv_w[l], row2(conv_b)
    wr, br = _block_diag(w_rgate[l]).astype(BF16), row2(b_rgate)
    wi, bi = _block_diag(w_igate[l]).astype(BF16), row2(b_igate)
    lam = row2(lru_lambda)
    rep = LANES // HEAD_DIM
    qg = jnp.tile(q_norm[l], rep).reshape(1, LANES)
    kg = jnp.tile(k_norm[l], rep).reshape(1, LANES)
    seg_ones = jnp.asarray(
        np.kron(np.eye(LANES // HEAD_DIM, dtype=np.float32), np.ones((HEAD_DIM, HEAD_DIM), np.float32)), BF16)
    sinks = attn_sinks[l]

    xp = x_prompt.reshape(B * S, D_MODEL)
    zp = _in_proj(xp, g_mix, w_in_b)
    yr_p, h_p = _rnn_prompt(zp, B, S, cw, cb, wr, br, wi, bi, lam)
    tabs_p = _rope_tables(jnp.arange(S, dtype=jnp.int32))
    q_p, kf_p, kb_p, vb_p = _qk_prep(zp, tabs_p, qg, kg, seg_ones)
    o_p = _attn_prompt(sinks, q_p, kb_p, vb_p, B, S)
    y_p = _token_pipeline_tail(xp, zp, yr_p, o_p, w, p_prompt[l].reshape(B * S, D_PLE))

    zp3 = zp.reshape(B, S, D_IN)
    keep = min(WINDOW, S)
    prompt_conv = zp3[:, S - (CONV_W - 1):, OFF_XR:OFF_XR + D_RNN]
    prompt_k = kf_p.reshape(B, S, D_KV)[:, S - keep:].reshape(B, keep, N_KV_HEADS, HEAD_DIM)
    prompt_v = zp3[:, S - keep:, OFF_V:OFF_V + D_KV].reshape(B, keep, N_KV_HEADS, HEAD_DIM)

    ns = DB * DS
    xs = x_sample.reshape(ns, D_MODEL)
    zs = _in_proj(xs, g_mix, w_in_b)
    zs3 = zs.reshape(DB, DS, D_IN)
    yr_s, h_s = _rnn_sample(zs.reshape(DB, DS * D_IN), state_conv[l].reshape(DB, (CONV_W - 1) * D_RNN),
                            state_rglru[l], cw, cb, wr, br, wi, bi, lam)
    yr_s = yr_s.reshape(ns, D_RNN)
    pos_s = PAST_LEN + jnp.arange(DS, dtype=jnp.int32)
    tabs_s = tuple(jnp.tile(a, (DB, 1)) for a in _rope_tables(pos_s))
    q_s, kf_s, kb_s, vb_s = _qk_prep(zs, tabs_s, qg, kg, seg_ones)
    tq = BF16_TILE_ROWS
    pad_t = lambda a, rows: jnp.pad(a.reshape(DB, DS, -1), ((0, 0), (0, rows - DS), (0, 0)))
    ck = cache_k[l].reshape(DB, wbuf, D_KV)
    cv = cache_v[l].reshape(DB, wbuf, D_KV)
    o_s, sample_k, sample_v = _attn_sample(
        sinks, pad_t(q_s, tq), pad_t(kb_s, tq), pad_t(vb_s, tq), pad_t(kf_s, SUBLANES),
        pad_t(zs3[:, :, OFF_V:OFF_V + D_KV], SUBLANES), ck, cv, DS)
    o_s = o_s[:, :DS].reshape(ns, D_Q)
    y_s = _token_pipeline_tail(xs, zs, yr_s, o_s, w, p_sample[l].reshape(ns, D_PLE))

    sample_conv = jnp.concatenate([state_conv[l], zs3[:, :, OFF_XR:OFF_XR + D_RNN]], axis=1)[:, DS:]
    sample_k = sample_k.reshape(DB, wbuf, N_KV_HEADS, HEAD_DIM)
    sample_v = sample_v.reshape(DB, wbuf, N_KV_HEADS, HEAD_DIM)

    return (y_p.reshape(B, S, D_MODEL), y_s.reshape(DB, DS, D_MODEL),
            prompt_conv[None], h_p.reshape(1, B, D_RNN), prompt_k[None], prompt_v[None],
            sample_conv[None], h_s[None], sample_k[None], sample_v[None])
```

```python
import functools

import jax
import jax.numpy as jnp
import numpy as np
from jax import lax
from jax.experimental import pallas as pl
from jax.experimental.pallas import tpu as pltpu

D_MODEL = 2048
D_RNN = D_MODEL // 2
RNN_BLOCKS = 8
RNN_BLOCK = D_RNN // RNN_BLOCKS
CONV_W = 4
LRU_C = 8.0
HEAD_DIM = 64
N_Q_HEADS = D_MODEL // 2 // HEAD_DIM
N_KV_HEADS = 4
GROUP = N_Q_HEADS // N_KV_HEADS
D_Q = N_Q_HEADS * HEAD_DIM
D_KV = N_KV_HEADS * HEAD_DIM
WINDOW = 128
ROT_DIM = HEAD_DIM // 4
ROPE_THETA = 500000.0
N_KEYS = 128
N_EXPERTS = N_KEYS * N_KEYS
PEER_HEADS = 8
PEER_TOPK = 16
D_KEY = 256
D_HALF_KEY = D_KEY // 2
N_PICKS = PEER_HEADS * PEER_TOPK
D_PLE = 256
EPS = 1e-6
NEG_INF = -1e30
PAST_LEN = 16384

LANES = 128
SUBLANES = 8
BF16_TILE_ROWS = 16
VMEM_LIMIT_BYTES = 56 * 1024 * 1024

OFF_GA = 0
OFF_GB = OFF_GA + D_MODEL
OFF_XR = OFF_GB + D_MODEL
OFF_GR = OFF_XR + D_RNN
OFF_Q = OFF_GR + D_RNN
OFF_K = OFF_Q + D_Q
OFF_V = OFF_K + D_KV
D_IN = OFF_V + D_KV

BF16 = jnp.bfloat16
F32 = jnp.float32


def _params(*sem):
    return pltpu.CompilerParams(dimension_semantics=sem, vmem_limit_bytes=VMEM_LIMIT_BYTES)


def _rmsnorm_rows(x, g):
    return x * lax.rsqrt(jnp.mean(x * x, axis=-1, keepdims=True) + EPS) * g


def _dot(a, b):
    return jnp.dot(a, b, preferred_element_type=F32)


def _dot_nt(a, b):
    return lax.dot_general(a, b, (((1,), (1,)), ((), ())), preferred_element_type=F32)


def _in_proj_kernel(x_ref, g_ref, w_ref, z_ref, xn_ref):
    @pl.when(pl.program_id(1) == 0)
    def _():
        xn_ref[...] = _rmsnorm_rows(x_ref[...], g_ref[...]).astype(BF16)

    z_ref[...] = _dot(xn_ref[...], w_ref[...])


def _in_proj(x, g, w, tm=1024, tn=1536):
    n = x.shape[0]
    tm = min(tm, n)
    return pl.pallas_call(
        _in_proj_kernel,
        out_shape=jax.ShapeDtypeStruct((n, D_IN), F32),
        grid=(n // tm, D_IN // tn),
        in_specs=[
            pl.BlockSpec((tm, D_MODEL), lambda i, j: (i, 0)),
            pl.BlockSpec((1, D_MODEL), lambda i, j: (0, 0)),
            pl.BlockSpec((D_MODEL, tn), lambda i, j: (0, j)),
        ],
        out_specs=pl.BlockSpec((tm, tn), lambda i, j: (i, j)),
        scratch_shapes=[pltpu.VMEM((tm, D_MODEL), BF16)],
        compiler_params=_params("parallel", "arbitrary"),
        name="in_proj",
    )(x, g, w)


def _softplus(x):
    return jnp.maximum(x, 0.0) + jnp.log(1.0 + jnp.exp(-jnp.abs(x)))


def _neg_expm1(x):
    t = jnp.tanh(-0.5 * x)
    return 2.0 * t / (1.0 + t)


def _lru_coeffs(xc, wr, br, wi, bi, lam, first_is_pos0):
    xb = xc.astype(BF16)
    r = jax.nn.sigmoid(_dot(xb, wr) + br)
    i = jax.nn.sigmoid(_dot(xb, wi) + bi)
    log_a = -LRU_C * r * _softplus(-lam)
    a = jnp.exp(log_a)
    mult = jnp.sqrt(_neg_expm1(2.0 * log_a))
    if first_is_pos0 is not None:
        row = lax.broadcasted_iota(jnp.int32, xc.shape, 0)
        mult = jnp.where(jnp.logical_and(first_is_pos0, row == 0), 1.0, mult)
    return a, mult * i * xc


def _rnn_prompt_kernel(xr_ref, gr_ref, cw_ref, cb_ref, wr_ref, br_ref, wi_ref, bi_ref, lam_ref,
                       yr_ref, hlast_ref, tail_ref, h_ref, *, tb):
    t = pl.program_id(1)

    @pl.when(t == 0)
    def _():
        tail_ref[...] = jnp.zeros_like(tail_ref)
        h_ref[...] = jnp.zeros_like(h_ref)

    x = xr_ref[...]
    tail = tail_ref[...]
    row8 = lax.broadcasted_iota(jnp.int32, (SUBLANES, D_RNN), 0)
    cw = cw_ref[...]
    xc = cb_ref[...] + cw[CONV_W - 1:CONV_W, :] * x
    for k in range(1, CONV_W):
        xs = pltpu.roll(x, k, axis=0)
        head = jnp.where(row8 < k, pltpu.roll(tail, k, axis=0), xs[:SUBLANES])
        xs = jnp.concatenate([head, xs[SUBLANES:]], axis=0)
        xc = xc + cw[CONV_W - 1 - k:CONV_W - k, :] * xs
    tail_ref[...] = x[tb - SUBLANES:, :]

    a, b = _lru_coeffs(xc, wr_ref[...], br_ref[...], wi_ref[...], bi_ref[...], lam_ref[...], t == 0)

    ng = tb // SUBLANES
    a3 = a.reshape(ng, SUBLANES, D_RNN)
    b3 = b.reshape(ng, SUBLANES, D_RNN)
    sub = lax.broadcasted_iota(jnp.int32, (ng, SUBLANES, D_RNN), 1)
    d = 1
    while d < SUBLANES:
        a_sh = pltpu.roll(a3, d, axis=1)
        b_sh = pltpu.roll(b3, d, axis=1)
        keep = sub < d
        b3 = jnp.where(keep, b3, a3 * b_sh + b3)
        a3 = jnp.where(keep, a3, a3 * a_sh)
        d *= 2
    carry = h_ref[...]
    hs = []
    for g in range(ng):
        hs.append(b3[g] + a3[g] * carry)
        carry = hs[-1][SUBLANES - 1:SUBLANES, :]
    h_ref[...] = carry
    hlast_ref[0] = carry
    yr_ref[...] = (jnp.concatenate(hs, axis=0) * jax.nn.gelu(gr_ref[...])).astype(BF16)


def _rnn_prompt(z, batch, seq, cw, cb, wr, br, wi, bi, lam, tb=256):
    nt = seq // tb
    vec = pl.BlockSpec((1, D_RNN), lambda b, t: (0, 0))
    mat = pl.BlockSpec((D_RNN, D_RNN), lambda b, t: (0, 0))
    return pl.pallas_call(
        functools.partial(_rnn_prompt_kernel, tb=tb),
        out_shape=(jax.ShapeDtypeStruct((batch * seq, D_RNN), BF16),
                   jax.ShapeDtypeStruct((batch, 1, D_RNN), F32)),
        grid=(batch, nt),
        in_specs=[
            pl.BlockSpec((tb, D_RNN), lambda b, t: (b * nt + t, OFF_XR // D_RNN)),
            pl.BlockSpec((tb, D_RNN), lambda b, t: (b * nt + t, OFF_GR // D_RNN)),
            pl.BlockSpec((CONV_W, D_RNN), lambda b, t: (0, 0)),
            vec, mat, vec, mat, vec, vec,
        ],
        out_specs=(pl.BlockSpec((tb, D_RNN), lambda b, t: (b * nt + t, 0)),
                   pl.BlockSpec((1, 1, D_RNN), lambda b, t: (b, 0, 0))),
        scratch_shapes=[pltpu.VMEM((SUBLANES, D_RNN), F32), pltpu.VMEM((1, D_RNN), F32)],
        compiler_params=_params("parallel", "arbitrary"),
        name="rnn_prompt",
    )(z, z, cw, cb, wr, br, wi, bi, lam)


def _rnn_sample_kernel(z_ref, buf_ref, h0_ref, cw_ref, cb_ref, wr_ref, br_ref, wi_ref,
                       bi_ref, lam_ref, yr_ref, hlast_ref, *, steps):
    col = lambda s, off: z_ref[:, s * D_IN + off:s * D_IN + off + D_RNN]
    cw = cw_ref[...]
    xp = [buf_ref[:, k * D_RNN:(k + 1) * D_RNN] for k in range(CONV_W - 1)] + [col(s, OFF_XR) for s in range(steps)]
    h = h0_ref[...]
    for s in range(steps):
        xc = cb_ref[...] + sum(cw[k:k + 1, :] * xp[s + k] for k in range(CONV_W))
        a, b = _lru_coeffs(xc, wr_ref[...], br_ref[...], wi_ref[...], bi_ref[...], lam_ref[...], None)
        h = a * h + b
        yr_ref[:, s * D_RNN:(s + 1) * D_RNN] = (h * jax.nn.gelu(col(s, OFF_GR))).astype(BF16)
    hlast_ref[...] = h


def _rnn_sample(z2, buf2, h0, cw, cb, wr, br, wi, bi, lam):
    db = z2.shape[0]
    steps = z2.shape[1] // D_IN
    return pl.pallas_call(
        functools.partial(_rnn_sample_kernel, steps=steps),
        out_shape=(jax.ShapeDtypeStruct((db, steps * D_RNN), BF16),
                   jax.ShapeDtypeStruct((db, D_RNN), F32)),
        compiler_params=pltpu.CompilerParams(vmem_limit_bytes=VMEM_LIMIT_BYTES),
        name="rnn_sample",
    )(z2, buf2, h0, cw, cb, wr, br, wi, bi, lam)


def _headnorm_rope(x, gain, seg_ones, cos_t, sin_up_t, sin_dn_t, scale):
    w = x.shape[1]
    outs = []
    for c in range(w // LANES):
        xc = x[:, c * LANES:(c + 1) * LANES]
        sq = xc * xc
        hi = sq.astype(BF16)
        lo = (sq - hi.astype(F32)).astype(BF16)
        ms = (_dot(hi, seg_ones) + _dot(lo, seg_ones)) * (1.0 / HEAD_DIM)
        xt = xc * lax.rsqrt(ms + EPS) * gain
        up = pltpu.roll(xt, LANES - ROT_DIM // 2, axis=1)
        dn = pltpu.roll(xt, ROT_DIM // 2, axis=1)
        outs.append((xt * cos_t + up * sin_up_t + dn * sin_dn_t) * scale)
    return jnp.concatenate(outs, axis=1)


def _qk_prep_kernel(q_ref, k_ref, v_ref, cos_ref, sup_ref, sdn_ref, qg_ref, kg_ref, seg_ref,
                    qo_ref, kf_ref, kb_ref, vb_ref):
    cos_t, sup, sdn = cos_ref[...], sup_ref[...], sdn_ref[...]
    seg = seg_ref[...]
    q = _headnorm_rope(q_ref[...], qg_ref[...], seg, cos_t, sup, sdn, HEAD_DIM ** -0.5)
    q = jnp.concatenate([q[:, h * HEAD_DIM:(h + 1) * HEAD_DIM] for h in _head_perm()], axis=1)
    qo_ref[...] = q.astype(BF16)
    k = _headnorm_rope(k_ref[...], kg_ref[...], seg, cos_t, sup, sdn, 1.0)
    kf_ref[...] = k
    kb_ref[...] = k.astype(BF16)
    vb_ref[...] = v_ref[...].astype(BF16)


def _qk_prep(z, tabs, qg, kg, seg_ones, tm=512):
    n = z.shape[0]
    tm = min(tm, n)
    cos_t, sup_t, sdn_t = tabs
    ntab = cos_t.shape[0] // tm
    tab = pl.BlockSpec((tm, LANES), lambda i: (i % ntab, 0))
    const = lambda shape: pl.BlockSpec(shape, lambda i: (0, 0))
    return pl.pallas_call(
        _qk_prep_kernel,
        out_shape=(jax.ShapeDtypeStruct((n, D_Q), BF16), jax.ShapeDtypeStruct((n, D_KV), F32),
                   jax.ShapeDtypeStruct((n, D_KV), BF16), jax.ShapeDtypeStruct((n, D_KV), BF16)),
        grid=(n // tm,),
        in_specs=[
            pl.BlockSpec((tm, D_Q), lambda i: (i, OFF_Q // D_Q)),
            pl.BlockSpec((tm, D_KV), lambda i: (i, OFF_K // D_KV)),
            pl.BlockSpec((tm, D_KV), lambda i: (i, OFF_V // D_KV)),
            tab, tab, tab,
            const((1, LANES)), const((1, LANES)), const((LANES, LANES)),
        ],
        out_specs=(pl.BlockSpec((tm, D_Q), lambda i: (i, 0)), pl.BlockSpec((tm, D_KV), lambda i: (i, 0)),
                   pl.BlockSpec((tm, D_KV), lambda i: (i, 0)), pl.BlockSpec((tm, D_KV), lambda i: (i, 0))),
        compiler_params=_params("parallel"),
        name="qk_prep",
    )(z, z, z, cos_t, sup_t, sdn_t, qg, kg, seg_ones)


def _sink_attention(q, k, v, valid, sink_ref, tq):
    lane_head = lax.broadcasted_iota(jnp.int32, (tq, D_KV), 1) // HEAD_DIM
    blocks = [(j, g) for j in range(N_KV_HEADS) for g in range(GROUP)]
    rowblk = lax.broadcasted_iota(jnp.int32, (N_Q_HEADS * tq, 1), 0) // tq
    keep = [jnp.where(lane_head == j, 1.0, 0.0).astype(BF16) for j in range(N_KV_HEADS)]
    qs = jnp.concatenate([q[:, g * D_KV:(g + 1) * D_KV] * keep[j] for j, g in blocks], axis=0)
    s = jnp.where(jnp.concatenate([valid] * N_Q_HEADS, axis=0), _dot_nt(qs, k), NEG_INF)
    sk = jnp.zeros((N_Q_HEADS * tq, 1), F32)
    for b, (j, g) in enumerate(blocks):
        sk = jnp.where(rowblk == b, sink_ref[j * GROUP + g], sk)
    m = jnp.maximum(jnp.max(s, axis=-1, keepdims=True), sk)
    p = jnp.exp(s - m)
    denom = jnp.sum(p, axis=-1, keepdims=True) + jnp.exp(sk - m)
    pv = _dot(p.astype(BF16), v) / denom
    out = [jnp.zeros((tq, D_KV), F32) for _ in range(GROUP)]
    for b, (j, g) in enumerate(blocks):
        out[g] = jnp.where(lane_head == j, pv[b * tq:(b + 1) * tq], out[g])
    return jnp.concatenate(out, axis=1)


def _attn_prompt_kernel(sink_ref, q_ref, kp_ref, kc_ref, vp_ref, vc_ref, o_ref):
    nb = pl.program_id(1)
    k = jnp.concatenate([kp_ref[...], kc_ref[...]], axis=0)
    v = jnp.concatenate([vp_ref[...], vc_ref[...]], axis=0)
    i = lax.broadcasted_iota(jnp.int32, (WINDOW, 2 * WINDOW), 0)
    j = lax.broadcasted_iota(jnp.int32, (WINDOW, 2 * WINDOW), 1)
    d = WINDOW + i - j
    valid = (d >= 0) & (d < WINDOW) & ((j >= WINDOW) | (nb > 0))
    o_ref[...] = _sink_attention(q_ref[...], k, v, valid, sink_ref, WINDOW).astype(BF16)


def _attn_prompt(sinks, q, kb, vb, batch, seq):
    nblk = seq // WINDOW
    cur = lambda w: pl.BlockSpec((WINDOW, w), lambda b, t: (b * nblk + t, 0))
    prev = lambda w: pl.BlockSpec((WINDOW, w), lambda b, t: (b * nblk + jnp.maximum(t - 1, 0), 0))
    return pl.pallas_call(
        _attn_prompt_kernel,
        out_shape=jax.ShapeDtypeStruct((batch * seq, D_Q), BF16),
        grid=(batch, nblk),
        in_specs=[pl.BlockSpec(memory_space=pltpu.SMEM), cur(D_Q), prev(D_KV), cur(D_KV),
                  prev(D_KV), cur(D_KV)],
        out_specs=cur(D_Q),
        compiler_params=_params("parallel", "arbitrary"),
        name="attn_prompt",
    )(sinks, q, kb, kb, vb, vb)


def _shift_in(cache, new8, steps):
    wbuf = cache.shape[0]
    rolled = pltpu.roll(cache, wbuf - steps, axis=0)
    row8 = lax.broadcasted_iota(jnp.int32, new8.shape, 0)
    tail = jnp.where(row8 >= SUBLANES - steps, pltpu.roll(new8, SUBLANES - steps, axis=0),
                     rolled[wbuf - SUBLANES:])
    return jnp.concatenate([rolled[:wbuf - SUBLANES], tail], axis=0)


def _attn_sample_kernel(sink_ref, q_ref, kn_ref, vn_ref, knf_ref, vnf_ref, ck_ref, cv_ref, o_ref, ok_ref, ov_ref,
                        *, bs, tq, wbuf, steps):
    t = lax.broadcasted_iota(jnp.int32, (tq, wbuf + tq), 0)
    c = lax.broadcasted_iota(jnp.int32, (tq, wbuf + tq), 1)
    d = wbuf + t - c
    valid = (d >= 0) & (d < WINDOW)
    for s in range(bs):
        ck, cv = ck_ref[s], cv_ref[s]
        k = jnp.concatenate([ck.astype(BF16), kn_ref[s]], axis=0)
        v = jnp.concatenate([cv.astype(BF16), vn_ref[s]], axis=0)
        o_ref[s] = _sink_attention(q_ref[s], k, v, valid, sink_ref, tq).astype(BF16)
        ok_ref[s] = _shift_in(ck, knf_ref[s], steps)
        ov_ref[s] = _shift_in(cv, vnf_ref[s], steps)


def _attn_sample(sinks, q3, kn3, vn3, knf3, vnf3, cache_k, cache_v, steps, bs=8):
    db, tq, _ = q3.shape
    wbuf = cache_k.shape[1]
    assert steps <= SUBLANES and knf3.shape[1] == SUBLANES and db % bs == 0
    blk = lambda r, w: pl.BlockSpec((bs, r, w), lambda b: (b, 0, 0))
    cache = jax.ShapeDtypeStruct((db, wbuf, D_KV), F32)
    return pl.pallas_call(
        functools.partial(_attn_sample_kernel, bs=bs, tq=tq, wbuf=wbuf, steps=steps),
        out_shape=(jax.ShapeDtypeStruct((db, tq, D_Q), BF16), cache, cache),
        grid=(db // bs,),
        in_specs=[pl.BlockSpec(memory_space=pltpu.SMEM), blk(tq, D_Q), blk(tq, D_KV), blk(tq, D_KV),
                  blk(SUBLANES, D_KV), blk(SUBLANES, D_KV), blk(wbuf, D_KV), blk(wbuf, D_KV)],
        out_specs=(blk(tq, D_Q), blk(wbuf, D_KV), blk(wbuf, D_KV)),
        compiler_params=_params("parallel"),
        name="attn_sample",
    )(sinks, q3, kn3, vn3, knf3, vnf3, cache_k, cache_v)


def _merge_kernel(yr_ref, o_ref, ga_ref, gb_ref, wr_ref, wa_ref, m_ref):
    a = _dot(yr_ref[...], wr_ref[...])
    b = _dot(o_ref[...], wa_ref[...])
    m_ref[...] = (jax.nn.sigmoid(ga_ref[...]) * a + jax.nn.sigmoid(gb_ref[...]) * b).astype(BF16)


def _merge(yr, o, z, wr, wa, tm=512, tn=D_MODEL):
    n = yr.shape[0]
    tm = min(tm, n)
    nj = D_MODEL // tn
    return pl.pallas_call(
        _merge_kernel,
        out_shape=jax.ShapeDtypeStruct((n, D_MODEL), BF16),
        grid=(n // tm, nj),
        in_specs=[
            pl.BlockSpec((tm, D_RNN), lambda i, j: (i, 0)),
            pl.BlockSpec((tm, D_Q), lambda i, j: (i, 0)),
            pl.BlockSpec((tm, tn), lambda i, j: (i, OFF_GA // tn + j)),
            pl.BlockSpec((tm, tn), lambda i, j: (i, OFF_GB // tn + j)),
            pl.BlockSpec((D_RNN, tn), lambda i, j: (0, j)),
            pl.BlockSpec((D_Q, tn), lambda i, j: (0, j)),
        ],
        out_specs=pl.BlockSpec((tm, tn), lambda i, j: (i, j)),
        compiler_params=_params("parallel", "arbitrary"),
        name="merge",
    )(yr, o, z, z, wr, wa)


def _out_proj_kernel(x_ref, m_ref, w_ref, g_ref, x1_ref, n2_ref):
    x1 = x_ref[...] + _dot(m_ref[...], w_ref[...])
    x1_ref[...] = x1
    n2_ref[...] = _rmsnorm_rows(x1, g_ref[...]).astype(BF16)


def _out_proj(x, m, w, g, tm=512):
    n = x.shape[0]
    tm = min(tm, n)
    row = lambda dt: pl.BlockSpec((tm, D_MODEL), lambda i: (i, 0))
    return pl.pallas_call(
        _out_proj_kernel,
        out_shape=(jax.ShapeDtypeStruct((n, D_MODEL), F32), jax.ShapeDtypeStruct((n, D_MODEL), BF16)),
        grid=(n // tm,),
        in_specs=[row(F32), row(BF16), pl.BlockSpec((D_MODEL, D_MODEL), lambda i: (0, 0)),
                  pl.BlockSpec((1, D_MODEL), lambda i: (0, 0))],
        out_specs=(row(F32), row(BF16)),
        compiler_params=_params("parallel"),
        name="out_proj",
    )(x, m, w, g)


def _topk_slabs(slabs, keys, k):
    t = slabs[0].shape[1]
    slot = lax.broadcasted_iota(jnp.int32, (k, t), 0)
    vals = jnp.zeros((k, t), F32)
    ids = jnp.zeros((k, t), F32)
    for r in range(k):
        best, bkey = slabs[0], keys[0]
        for sl, ky in zip(slabs[1:], keys[1:]):
            take = sl > best
            best = jnp.where(take, sl, best)
            bkey = jnp.where(take, ky, bkey)
        m = jnp.max(best, axis=0, keepdims=True)
        i = jnp.min(jnp.where(best == m, bkey, jnp.inf), axis=0, keepdims=True)
        vals = jnp.where(slot == r, m, vals)
        ids = jnp.where(slot == r, i, ids)
        slabs = [jnp.where(ky == i, -jnp.inf, sl) for sl, ky in zip(slabs, keys)]
    return vals, ids


def _peer_route_kernel(n2_ref, wq_ref, sk_ref, idx_ref, gate_ref):
    q = _dot(n2_ref[...], wq_ref[...]).astype(BF16)
    gates, experts = [], []
    tcol = q.shape[0]
    row8 = lax.broadcasted_iota(jnp.int32, (SUBLANES, tcol), 0).astype(F32)
    key_slabs = [row8 + float(a) for a in range(0, N_KEYS, SUBLANES)]
    assert PEER_TOPK == 2 * SUBLANES
    for h in range(PEER_HEADS):
        top = []
        for p in range(2):
            c = (2 * h + p) * D_HALF_KEY
            s = _dot_nt(sk_ref[2 * h + p], q[:, c:c + D_HALF_KEY])
            top.append(_topk_slabs([s[a:a + SUBLANES] for a in range(0, N_KEYS, SUBLANES)], key_slabs, PEER_TOPK))
        (s1, i1), (s2, i2) = top
        hk = SUBLANES
        lo, hi = slice(0, hk), slice(hk, 2 * hk)
        one = lambda a: slice(a, a + 1)
        groups = [(one(0), lo, row8), (one(0), hi, row8 + float(hk))]
        groups += [(one(a), lo, row8 + float(a * PEER_TOPK)) for a in range(1, hk)]
        groups += [(hi, one(0), (row8 + float(hk)) * float(PEER_TOPK))]
        cand = [s1[ra] + s2[rb] for ra, rb, _ in groups]
        cidx = [i1[ra] * float(N_KEYS) + i2[rb] for ra, rb, _ in groups]
        rid = [key for _, _, key in groups]
        top_s, pos = _topk_slabs(cand, rid, PEER_TOPK)
        slot = lax.broadcasted_iota(jnp.int32, top_s.shape, 0)
        ids = jnp.zeros(top_s.shape, F32)
        for r in range(PEER_TOPK):
            hit = [jnp.where(ky == pos[r:r + 1, :], ci, -1.0) for ky, ci in zip(rid, cidx)]
            e = jnp.max(functools.reduce(jnp.maximum, hit), axis=0, keepdims=True)
            ids = jnp.where(slot == r, e, ids)
        w = jnp.exp(top_s - top_s[0:1, :])
        gates.append(w / jnp.sum(w, axis=0, keepdims=True))
        experts.append(ids)
    gate_ref[...] = jnp.concatenate(gates, axis=0).T
    idx_ref[...] = jnp.concatenate(experts, axis=0).T.astype(jnp.int32)


def _peer_route(n2, wq, sk, tm=256):
    n = n2.shape[0]
    tm = min(tm, n)
    return pl.pallas_call(
        _peer_route_kernel,
        out_shape=(jax.ShapeDtypeStruct((n, N_PICKS), jnp.int32), jax.ShapeDtypeStruct((n, N_PICKS), F32)),
        grid=(n // tm,),
        in_specs=[pl.BlockSpec((tm, D_MODEL), lambda i: (i, 0)),
                  pl.BlockSpec((D_MODEL, PEER_HEADS * D_KEY), lambda i: (0, 0)),
                  pl.BlockSpec((2 * PEER_HEADS, N_KEYS, D_HALF_KEY), lambda i: (0, 0, 0))],
        out_specs=(pl.BlockSpec((tm, N_PICKS), lambda i: (i, 0)), pl.BlockSpec((tm, N_PICKS), lambda i: (i, 0))),
        compiler_params=_params("parallel"),
        name="peer_route",
    )(n2, wq, sk)


D_TILES = D_MODEL // LANES
assert D_TILES == BF16_TILE_ROWS
PICK_GROUP = 16
WAIT_GROUP = 4


def _peer_mix_kernel(idx_ref, idxn_ref, x_ref, gate_ref, exp_ref, sel_ref, tab_ref, o_ref, buf_a, buf_b,
                     sem_ref, *, tb, nsteps):
    i = pl.program_id(0)
    bufs = (buf_a, buf_b)

    group_rows = WAIT_GROUP * N_PICKS

    def start_token(ids_ref, row, half, t, picks=(0, N_PICKS)):
        for j in range(*picks):
            pltpu.make_async_copy(tab_ref.at[ids_ref[row, j]], bufs[half].at[t * N_PICKS + j],
                                  sem_ref.at[half, t // WAIT_GROUP]).start(priority=j % 2)

    def wait_group(half, g):
        pltpu.make_async_copy(tab_ref.at[pl.ds(0, group_rows)], bufs[half].at[pl.ds(g * group_rows, group_rows)],
                              sem_ref.at[half, g]).wait()

    @pl.when(i == 0)
    def _():
        for t in range(tb):
            start_token(idx_ref, t, 0, t)

    rows = N_PICKS * D_TILES
    diag = (lax.broadcasted_iota(jnp.int32, (D_TILES, rows), 1) % D_TILES
            == lax.broadcasted_iota(jnp.int32, (D_TILES, rows), 0))

    sel = sel_ref[...]
    lane = lax.broadcasted_iota(jnp.int32, (N_PICKS, LANES), 1)

    first_picks, second_picks = (0, N_PICKS // 2), (N_PICKS // 2, N_PICKS)

    def run_tile(half, start_other):
        buf = bufs[half]
        hcols = jnp.zeros((N_PICKS, LANES), F32)
        for t in range(tb):
            if t % WAIT_GROUP == 0:
                wait_group(half, t // WAIT_GROUP)
            start_other(t, first_picks)
            u3 = buf[t * N_PICKS:(t + 1) * N_PICKS, 0:D_TILES, :]
            p = (u3 * x_ref[half * tb + t][None, :, :]).reshape(rows, LANES)
            grp = PICK_GROUP * D_TILES
            r = jnp.concatenate([_dot(sel, p[c * grp:(c + 1) * grp]) for c in range(N_PICKS // PICK_GROUP)],
                                axis=0)
            hcols = jnp.where(lane == t, jnp.sum(r, axis=1, keepdims=True), hcols)
        h = hcols.T[0:tb, :]
        w = (jax.nn.gelu(h) * gate_ref[half * tb:(half + 1) * tb, :]).astype(BF16)
        wrow = _dot(w, exp_ref[...])
        for t in range(tb):
            start_other(t, second_picks)
            v = buf[t * N_PICKS:(t + 1) * N_PICKS, D_TILES:2 * D_TILES, :].reshape(rows, LANES)
            wm = jnp.where(diag, wrow[t:t + 1, :], 0.0).astype(BF16)
            o_ref[half * tb + t] = _dot(wm, v)

    run_tile(0, lambda t, picks: start_token(idx_ref, tb + t, 1, t, picks))
    run_tile(1, lambda t, picks: start_token(idxn_ref, t, 0, t, picks))

    @pl.when(i == nsteps - 1)
    def _():
        for g in range(tb // WAIT_GROUP):
            wait_group(0, g)


def _peer_mix(idx, gates, x3, table, expand, sel, tb=BF16_TILE_ROWS):
    n = idx.shape[0]
    nsteps = n // (2 * tb)
    rows = N_PICKS * D_TILES
    out = pl.pallas_call(
        functools.partial(_peer_mix_kernel, tb=tb, nsteps=nsteps),
        out_shape=jax.ShapeDtypeStruct((n, D_TILES, LANES), F32),
        grid=(nsteps,),
        in_specs=[
            pl.BlockSpec((2 * tb, N_PICKS), lambda i: (i, 0), memory_space=pltpu.SMEM),
            pl.BlockSpec((tb, N_PICKS), lambda i: (jnp.minimum(2 * i + 2, 2 * nsteps - 2), 0),
                         memory_space=pltpu.SMEM),
            pl.BlockSpec((2 * tb, D_TILES, LANES), lambda i: (i, 0, 0)),
            pl.BlockSpec((2 * tb, N_PICKS), lambda i: (i, 0)),
            pl.BlockSpec((N_PICKS, rows), lambda i: (0, 0)),
            pl.BlockSpec((PICK_GROUP, PICK_GROUP * D_TILES), lambda i: (0, 0)),
            pl.BlockSpec(memory_space=pl.ANY),
        ],
        out_specs=pl.BlockSpec((2 * tb, D_TILES, LANES), lambda i: (i, 0, 0)),
        scratch_shapes=[pltpu.VMEM((tb * N_PICKS, 2 * D_TILES, LANES), BF16),
                        pltpu.VMEM((tb * N_PICKS, 2 * D_TILES, LANES), BF16),
                        pltpu.SemaphoreType.DMA((2, tb // WAIT_GROUP))],
        compiler_params=_params("arbitrary"),
        name="peer_mix",
    )(idx, idx, x3, gates, expand, sel, table)
    return out.reshape(n, D_MODEL)


def _pack_table_kernel(u_ref, v_ref, o_ref):
    o_ref[:, 0:D_TILES, :] = u_ref[...].astype(BF16)
    o_ref[:, D_TILES:2 * D_TILES, :] = v_ref[...].astype(BF16)


def _pack_table(u3, v3, te=256):
    e = u3.shape[0]
    te = min(te, e)
    blk = pl.BlockSpec((te, D_TILES, LANES), lambda i: (i, 0, 0))
    return pl.pallas_call(
        _pack_table_kernel,
        out_shape=jax.ShapeDtypeStruct((e, 2 * D_TILES, LANES), BF16),
        grid=(e // te,),
        in_specs=[blk, blk],
        out_specs=pl.BlockSpec((te, 2 * D_TILES, LANES), lambda i: (i, 0, 0)),
        compiler_params=_params("parallel"),
        name="pack_table",
    )(u3, v3)


def _ple_kernel(x1_ref, po_ref, ple_ref, g_ref, wp_ref, wg_ref, y_ref):
    x2 = x1_ref[...] + po_ref[...]
    n3 = _rmsnorm_rows(x2, g_ref[...]).astype(BF16)
    emb = _dot(ple_ref[...].astype(BF16), wp_ref[...])
    y_ref[...] = x2 + emb * jax.nn.sigmoid(_dot(n3, wg_ref[...]))


def _ple(x1, po, ple, g, wp, wg, tm=512):
    n = x1.shape[0]
    tm = min(tm, n)
    row = pl.BlockSpec((tm, D_MODEL), lambda i: (i, 0))
    return pl.pallas_call(
        _ple_kernel,
        out_shape=jax.ShapeDtypeStruct((n, D_MODEL), F32),
        grid=(n // tm,),
        in_specs=[row, row, pl.BlockSpec((tm, D_PLE), lambda i: (i, 0)),
                  pl.BlockSpec((1, D_MODEL), lambda i: (0, 0)),
                  pl.BlockSpec((D_PLE, D_MODEL), lambda i: (0, 0)),
                  pl.BlockSpec((D_MODEL, D_MODEL), lambda i: (0, 0))],
        out_specs=row,
        compiler_params=_params("parallel"),
        name="ple",
    )(x1, po, ple, g, wp, wg)


def _rope_tables(pos):
    half = ROT_DIM // 2
    inv = ROPE_THETA ** (-jnp.arange(0, ROT_DIM, 2, dtype=F32) / ROT_DIM)
    ang = pos.astype(F32)[:, None] * inv[None, :]
    cos, sin = jnp.cos(ang), jnp.sin(ang)
    n = pos.shape[0]
    pad = jnp.zeros((n, HEAD_DIM - ROT_DIM), F32)
    zh = jnp.zeros((n, half), F32)
    cos_h = jnp.concatenate([cos, cos, pad + 1.0], axis=1)
    up_h = jnp.concatenate([-sin, zh, pad], axis=1)
    dn_h = jnp.concatenate([zh, sin, pad], axis=1)
    rep = LANES // HEAD_DIM
    return tuple(jnp.tile(a, (1, rep)) for a in (cos_h, up_h, dn_h))


def _block_diag(w):
    eye = jnp.eye(RNN_BLOCKS, dtype=w.dtype)
    return jnp.einsum("ncd,nm->ncmd", w, eye).reshape(D_RNN, D_RNN)


def _head_perm():
    return np.array([j * GROUP + g for g in range(GROUP) for j in range(N_KV_HEADS)])


def _token_pipeline_tail(x, z, yr, o, w, ple):
    m = _merge(yr, o, z, w["proj_rnn"], w["proj_attn"])
    x1, n2 = _out_proj(x, m, w["out"], w["norm_ffn"])
    idx, gate = _peer_route(n2, w["peer_q"], w["sub_keys"])
    po = _peer_mix(idx, gate, n2.reshape(-1, D_TILES, LANES), w["peer_table"], w["expand"], w["pick_sum"])
    return _ple(x1, po, ple, w["norm_ple"], w["ple"], w["ple_gate"])


def kernel(x_prompt, x_sample, p_prompt, p_sample, state_conv, state_rglru, cache_k, cache_v, norm_mix, w_in, conv_w, conv_b, w_rgate, b_rgate, w_igate, b_igate, lru_lambda, w_proj_rnn, q_norm, k_norm, attn_sinks, w_proj_attn, w_out, norm_ffn, w_peer_q, peer_sub_keys, peer_u, peer_v, w_ple, norm_ple, w_ple_gate):
    depth = w_in.shape[0]
    assert depth == 1
    l = 0
    B, S, _ = x_prompt.shape
    DB, DS, _ = x_sample.shape
    wbuf = cache_k.shape[2]

    hp = _head_perm()
    offs = np.cumsum([0, D_RNN, D_RNN, D_Q, D_KV, D_KV, D_MODEL, D_MODEL])
    xr_c, gr_c, q_c, k_c, v_c, ga_c, gb_c = [np.arange(offs[i], offs[i + 1]) for i in range(7)]
    cols = np.concatenate([ga_c, gb_c, xr_c, gr_c, q_c, k_c, v_c])
    row2 = lambda a: a[l].reshape(1, -1)
    w = {
        "proj_rnn": w_proj_rnn[l].astype(BF16),
        "proj_attn": w_proj_attn[l].reshape(N_Q_HEADS, HEAD_DIM, D_MODEL)[hp].reshape(D_Q, D_MODEL).astype(BF16),
        "out": w_out[l].astype(BF16),
        "norm_ffn": row2(norm_ffn),
        "peer_q": w_peer_q[l].astype(BF16),
        "sub_keys": peer_sub_keys[l].reshape(2 * PEER_HEADS, N_KEYS, D_HALF_KEY).astype(BF16),
        "peer_table": _pack_table(peer_u[l].reshape(N_EXPERTS, D_TILES, LANES),
                                  peer_v[l].reshape(N_EXPERTS, D_TILES, LANES)),
        "norm_ple": row2(norm_ple),
        "ple": w_ple[l].astype(BF16),
        "ple_gate": w_ple_gate[l].astype(BF16),
    }
    expand = np.repeat(np.eye(N_PICKS, dtype=np.float32), D_TILES, axis=1)
    w["expand"] = jnp.asarray(expand, BF16)
    w["pick_sum"] = jnp.asarray(expand[:PICK_GROUP, :PICK_GROUP * D_TILES], BF16)
    runs = np.split(cols, np.flatnonzero(np.diff(cols) != 1) + 1)
    w_in_b = jnp.concatenate([w_in[l][:, r[0]:r[-1] + 1] for r in runs], axis=1).astype(BF16)
    g_mix = row2(norm_mix)
    cw, cb = conv_w[l], row2(conv_b)
    wr, br = _block_diag(w_rgate[l]).astype(BF16), row2(b_rgate)
    wi, bi = _block_diag(w_igate[l]).astype(BF16), row2(b_igate)
    lam = row2(lru_lambda)
    rep = LANES // HEAD_DIM
    qg = jnp.tile(q_norm[l], rep).reshape(1, LANES)
    kg = jnp.tile(k_norm[l], rep).reshape(1, LANES)
    seg_ones = jnp.asarray(
        np.kron(np.eye(LANES // HEAD_DIM, dtype=np.float32), np.ones((HEAD_DIM, HEAD_DIM), np.float32)), BF16)
    sinks = attn_sinks[l]

    xp = x_prompt.reshape(B * S, D_MODEL)
    zp = _in_proj(xp, g_mix, w_in_b)
    yr_p, h_p = _rnn_prompt(zp, B, S, cw, cb, wr, br, wi, bi, lam)
    tabs_p = _rope_tables(jnp.arange(S, dtype=jnp.int32))
    q_p, kf_p, kb_p, vb_p = _qk_prep(zp, tabs_p, qg, kg, seg_ones)
    o_p = _attn_prompt(sinks, q_p, kb_p, vb_p, B, S)
    y_p = _token_pipeline_tail(xp, zp, yr_p, o_p, w, p_prompt[l].reshape(B * S, D_PLE))

    zp3 = zp.reshape(B, S, D_IN)
    keep = min(WINDOW, S)
    prompt_conv = zp3[:, S - (CONV_W - 1):, OFF_XR:OFF_XR + D_RNN]
    prompt_k = kf_p.reshape(B, S, D_KV)[:, S - keep:].reshape(B, keep, N_KV_HEADS, HEAD_DIM)
    prompt_v = zp3[:, S - keep:, OFF_V:OFF_V + D_KV].reshape(B, keep, N_KV_HEADS, HEAD_DIM)

    ns = DB * DS
    xs = x_sample.reshape(ns, D_MODEL)
    zs = _in_proj(xs, g_mix, w_in_b)
    zs3 = zs.reshape(DB, DS, D_IN)
    yr_s, h_s = _rnn_sample(zs.reshape(DB, DS * D_IN), state_conv[l].reshape(DB, (CONV_W - 1) * D_RNN),
                            state_rglru[l], cw, cb, wr, br, wi, bi, lam)
    yr_s = yr_s.reshape(ns, D_RNN)
    pos_s = PAST_LEN + jnp.arange(DS, dtype=jnp.int32)
    tabs_s = tuple(jnp.tile(a, (DB, 1)) for a in _rope_tables(pos_s))
    q_s, kf_s, kb_s, vb_s = _qk_prep(zs, tabs_s, qg, kg, seg_ones)
    tq = BF16_TILE_ROWS
    pad_t = lambda a, rows: jnp.pad(a.reshape(DB, DS, -1), ((0, 0), (0, rows - DS), (0, 0)))
    ck = cache_k[l].reshape(DB, wbuf, D_KV)
    cv = cache_v[l].reshape(DB, wbuf, D_KV)
    o_s, sample_k, sample_v = _attn_sample(
        sinks, pad_t(q_s, tq), pad_t(kb_s, tq), pad_t(vb_s, tq), pad_t(kf_s, SUBLANES),
        pad_t(zs3[:, :, OFF_V:OFF_V + D_KV], SUBLANES), ck, cv, DS)
    o_s = o_s[:, :DS].reshape(ns, D_Q)
    y_s = _token_pipeline_tail(xs, zs, yr_s, o_s, w, p_sample[l].reshape(ns, D_PLE))

    sample_conv = jnp.concatenate([state_conv[l], zs3[:, :, OFF_XR:OFF_XR + D_RNN]], axis=1)[:, DS:]
    sample_k = sample_k.reshape(DB, wbuf, N_KV_HEADS, HEAD_DIM)
    sample_v = sample_v.reshape(DB, wbuf, N_KV_HEADS, HEAD_DIM)

    return (y_p.reshape(B, S, D_MODEL), y_s.reshape(DB, DS, D_MODEL),
            prompt_conv[None], h_p.reshape(1, B, D_RNN), prompt_k[None], prompt_v[None],
            sample_conv[None], h_s[None], sample_k[None], sample_v[None])
```

```python
import functools

import jax
import jax.numpy as jnp
import numpy as np
from jax import lax
from jax.experimental import pallas as pl
from jax.experimental.pallas import tpu as pltpu

D_MODEL = 2048
D_RNN = D_MODEL // 2
RNN_BLOCKS = 8
RNN_BLOCK = D_RNN // RNN_BLOCKS
CONV_W = 4
LRU_C = 8.0
HEAD_DIM = 64
N_Q_HEADS = D_MODEL // 2 // HEAD_DIM
N_KV_HEADS = 4
GROUP = N_Q_HEADS // N_KV_HEADS
D_Q = N_Q_HEADS * HEAD_DIM
D_KV = N_KV_HEADS * HEAD_DIM
WINDOW = 128
ROT_DIM = HEAD_DIM // 4
ROPE_THETA = 500000.0
N_KEYS = 128
N_EXPERTS = N_KEYS * N_KEYS
PEER_HEADS = 8
PEER_TOPK = 16
D_KEY = 256
D_HALF_KEY = D_KEY // 2
N_PICKS = PEER_HEADS * PEER_TOPK
D_PLE = 256
EPS = 1e-6
NEG_INF = -1e30
PAST_LEN = 16384

LANES = 128
SUBLANES = 8
BF16_TILE_ROWS = 16
VMEM_LIMIT_BYTES = 56 * 1024 * 1024

OFF_GA = 0
OFF_GB = OFF_GA + D_MODEL
OFF_XR = OFF_GB + D_MODEL
OFF_GR = OFF_XR + D_RNN
OFF_Q = OFF_GR + D_RNN
OFF_K = OFF_Q + D_Q
OFF_V = OFF_K + D_KV
D_IN = OFF_V + D_KV

BF16 = jnp.bfloat16
F32 = jnp.float32


def _params(*sem):
    return pltpu.CompilerParams(dimension_semantics=sem, vmem_limit_bytes=VMEM_LIMIT_BYTES)


def _rmsnorm_rows(x, g):
    return x * lax.rsqrt(jnp.mean(x * x, axis=-1, keepdims=True) + EPS) * g


def _dot(a, b):
    return jnp.dot(a, b, preferred_element_type=F32)


def _dot_nt(a, b):
    return lax.dot_general(a, b, (((1,), (1,)), ((), ())), preferred_element_type=F32)


def _in_proj_kernel(x_ref, g_ref, w_ref, z_ref, xn_ref):
    @pl.when(pl.program_id(1) == 0)
    def _():
        xn_ref[...] = _rmsnorm_rows(x_ref[...], g_ref[...]).astype(BF16)

    z_ref[...] = _dot(xn_ref[...], w_ref[...])


def _in_proj(x, g, w, tm=1024, tn=1536):
    n = x.shape[0]
    tm = min(tm, n)
    return pl.pallas_call(
        _in_proj_kernel,
        out_shape=jax.ShapeDtypeStruct((n, D_IN), F32),
        grid=(n // tm, D_IN // tn),
        in_specs=[
            pl.BlockSpec((tm, D_MODEL), lambda i, j: (i, 0)),
            pl.BlockSpec((1, D_MODEL), lambda i, j: (0, 0)),
            pl.BlockSpec((D_MODEL, tn), lambda i, j: (0, j)),
        ],
        out_specs=pl.BlockSpec((tm, tn), lambda i, j: (i, j)),
        scratch_shapes=[pltpu.VMEM((tm, D_MODEL), BF16)],
        compiler_params=_params("parallel", "arbitrary"),
        name="in_proj",
    )(x, g, w)


def _softplus(x):
    return jnp.maximum(x, 0.0) + jnp.log(1.0 + jnp.exp(-jnp.abs(x)))


def _neg_expm1(x):
    t = jnp.tanh(-0.5 * x)
    return 2.0 * t / (1.0 + t)


def _lru_coeffs(xc, wr, br, wi, bi, lam, first_is_pos0):
    xb = xc.astype(BF16)
    r = jax.nn.sigmoid(_dot(xb, wr) + br)
    i = jax.nn.sigmoid(_dot(xb, wi) + bi)
    log_a = -LRU_C * r * _softplus(-lam)
    a = jnp.exp(log_a)
    mult = jnp.sqrt(_neg_expm1(2.0 * log_a))
    if first_is_pos0 is not None:
        row = lax.broadcasted_iota(jnp.int32, xc.shape, 0)
        mult = jnp.where(jnp.logical_and(first_is_pos0, row == 0), 1.0, mult)
    return a, mult * i * xc


def _rnn_prompt_kernel(xr_ref, gr_ref, cw_ref, cb_ref, wr_ref, br_ref, wi_ref, bi_ref, lam_ref,
                       yr_ref, hlast_ref, tail_ref, h_ref, *, tb):
    t = pl.program_id(1)

    @pl.when(t == 0)
    def _():
        tail_ref[...] = jnp.zeros_like(tail_ref)
        h_ref[...] = jnp.zeros_like(h_ref)

    x = xr_ref[...]
    tail = tail_ref[...]
    row8 = lax.broadcasted_iota(jnp.int32, (SUBLANES, D_RNN), 0)
    cw = cw_ref[...]
    xc = cb_ref[...] + cw[CONV_W - 1:CONV_W, :] * x
    for k in range(1, CONV_W):
        xs = pltpu.roll(x, k, axis=0)
        head = jnp.where(row8 < k, pltpu.roll(tail, k, axis=0), xs[:SUBLANES])
        xs = jnp.concatenate([head, xs[SUBLANES:]], axis=0)
        xc = xc + cw[CONV_W - 1 - k:CONV_W - k, :] * xs
    tail_ref[...] = x[tb - SUBLANES:, :]

    a, b = _lru_coeffs(xc, wr_ref[...], br_ref[...], wi_ref[...], bi_ref[...], lam_ref[...], t == 0)

    ng = tb // SUBLANES
    a3 = a.reshape(ng, SUBLANES, D_RNN)
    b3 = b.reshape(ng, SUBLANES, D_RNN)
    sub = lax.broadcasted_iota(jnp.int32, (ng, SUBLANES, D_RNN), 1)
    d = 1
    while d < SUBLANES:
        a_sh = pltpu.roll(a3, d, axis=1)
        b_sh = pltpu.roll(b3, d, axis=1)
        keep = sub < d
        b3 = jnp.where(keep, b3, a3 * b_sh + b3)
        a3 = jnp.where(keep, a3, a3 * a_sh)
        d *= 2
    carry = h_ref[...]
    hs = []
    for g in range(ng):
        hs.append(b3[g] + a3[g] * carry)
        carry = hs[-1][SUBLANES - 1:SUBLANES, :]
    h_ref[...] = carry
    hlast_ref[0] = carry
    yr_ref[...] = (jnp.concatenate(hs, axis=0) * jax.nn.gelu(gr_ref[...])).astype(BF16)


def _rnn_prompt(z, batch, seq, cw, cb, wr, br, wi, bi, lam, tb=256):
    nt = seq // tb
    vec = pl.BlockSpec((1, D_RNN), lambda b, t: (0, 0))
    mat = pl.BlockSpec((D_RNN, D_RNN), lambda b, t: (0, 0))
    return pl.pallas_call(
        functools.partial(_rnn_prompt_kernel, tb=tb),
        out_shape=(jax.ShapeDtypeStruct((batch * seq, D_RNN), BF16),
                   jax.ShapeDtypeStruct((batch, 1, D_RNN), F32)),
        grid=(batch, nt),
        in_specs=[
            pl.BlockSpec((tb, D_RNN), lambda b, t: (b * nt + t, OFF_XR // D_RNN)),
            pl.BlockSpec((tb, D_RNN), lambda b, t: (b * nt + t, OFF_GR // D_RNN)),
            pl.BlockSpec((CONV_W, D_RNN), lambda b, t: (0, 0)),
            vec, mat, vec, mat, vec, vec,
        ],
        out_specs=(pl.BlockSpec((tb, D_RNN), lambda b, t: (b * nt + t, 0)),
                   pl.BlockSpec((1, 1, D_RNN), lambda b, t: (b, 0, 0))),
        scratch_shapes=[pltpu.VMEM((SUBLANES, D_RNN), F32), pltpu.VMEM((1, D_RNN), F32)],
        compiler_params=_params("parallel", "arbitrary"),
        name="rnn_prompt",
    )(z, z, cw, cb, wr, br, wi, bi, lam)


def _rnn_sample_kernel(z_ref, buf_ref, h0_ref, cw_ref, cb_ref, wr_ref, br_ref, wi_ref,
                       bi_ref, lam_ref, yr_ref, hlast_ref, *, steps):
    col = lambda s, off: z_ref[:, s * D_IN + off:s * D_IN + off + D_RNN]
    cw = cw_ref[...]
    xp = [buf_ref[:, k * D_RNN:(k + 1) * D_RNN] for k in range(CONV_W - 1)] + [col(s, OFF_XR) for s in range(steps)]
    h = h0_ref[...]
    for s in range(steps):
        xc = cb_ref[...] + sum(cw[k:k + 1, :] * xp[s + k] for k in range(CONV_W))
        a, b = _lru_coeffs(xc, wr_ref[...], br_ref[...], wi_ref[...], bi_ref[...], lam_ref[...], None)
        h = a * h + b
        yr_ref[:, s * D_RNN:(s + 1) * D_RNN] = (h * jax.nn.gelu(col(s, OFF_GR))).astype(BF16)
    hlast_ref[...] = h


def _rnn_sample(z2, buf2, h0, cw, cb, wr, br, wi, bi, lam):
    db = z2.shape[0]
    steps = z2.shape[1] // D_IN
    return pl.pallas_call(
        functools.partial(_rnn_sample_kernel, steps=steps),
        out_shape=(jax.ShapeDtypeStruct((db, steps * D_RNN), BF16),
                   jax.ShapeDtypeStruct((db, D_RNN), F32)),
        compiler_params=pltpu.CompilerParams(vmem_limit_bytes=VMEM_LIMIT_BYTES),
        name="rnn_sample",
    )(z2, buf2, h0, cw, cb, wr, br, wi, bi, lam)


def _headnorm_rope(x, gain, seg_ones, cos_t, sin_up_t, sin_dn_t, scale):
    w = x.shape[1]
    outs = []
    for c in range(w // LANES):
        xc = x[:, c * LANES:(c + 1) * LANES]
        sq = xc * xc
        hi = sq.astype(BF16)
        lo = (sq - hi.astype(F32)).astype(BF16)
        ms = (_dot(hi, seg_ones) + _dot(lo, seg_ones)) * (1.0 / HEAD_DIM)
        xt = xc * lax.rsqrt(ms + EPS) * gain
        up = pltpu.roll(xt, LANES - ROT_DIM // 2, axis=1)
        dn = pltpu.roll(xt, ROT_DIM // 2, axis=1)
        outs.append((xt * cos_t + up * sin_up_t + dn * sin_dn_t) * scale)
    return jnp.concatenate(outs, axis=1)


def _qk_prep_kernel(q_ref, k_ref, v_ref, cos_ref, sup_ref, sdn_ref, qg_ref, kg_ref, seg_ref,
                    qo_ref, kf_ref, kb_ref, vb_ref):
    cos_t, sup, sdn = cos_ref[...], sup_ref[...], sdn_ref[...]
    seg = seg_ref[...]
    q = _headnorm_rope(q_ref[...], qg_ref[...], seg, cos_t, sup, sdn, HEAD_DIM ** -0.5)
    q = jnp.concatenate([q[:, h * HEAD_DIM:(h + 1) * HEAD_DIM] for h in _head_perm()], axis=1)
    qo_ref[...] = q.astype(BF16)
    k = _headnorm_rope(k_ref[...], kg_ref[...], seg, cos_t, sup, sdn, 1.0)
    kf_ref[...] = k
    kb_ref[...] = k.astype(BF16)
    vb_ref[...] = v_ref[...].astype(BF16)


def _qk_prep(z, tabs, qg, kg, seg_ones, tm=512):
    n = z.shape[0]
    tm = min(tm, n)
    cos_t, sup_t, sdn_t = tabs
    ntab = cos_t.shape[0] // tm
    tab = pl.BlockSpec((tm, LANES), lambda i: (i % ntab, 0))
    const = lambda shape: pl.BlockSpec(shape, lambda i: (0, 0))
    return pl.pallas_call(
        _qk_prep_kernel,
        out_shape=(jax.ShapeDtypeStruct((n, D_Q), BF16), jax.ShapeDtypeStruct((n, D_KV), F32),
                   jax.ShapeDtypeStruct((n, D_KV), BF16), jax.ShapeDtypeStruct((n, D_KV), BF16)),
        grid=(n // tm,),
        in_specs=[
            pl.BlockSpec((tm, D_Q), lambda i: (i, OFF_Q // D_Q)),
            pl.BlockSpec((tm, D_KV), lambda i: (i, OFF_K // D_KV)),
            pl.BlockSpec((tm, D_KV), lambda i: (i, OFF_V // D_KV)),
            tab, tab, tab,
            const((1, LANES)), const((1, LANES)), const((LANES, LANES)),
        ],
        out_specs=(pl.BlockSpec((tm, D_Q), lambda i: (i, 0)), pl.BlockSpec((tm, D_KV), lambda i: (i, 0)),
                   pl.BlockSpec((tm, D_KV), lambda i: (i, 0)), pl.BlockSpec((tm, D_KV), lambda i: (i, 0))),
        compiler_params=_params("parallel"),
        name="qk_prep",
    )(z, z, z, cos_t, sup_t, sdn_t, qg, kg, seg_ones)


def _sink_attention(q, k, v, valid, sink_ref, tq):
    lane_head = lax.broadcasted_iota(jnp.int32, (tq, D_KV), 1) // HEAD_DIM
    blocks = [(j, g) for j in range(N_KV_HEADS) for g in range(GROUP)]
    rowblk = lax.broadcasted_iota(jnp.int32, (N_Q_HEADS * tq, 1), 0) // tq
    keep = [jnp.where(lane_head == j, 1.0, 0.0).astype(BF16) for j in range(N_KV_HEADS)]
    qs = jnp.concatenate([q[:, g * D_KV:(g + 1) * D_KV] * keep[j] for j, g in blocks], axis=0)
    s = jnp.where(jnp.concatenate([valid] * N_Q_HEADS, axis=0), _dot_nt(qs, k), NEG_INF)
    sk = jnp.zeros((N_Q_HEADS * tq, 1), F32)
    for b, (j, g) in enumerate(blocks):
        sk = jnp.where(rowblk == b, sink_ref[j * GROUP + g], sk)
    m = jnp.maximum(jnp.max(s, axis=-1, keepdims=True), sk)
    p = jnp.exp(s - m)
    denom = jnp.sum(p, axis=-1, keepdims=True) + jnp.exp(sk - m)
    pv = _dot(p.astype(BF16), v) / denom
    out = [jnp.zeros((tq, D_KV), F32) for _ in range(GROUP)]
    for b, (j, g) in enumerate(blocks):
        out[g] = jnp.where(lane_head == j, pv[b * tq:(b + 1) * tq], out[g])
    return jnp.concatenate(out, axis=1)


def _attn_prompt_kernel(sink_ref, q_ref, kp_ref, kc_ref, vp_ref, vc_ref, o_ref):
    nb = pl.program_id(1)
    k = jnp.concatenate([kp_ref[...], kc_ref[...]], axis=0)
    v = jnp.concatenate([vp_ref[...], vc_ref[...]], axis=0)
    i = lax.broadcasted_iota(jnp.int32, (WINDOW, 2 * WINDOW), 0)
    j = lax.broadcasted_iota(jnp.int32, (WINDOW, 2 * WINDOW), 1)
    d = WINDOW + i - j
    valid = (d >= 0) & (d < WINDOW) & ((j >= WINDOW) | (nb > 0))
    o_ref[...] = _sink_attention(q_ref[...], k, v, valid, sink_ref, WINDOW).astype(BF16)


def _attn_prompt(sinks, q, kb, vb, batch, seq):
    nblk = seq // WINDOW
    cur = lambda w: pl.BlockSpec((WINDOW, w), lambda b, t: (b * nblk + t, 0))
    prev = lambda w: pl.BlockSpec((WINDOW, w), lambda b, t: (b * nblk + jnp.maximum(t - 1, 0), 0))
    return pl.pallas_call(
        _attn_prompt_kernel,
        out_shape=jax.ShapeDtypeStruct((batch * seq, D_Q), BF16),
        grid=(batch, nblk),
        in_specs=[pl.BlockSpec(memory_space=pltpu.SMEM), cur(D_Q), prev(D_KV), cur(D_KV),
                  prev(D_KV), cur(D_KV)],
        out_specs=cur(D_Q),
        compiler_params=_params("parallel", "arbitrary"),
        name="attn_prompt",
    )(sinks, q, kb, kb, vb, vb)


def _shift_in(cache, new8, steps):
    wbuf = cache.shape[0]
    rolled = pltpu.roll(cache, wbuf - steps, axis=0)
    row8 = lax.broadcasted_iota(jnp.int32, new8.shape, 0)
    tail = jnp.where(row8 >= SUBLANES - steps, pltpu.roll(new8, SUBLANES - steps, axis=0),
                     rolled[wbuf - SUBLANES:])
    return jnp.concatenate([rolled[:wbuf - SUBLANES], tail], axis=0)


def _attn_sample_kernel(sink_ref, q_ref, kn_ref, vn_ref, knf_ref, vnf_ref, ck_ref, cv_ref, o_ref, ok_ref, ov_ref,
                        *, bs, tq, wbuf, steps):
    t = lax.broadcasted_iota(jnp.int32, (tq, wbuf + tq), 0)
    c = lax.broadcasted_iota(jnp.int32, (tq, wbuf + tq), 1)
    d = wbuf + t - c
    valid = (d >= 0) & (d < WINDOW)
    for s in range(bs):
        ck, cv = ck_ref[s], cv_ref[s]
        k = jnp.concatenate([ck.astype(BF16), kn_ref[s]], axis=0)
        v = jnp.concatenate([cv.astype(BF16), vn_ref[s]], axis=0)
        o_ref[s] = _sink_attention(q_ref[s], k, v, valid, sink_ref, tq).astype(BF16)
        ok_ref[s] = _shift_in(ck, knf_ref[s], steps)
        ov_ref[s] = _shift_in(cv, vnf_ref[s], steps)


def _attn_sample(sinks, q3, kn3, vn3, knf3, vnf3, cache_k, cache_v, steps, bs=8):
    db, tq, _ = q3.shape
    wbuf = cache_k.shape[1]
    assert steps <= SUBLANES and knf3.shape[1] == SUBLANES and db % bs == 0
    blk = lambda r, w: pl.BlockSpec((bs, r, w), lambda b: (b, 0, 0))
    cache = jax.ShapeDtypeStruct((db, wbuf, D_KV), F32)
    return pl.pallas_call(
        functools.partial(_attn_sample_kernel, bs=bs, tq=tq, wbuf=wbuf, steps=steps),
        out_shape=(jax.ShapeDtypeStruct((db, tq, D_Q), BF16), cache, cache),
        grid=(db // bs,),
        in_specs=[pl.BlockSpec(memory_space=pltpu.SMEM), blk(tq, D_Q), blk(tq, D_KV), blk(tq, D_KV),
                  blk(SUBLANES, D_KV), blk(SUBLANES, D_KV), blk(wbuf, D_KV), blk(wbuf, D_KV)],
        out_specs=(blk(tq, D_Q), blk(wbuf, D_KV), blk(wbuf, D_KV)),
        compiler_params=_params("parallel"),
        name="attn_sample",
    )(sinks, q3, kn3, vn3, knf3, vnf3, cache_k, cache_v)


def _merge_out_kernel(x_ref, yr_ref, o_ref, ga_ref, gb_ref, wr_ref, wa_ref, wo_ref, g_ref, x1_ref, n2_ref):
    a = _dot(yr_ref[...], wr_ref[...])
    b = _dot(o_ref[...], wa_ref[...])
    merged = (jax.nn.sigmoid(ga_ref[...]) * a + jax.nn.sigmoid(gb_ref[...]) * b).astype(BF16)
    x1 = x_ref[...] + _dot(merged, wo_ref[...])
    x1_ref[...] = x1
    n2_ref[...] = _rmsnorm_rows(x1, g_ref[...]).astype(BF16)


def _merge_out(x, yr, o, z, wr, wa, wo, g, tm=256):
    n = x.shape[0]
    tm = min(tm, n)
    row = lambda w, c=0: pl.BlockSpec((tm, w), lambda i: (i, c))
    const = lambda r, c: pl.BlockSpec((r, c), lambda i: (0, 0), pipeline_mode=pl.Buffered(1))
    return pl.pallas_call(
        _merge_out_kernel,
        out_shape=(jax.ShapeDtypeStruct((n, D_MODEL), F32), jax.ShapeDtypeStruct((n, D_MODEL), BF16)),
        grid=(n // tm,),
        in_specs=[row(D_MODEL), row(D_RNN), row(D_Q), row(D_MODEL, OFF_GA // D_MODEL), row(D_MODEL, OFF_GB // D_MODEL),
                  const(D_RNN, D_MODEL), const(D_Q, D_MODEL), const(D_MODEL, D_MODEL), const(1, D_MODEL)],
        out_specs=(row(D_MODEL), row(D_MODEL)),
        compiler_params=_params("parallel"),
        name="merge_out",
    )(x, yr, o, z, z, wr, wa, wo, g)


def _topk_slabs(slabs, keys, k):
    t = slabs[0].shape[1]
    slot = lax.broadcasted_iota(jnp.int32, (k, t), 0)
    vals = jnp.zeros((k, t), F32)
    ids = jnp.zeros((k, t), F32)
    for r in range(k):
        best, bkey = slabs[0], keys[0]
        for sl, ky in zip(slabs[1:], keys[1:]):
            take = sl > best
            best = jnp.where(take, sl, best)
            bkey = jnp.where(take, ky, bkey)
        m = jnp.max(best, axis=0, keepdims=True)
        i = jnp.min(jnp.where(best == m, bkey, jnp.inf), axis=0, keepdims=True)
        vals = jnp.where(slot == r, m, vals)
        ids = jnp.where(slot == r, i, ids)
        slabs = [jnp.where(ky == i, -jnp.inf, sl) for sl, ky in zip(slabs, keys)]
    return vals, ids


def _peer_route_kernel(n2_ref, wq_ref, sk_ref, idx_ref, gate_ref):
    q = _dot(n2_ref[...], wq_ref[...]).astype(BF16)
    gates, experts = [], []
    tcol = q.shape[0]
    row8 = lax.broadcasted_iota(jnp.int32, (SUBLANES, tcol), 0).astype(F32)
    key_slabs = [row8 + float(a) for a in range(0, N_KEYS, SUBLANES)]
    assert PEER_TOPK == 2 * SUBLANES
    for h in range(PEER_HEADS):
        top = []
        for p in range(2):
            c = (2 * h + p) * D_HALF_KEY
            s = _dot_nt(sk_ref[2 * h + p], q[:, c:c + D_HALF_KEY])
            top.append(_topk_slabs([s[a:a + SUBLANES] for a in range(0, N_KEYS, SUBLANES)], key_slabs, PEER_TOPK))
        (s1, i1), (s2, i2) = top
        hk = SUBLANES
        lo, hi = slice(0, hk), slice(hk, 2 * hk)
        one = lambda a: slice(a, a + 1)
        groups = [(one(0), lo, row8), (one(0), hi, row8 + float(hk))]
        groups += [(one(a), lo, row8 + float(a * PEER_TOPK)) for a in range(1, hk)]
        groups += [(hi, one(0), (row8 + float(hk)) * float(PEER_TOPK))]
        cand = [s1[ra] + s2[rb] for ra, rb, _ in groups]
        cidx = [i1[ra] * float(N_KEYS) + i2[rb] for ra, rb, _ in groups]
        rid = [key for _, _, key in groups]
        top_s, pos = _topk_slabs(cand, rid, PEER_TOPK)
        slot = lax.broadcasted_iota(jnp.int32, top_s.shape, 0)
        ids = jnp.zeros(top_s.shape, F32)
        for r in range(PEER_TOPK):
            hit = [jnp.where(ky == pos[r:r + 1, :], ci, -1.0) for ky, ci in zip(rid, cidx)]
            e = jnp.max(functools.reduce(jnp.maximum, hit), axis=0, keepdims=True)
            ids = jnp.where(slot == r, e, ids)
        w = jnp.exp(top_s - top_s[0:1, :])
        gates.append(w / jnp.sum(w, axis=0, keepdims=True))
        experts.append(ids)
    gate_ref[...] = jnp.concatenate(gates, axis=0).T
    idx_ref[...] = jnp.concatenate(experts, axis=0).T.astype(jnp.int32)


def _peer_route(n2, wq, sk, tm=256):
    n = n2.shape[0]
    tm = min(tm, n)
    return pl.pallas_call(
        _peer_route_kernel,
        out_shape=(jax.ShapeDtypeStruct((n, N_PICKS), jnp.int32), jax.ShapeDtypeStruct((n, N_PICKS), F32)),
        grid=(n // tm,),
        in_specs=[pl.BlockSpec((tm, D_MODEL), lambda i: (i, 0)),
                  pl.BlockSpec((D_MODEL, PEER_HEADS * D_KEY), lambda i: (0, 0)),
                  pl.BlockSpec((2 * PEER_HEADS, N_KEYS, D_HALF_KEY), lambda i: (0, 0, 0))],
        out_specs=(pl.BlockSpec((tm, N_PICKS), lambda i: (i, 0)), pl.BlockSpec((tm, N_PICKS), lambda i: (i, 0))),
        compiler_params=_params("parallel"),
        name="peer_route",
    )(n2, wq, sk)


D_TILES = D_MODEL // LANES
assert D_TILES == BF16_TILE_ROWS
PICK_GROUP = 16
WAIT_GROUP = 4


def _peer_mix_kernel(idx_ref, idxn_ref, x_ref, gate_ref, exp_ref, sel_ref, tab_ref, o_ref, buf_a, buf_b,
                     sem_ref, *, tb, nsteps):
    i = pl.program_id(0)
    bufs = (buf_a, buf_b)

    group_rows = WAIT_GROUP * N_PICKS

    def start_token(ids_ref, row, half, t, picks=(0, N_PICKS)):
        for j in range(*picks):
            pltpu.make_async_copy(tab_ref.at[ids_ref[row, j]], bufs[half].at[t * N_PICKS + j],
                                  sem_ref.at[half, t // WAIT_GROUP]).start(priority=j % 2)

    def wait_group(half, g):
        pltpu.make_async_copy(tab_ref.at[pl.ds(0, group_rows)], bufs[half].at[pl.ds(g * group_rows, group_rows)],
                              sem_ref.at[half, g]).wait()

    @pl.when(i == 0)
    def _():
        for t in range(tb):
            start_token(idx_ref, t, 0, t)

    rows = N_PICKS * D_TILES
    diag = (lax.broadcasted_iota(jnp.int32, (D_TILES, rows), 1) % D_TILES
            == lax.broadcasted_iota(jnp.int32, (D_TILES, rows), 0))

    sel = sel_ref[...]
    lane = lax.broadcasted_iota(jnp.int32, (N_PICKS, LANES), 1)

    first_picks, second_picks = (0, N_PICKS // 2), (N_PICKS // 2, N_PICKS)

    def run_tile(half, start_other):
        buf = bufs[half]
        hcols = jnp.zeros((N_PICKS, LANES), F32)
        for t in range(tb):
            if t % WAIT_GROUP == 0:
                wait_group(half, t // WAIT_GROUP)
            start_other(t, first_picks)
            u3 = buf[t * N_PICKS:(t + 1) * N_PICKS, 0:D_TILES, :]
            p = (u3 * x_ref[half * tb + t][None, :, :]).reshape(rows, LANES)
            grp = PICK_GROUP * D_TILES
            r = jnp.concatenate([_dot(sel, p[c * grp:(c + 1) * grp]) for c in range(N_PICKS // PICK_GROUP)],
                                axis=0)
            hcols = jnp.where(lane == t, jnp.sum(r, axis=1, keepdims=True), hcols)
        h = hcols.T[0:tb, :]
        w = (jax.nn.gelu(h) * gate_ref[half * tb:(half + 1) * tb, :]).astype(BF16)
        wrow = _dot(w, exp_ref[...])
        for t in range(tb):
            start_other(t, second_picks)
            v = buf[t * N_PICKS:(t + 1) * N_PICKS, D_TILES:2 * D_TILES, :].reshape(rows, LANES)
            wm = jnp.where(diag, wrow[t:t + 1, :], 0.0).astype(BF16)
            o_ref[half * tb + t] = _dot(wm, v)

    run_tile(0, lambda t, picks: start_token(idx_ref, tb + t, 1, t, picks))
    run_tile(1, lambda t, picks: start_token(idxn_ref, t, 0, t, picks))

    @pl.when(i == nsteps - 1)
    def _():
        for g in range(tb // WAIT_GROUP):
            wait_group(0, g)


def _peer_mix(idx, gates, x3, table, expand, sel, tb=BF16_TILE_ROWS):
    n = idx.shape[0]
    nsteps = n // (2 * tb)
    rows = N_PICKS * D_TILES
    out = pl.pallas_call(
        functools.partial(_peer_mix_kernel, tb=tb, nsteps=nsteps),
        out_shape=jax.ShapeDtypeStruct((n, D_TILES, LANES), F32),
        grid=(nsteps,),
        in_specs=[
            pl.BlockSpec((2 * tb, N_PICKS), lambda i: (i, 0), memory_space=pltpu.SMEM),
            pl.BlockSpec((tb, N_PICKS), lambda i: (jnp.minimum(2 * i + 2, 2 * nsteps - 2), 0),
                         memory_space=pltpu.SMEM),
            pl.BlockSpec((2 * tb, D_TILES, LANES), lambda i: (i, 0, 0)),
            pl.BlockSpec((2 * tb, N_PICKS), lambda i: (i, 0)),
            pl.BlockSpec((N_PICKS, rows), lambda i: (0, 0)),
            pl.BlockSpec((PICK_GROUP, PICK_GROUP * D_TILES), lambda i: (0, 0)),
            pl.BlockSpec(memory_space=pl.ANY),
        ],
        out_specs=pl.BlockSpec((2 * tb, D_TILES, LANES), lambda i: (i, 0, 0)),
        scratch_shapes=[pltpu.VMEM((tb * N_PICKS, 2 * D_TILES, LANES), BF16),
                        pltpu.VMEM((tb * N_PICKS, 2 * D_TILES, LANES), BF16),
                        pltpu.SemaphoreType.DMA((2, tb // WAIT_GROUP))],
        compiler_params=_params("arbitrary"),
        name="peer_mix",
    )(idx, idx, x3, gates, expand, sel, table)
    return out.reshape(n, D_MODEL)


def _pack_table_kernel(u_ref, v_ref, o_ref):
    o_ref[:, 0:D_TILES, :] = u_ref[...].astype(BF16)
    o_ref[:, D_TILES:2 * D_TILES, :] = v_ref[...].astype(BF16)


def _pack_table(u3, v3, te=256):
    e = u3.shape[0]
    te = min(te, e)
    blk = pl.BlockSpec((te, D_TILES, LANES), lambda i: (i, 0, 0))
    return pl.pallas_call(
        _pack_table_kernel,
        out_shape=jax.ShapeDtypeStruct((e, 2 * D_TILES, LANES), BF16),
        grid=(e // te,),
        in_specs=[blk, blk],
        out_specs=pl.BlockSpec((te, 2 * D_TILES, LANES), lambda i: (i, 0, 0)),
        compiler_params=_params("parallel"),
        name="pack_table",
    )(u3, v3)


def _ple_kernel(x1_ref, po_ref, ple_ref, g_ref, wp_ref, wg_ref, y_ref):
    x2 = x1_ref[...] + po_ref[...]
    n3 = _rmsnorm_rows(x2, g_ref[...]).astype(BF16)
    emb = _dot(ple_ref[...].astype(BF16), wp_ref[...])
    y_ref[...] = x2 + emb * jax.nn.sigmoid(_dot(n3, wg_ref[...]))


def _ple(x1, po, ple, g, wp, wg, tm=512):
    n = x1.shape[0]
    tm = min(tm, n)
    row = pl.BlockSpec((tm, D_MODEL), lambda i: (i, 0))
    return pl.pallas_call(
        _ple_kernel,
        out_shape=jax.ShapeDtypeStruct((n, D_MODEL), F32),
        grid=(n // tm,),
        in_specs=[row, row, pl.BlockSpec((tm, D_PLE), lambda i: (i, 0)),
                  pl.BlockSpec((1, D_MODEL), lambda i: (0, 0)),
                  pl.BlockSpec((D_PLE, D_MODEL), lambda i: (0, 0)),
                  pl.BlockSpec((D_MODEL, D_MODEL), lambda i: (0, 0))],
        out_specs=row,
        compiler_params=_params("parallel"),
        name="ple",
    )(x1, po, ple, g, wp, wg)


def _rope_tables(pos):
    half = ROT_DIM // 2
    inv = ROPE_THETA ** (-jnp.arange(0, ROT_DIM, 2, dtype=F32) / ROT_DIM)
    ang = pos.astype(F32)[:, None] * inv[None, :]
    cos, sin = jnp.cos(ang), jnp.sin(ang)
    n = pos.shape[0]
    pad = jnp.zeros((n, HEAD_DIM - ROT_DIM), F32)
    zh = jnp.zeros((n, half), F32)
    cos_h = jnp.concatenate([cos, cos, pad + 1.0], axis=1)
    up_h = jnp.concatenate([-sin, zh, pad], axis=1)
    dn_h = jnp.concatenate([zh, sin, pad], axis=1)
    rep = LANES // HEAD_DIM
    return tuple(jnp.tile(a, (1, rep)) for a in (cos_h, up_h, dn_h))


def _block_diag(w):
    eye = jnp.eye(RNN_BLOCKS, dtype=w.dtype)
    return jnp.einsum("ncd,nm->ncmd", w, eye).reshape(D_RNN, D_RNN)


def _head_perm():
    return np.array([j * GROUP + g for g in range(GROUP) for j in range(N_KV_HEADS)])


def _token_pipeline_tail(x, z, yr, o, w, ple):
    x1, n2 = _merge_out(x, yr, o, z, w["proj_rnn"], w["proj_attn"], w["out"], w["norm_ffn"])
    idx, gate = _peer_route(n2, w["peer_q"], w["sub_keys"])
    po = _peer_mix(idx, gate, n2.reshape(-1, D_TILES, LANES), w["peer_table"], w["expand"], w["pick_sum"])
    return _ple(x1, po, ple, w["norm_ple"], w["ple"], w["ple_gate"])


def kernel(x_prompt, x_sample, p_prompt, p_sample, state_conv, state_rglru, cache_k, cache_v, norm_mix, w_in, conv_w, conv_b, w_rgate, b_rgate, w_igate, b_igate, lru_lambda, w_proj_rnn, q_norm, k_norm, attn_sinks, w_proj_attn, w_out, norm_ffn, w_peer_q, peer_sub_keys, peer_u, peer_v, w_ple, norm_ple, w_ple_gate):
    depth = w_in.shape[0]
    assert depth == 1
    l = 0
    B, S, _ = x_prompt.shape
    DB, DS, _ = x_sample.shape
    wbuf = cache_k.shape[2]

    hp = _head_perm()
    offs = np.cumsum([0, D_RNN, D_RNN, D_Q, D_KV, D_KV, D_MODEL, D_MODEL])
    xr_c, gr_c, q_c, k_c, v_c, ga_c, gb_c = [np.arange(offs[i], offs[i + 1]) for i in range(7)]
    cols = np.concatenate([ga_c, gb_c, xr_c, gr_c, q_c, k_c, v_c])
    row2 = lambda a: a[l].reshape(1, -1)
    w = {
        "proj_rnn": w_proj_rnn[l].astype(BF16),
        "proj_attn": w_proj_attn[l].reshape(N_Q_HEADS, HEAD_DIM, D_MODEL)[hp].reshape(D_Q, D_MODEL).astype(BF16),
        "out": w_out[l].astype(BF16),
        "norm_ffn": row2(norm_ffn),
        "peer_q": w_peer_q[l].astype(BF16),
        "sub_keys": peer_sub_keys[l].reshape(2 * PEER_HEADS, N_KEYS, D_HALF_KEY).astype(BF16),
        "peer_table": _pack_table(peer_u[l].reshape(N_EXPERTS, D_TILES, LANES),
                                  peer_v[l].reshape(N_EXPERTS, D_TILES, LANES)),
        "norm_ple": row2(norm_ple),
        "ple": w_ple[l].astype(BF16),
        "ple_gate": w_ple_gate[l].astype(BF16),
    }
    expand = np.repeat(np.eye(N_PICKS, dtype=np.float32), D_TILES, axis=1)
    w["expand"] = jnp.asarray(expand, BF16)
    w["pick_sum"] = jnp.asarray(expand[:PICK_GROUP, :PICK_GROUP * D_TILES], BF16)
    runs = np.split(cols, np.flatnonzero(np.diff(cols) != 1) + 1)
    w_in_b = jnp.concatenate([w_in[l][:, r[0]:r[-1] + 1] for r in runs], axis=1).astype(BF16)
    g_mix = row2(norm_mix)
    cw, cb = conv_w[l], row2(conv_b)
    wr, br = _block_diag(w_rgate[l]).astype(BF16), row2(b_rgate)
    wi, bi = _block_diag(w_igate[l]).astype(BF16), row2(b_igate)
    lam = row2(lru_lambda)
    rep = LANES // HEAD_DIM
    qg = jnp.tile(q_norm[l], rep).reshape(1, LANES)
    kg = jnp.tile(k_norm[l], rep).reshape(1, LANES)
    seg_ones = jnp.asarray(
        np.kron(np.eye(LANES // HEAD_DIM, dtype=np.float32), np.ones((HEAD_DIM, HEAD_DIM), np.float32)), BF16)
    sinks = attn_sinks[l]

    xp = x_prompt.reshape(B * S, D_MODEL)
    zp = _in_proj(xp, g_mix, w_in_b)
    yr_p, h_p = _rnn_prompt(zp, B, S, cw, cb, wr, br, wi, bi, lam)
    tabs_p = _rope_tables(jnp.arange(S, dtype=jnp.int32))
    q_p, kf_p, kb_p, vb_p = _qk_prep(zp, tabs_p, qg, kg, seg_ones)
    o_p = _attn_prompt(sinks, q_p, kb_p, vb_p, B, S)
    y_p = _token_pipeline_tail(xp, zp, yr_p, o_p, w, p_prompt[l].reshape(B * S, D_PLE))

    zp3 = zp.reshape(B, S, D_IN)
    keep = min(WINDOW, S)
    prompt_conv = zp3[:, S - (CONV_W - 1):, OFF_XR:OFF_XR + D_RNN]
    prompt_k = kf_p.reshape(B, S, D_KV)[:, S - keep:].reshape(B, keep, N_KV_HEADS, HEAD_DIM)
    prompt_v = zp3[:, S - keep:, OFF_V:OFF_V + D_KV].reshape(B, keep, N_KV_HEADS, HEAD_DIM)

    ns = DB * DS
    xs = x_sample.reshape(ns, D_MODEL)
    zs = _in_proj(xs, g_mix, w_in_b)
    zs3 = zs.reshape(DB, DS, D_IN)
    yr_s, h_s = _rnn_sample(zs.reshape(DB, DS * D_IN), state_conv[l].reshape(DB, (CONV_W - 1) * D_RNN),
                            state_rglru[l], cw, cb, wr, br, wi, bi, lam)
    yr_s = yr_s.reshape(ns, D_RNN)
    pos_s = PAST_LEN + jnp.arange(DS, dtype=jnp.int32)
    tabs_s = tuple(jnp.tile(a, (DB, 1)) for a in _rope_tables(pos_s))
    q_s, kf_s, kb_s, vb_s = _qk_prep(zs, tabs_s, qg, kg, seg_ones)
    tq = BF16_TILE_ROWS
    pad_t = lambda a, rows: jnp.pad(a.reshape(DB, DS, -1), ((0, 0), (0, rows - DS), (0, 0)))
    ck = cache_k[l].reshape(DB, wbuf, D_KV)
    cv = cache_v[l].reshape(DB, wbuf, D_KV)
    o_s, sample_k, sample_v = _attn_sample(
        sinks, pad_t(q_s, tq), pad_t(kb_s, tq), pad_t(vb_s, tq), pad_t(kf_s, SUBLANES),
        pad_t(zs3[:, :, OFF_V:OFF_V + D_KV], SUBLANES), ck, cv, DS)
    o_s = o_s[:, :DS].reshape(ns, D_Q)
    y_s = _token_pipeline_tail(xs, zs, yr_s, o_s, w, p_sample[l].reshape(ns, D_PLE))

    sample_conv = jnp.concatenate([state_conv[l], zs3[:, :, OFF_XR:OFF_XR + D_RNN]], axis=1)[:, DS:]
    sample_k = sample_k.reshape(DB, wbuf, N_KV_HEADS, HEAD_DIM)
    sample_v = sample_v.reshape(DB, wbuf, N_KV_HEADS, HEAD_DIM)

    return (y_p.reshape(B, S, D_MODEL), y_s.reshape(DB, DS, D_MODEL),
            prompt_conv[None], h_p.reshape(1, B, D_RNN), prompt_k[None], prompt_v[None],
            sample_conv[None], h_s[None], sample_k[None], sample_v[None])
```

```python
import functools

import jax
import jax.numpy as jnp
import numpy as np
from jax import lax
from jax.experimental import pallas as pl
from jax.experimental.pallas import tpu as pltpu

D_MODEL = 2048
D_RNN = D_MODEL // 2
RNN_BLOCKS = 8
RNN_BLOCK = D_RNN // RNN_BLOCKS
CONV_W = 4
LRU_C = 8.0
HEAD_DIM = 64
N_Q_HEADS = D_MODEL // 2 // HEAD_DIM
N_KV_HEADS = 4
GROUP = N_Q_HEADS // N_KV_HEADS
D_Q = N_Q_HEADS * HEAD_DIM
D_KV = N_KV_HEADS * HEAD_DIM
WINDOW = 128
ROT_DIM = HEAD_DIM // 4
ROPE_THETA = 500000.0
N_KEYS = 128
N_EXPERTS = N_KEYS * N_KEYS
PEER_HEADS = 8
PEER_TOPK = 16
D_KEY = 256
D_HALF_KEY = D_KEY // 2
N_PICKS = PEER_HEADS * PEER_TOPK
D_PLE = 256
EPS = 1e-6
NEG_INF = -1e30
PAST_LEN = 16384

LANES = 128
SUBLANES = 8
BF16_TILE_ROWS = 16
VMEM_LIMIT_BYTES = 56 * 1024 * 1024

OFF_GA = 0
OFF_GB = OFF_GA + D_MODEL
OFF_XR = OFF_GB + D_MODEL
OFF_GR = OFF_XR + D_RNN
OFF_Q = OFF_GR + D_RNN
OFF_K = OFF_Q + D_Q
OFF_V = OFF_K + D_KV
D_IN = OFF_V + D_KV

BF16 = jnp.bfloat16
F32 = jnp.float32


def _params(*sem):
    return pltpu.CompilerParams(dimension_semantics=sem, vmem_limit_bytes=VMEM_LIMIT_BYTES)


def _rmsnorm_rows(x, g):
    return x * lax.rsqrt(jnp.mean(x * x, axis=-1, keepdims=True) + EPS) * g


def _dot(a, b):
    return jnp.dot(a, b, preferred_element_type=F32)


def _dot_nt(a, b):
    return lax.dot_general(a, b, (((1,), (1,)), ((), ())), preferred_element_type=F32)


def _in_proj_kernel(x_ref, g_ref, w_ref, z_ref, xn_ref):
    @pl.when(pl.program_id(1) == 0)
    def _():
        xn_ref[...] = _rmsnorm_rows(x_ref[...], g_ref[...]).astype(BF16)

    z_ref[...] = _dot(xn_ref[...], w_ref[...])


def _in_proj(x, g, w, tm=1024, tn=1536):
    n = x.shape[0]
    tm = min(tm, n)
    return pl.pallas_call(
        _in_proj_kernel,
        out_shape=jax.ShapeDtypeStruct((n, D_IN), F32),
        grid=(n // tm, D_IN // tn),
        in_specs=[
            pl.BlockSpec((tm, D_MODEL), lambda i, j: (i, 0)),
            pl.BlockSpec((1, D_MODEL), lambda i, j: (0, 0)),
            pl.BlockSpec((D_MODEL, tn), lambda i, j: (0, j)),
        ],
        out_specs=pl.BlockSpec((tm, tn), lambda i, j: (i, j)),
        scratch_shapes=[pltpu.VMEM((tm, D_MODEL), BF16)],
        compiler_params=_params("parallel", "arbitrary"),
        name="in_proj",
    )(x, g, w)


def _softplus(x):
    return jnp.maximum(x, 0.0) + jnp.log(1.0 + jnp.exp(-jnp.abs(x)))


def _neg_expm1(x):
    t = jnp.tanh(-0.5 * x)
    return 2.0 * t / (1.0 + t)


def _lru_coeffs(xc, wr, br, wi, bi, lam, first_is_pos0):
    xb = xc.astype(BF16)
    r = jax.nn.sigmoid(_dot(xb, wr) + br)
    i = jax.nn.sigmoid(_dot(xb, wi) + bi)
    log_a = -LRU_C * r * _softplus(-lam)
    a = jnp.exp(log_a)
    mult = jnp.sqrt(_neg_expm1(2.0 * log_a))
    if first_is_pos0 is not None:
        row = lax.broadcasted_iota(jnp.int32, xc.shape, 0)
        mult = jnp.where(jnp.logical_and(first_is_pos0, row == 0), 1.0, mult)
    return a, mult * i * xc


def _rnn_prompt_kernel(xr_ref, gr_ref, cw_ref, cb_ref, wr_ref, br_ref, wi_ref, bi_ref, lam_ref,
                       yr_ref, hlast_ref, tail_ref, h_ref, *, tb):
    t = pl.program_id(1)

    @pl.when(t == 0)
    def _():
        tail_ref[...] = jnp.zeros_like(tail_ref)
        h_ref[...] = jnp.zeros_like(h_ref)

    x = xr_ref[...]
    tail = tail_ref[...]
    row8 = lax.broadcasted_iota(jnp.int32, (SUBLANES, D_RNN), 0)
    cw = cw_ref[...]
    xc = cb_ref[...] + cw[CONV_W - 1:CONV_W, :] * x
    for k in range(1, CONV_W):
        xs = pltpu.roll(x, k, axis=0)
        head = jnp.where(row8 < k, pltpu.roll(tail, k, axis=0), xs[:SUBLANES])
        xs = jnp.concatenate([head, xs[SUBLANES:]], axis=0)
        xc = xc + cw[CONV_W - 1 - k:CONV_W - k, :] * xs
    tail_ref[...] = x[tb - SUBLANES:, :]

    a, b = _lru_coeffs(xc, wr_ref[...], br_ref[...], wi_ref[...], bi_ref[...], lam_ref[...], t == 0)

    ng = tb // SUBLANES
    a3 = a.reshape(ng, SUBLANES, D_RNN)
    b3 = b.reshape(ng, SUBLANES, D_RNN)
    sub = lax.broadcasted_iota(jnp.int32, (ng, SUBLANES, D_RNN), 1)
    d = 1
    while d < SUBLANES:
        a_sh = pltpu.roll(a3, d, axis=1)
        b_sh = pltpu.roll(b3, d, axis=1)
        keep = sub < d
        b3 = jnp.where(keep, b3, a3 * b_sh + b3)
        a3 = jnp.where(keep, a3, a3 * a_sh)
        d *= 2
    carry = h_ref[...]
    hs = []
    for g in range(ng):
        hs.append(b3[g] + a3[g] * carry)
        carry = hs[-1][SUBLANES - 1:SUBLANES, :]
    h_ref[...] = carry
    hlast_ref[0] = carry
    yr_ref[...] = (jnp.concatenate(hs, axis=0) * jax.nn.gelu(gr_ref[...])).astype(BF16)


def _rnn_prompt(z, batch, seq, cw, cb, wr, br, wi, bi, lam, tb=256):
    nt = seq // tb
    vec = pl.BlockSpec((1, D_RNN), lambda b, t: (0, 0))
    mat = pl.BlockSpec((D_RNN, D_RNN), lambda b, t: (0, 0))
    return pl.pallas_call(
        functools.partial(_rnn_prompt_kernel, tb=tb),
        out_shape=(jax.ShapeDtypeStruct((batch * seq, D_RNN), BF16),
                   jax.ShapeDtypeStruct((batch, 1, D_RNN), F32)),
        grid=(batch, nt),
        in_specs=[
            pl.BlockSpec((tb, D_RNN), lambda b, t: (b * nt + t, OFF_XR // D_RNN)),
            pl.BlockSpec((tb, D_RNN), lambda b, t: (b * nt + t, OFF_GR // D_RNN)),
            pl.BlockSpec((CONV_W, D_RNN), lambda b, t: (0, 0)),
            vec, mat, vec, mat, vec, vec,
        ],
        out_specs=(pl.BlockSpec((tb, D_RNN), lambda b, t: (b * nt + t, 0)),
                   pl.BlockSpec((1, 1, D_RNN), lambda b, t: (b, 0, 0))),
        scratch_shapes=[pltpu.VMEM((SUBLANES, D_RNN), F32), pltpu.VMEM((1, D_RNN), F32)],
        compiler_params=_params("parallel", "arbitrary"),
        name="rnn_prompt",
    )(z, z, cw, cb, wr, br, wi, bi, lam)


def _rnn_sample_kernel(z_ref, buf_ref, h0_ref, cw_ref, cb_ref, wr_ref, br_ref, wi_ref,
                       bi_ref, lam_ref, yr_ref, hlast_ref, *, steps):
    col = lambda s, off: z_ref[:, s * D_IN + off:s * D_IN + off + D_RNN]
    cw = cw_ref[...]
    xp = [buf_ref[:, k * D_RNN:(k + 1) * D_RNN] for k in range(CONV_W - 1)] + [col(s, OFF_XR) for s in range(steps)]
    h = h0_ref[...]
    for s in range(steps):
        xc = cb_ref[...] + sum(cw[k:k + 1, :] * xp[s + k] for k in range(CONV_W))
        a, b = _lru_coeffs(xc, wr_ref[...], br_ref[...], wi_ref[...], bi_ref[...], lam_ref[...], None)
        h = a * h + b
        yr_ref[:, s * D_RNN:(s + 1) * D_RNN] = (h * jax.nn.gelu(col(s, OFF_GR))).astype(BF16)
    hlast_ref[...] = h


def _rnn_sample(z2, buf2, h0, cw, cb, wr, br, wi, bi, lam):
    db = z2.shape[0]
    steps = z2.shape[1] // D_IN
    return pl.pallas_call(
        functools.partial(_rnn_sample_kernel, steps=steps),
        out_shape=(jax.ShapeDtypeStruct((db, steps * D_RNN), BF16),
                   jax.ShapeDtypeStruct((db, D_RNN), F32)),
        compiler_params=pltpu.CompilerParams(vmem_limit_bytes=VMEM_LIMIT_BYTES),
        name="rnn_sample",
    )(z2, buf2, h0, cw, cb, wr, br, wi, bi, lam)


def _headnorm_rope(x, gain, seg_ones, cos_t, sin_up_t, sin_dn_t, scale):
    w = x.shape[1]
    outs = []
    for c in range(w // LANES):
        xc = x[:, c * LANES:(c + 1) * LANES]
        sq = xc * xc
        hi = sq.astype(BF16)
        lo = (sq - hi.astype(F32)).astype(BF16)
        ms = (_dot(hi, seg_ones) + _dot(lo, seg_ones)) * (1.0 / HEAD_DIM)
        xt = xc * lax.rsqrt(ms + EPS) * gain
        up = pltpu.roll(xt, LANES - ROT_DIM // 2, axis=1)
        dn = pltpu.roll(xt, ROT_DIM // 2, axis=1)
        outs.append((xt * cos_t + up * sin_up_t + dn * sin_dn_t) * scale)
    return jnp.concatenate(outs, axis=1)


def _qk_prep_kernel(q_ref, k_ref, v_ref, cos_ref, sup_ref, sdn_ref, qg_ref, kg_ref, seg_ref,
                    qo_ref, kf_ref, kb_ref, vb_ref):
    cos_t, sup, sdn = cos_ref[...], sup_ref[...], sdn_ref[...]
    seg = seg_ref[...]
    q = _headnorm_rope(q_ref[...], qg_ref[...], seg, cos_t, sup, sdn, HEAD_DIM ** -0.5)
    q = jnp.concatenate([q[:, h * HEAD_DIM:(h + 1) * HEAD_DIM] for h in _head_perm()], axis=1)
    qo_ref[...] = q.astype(BF16)
    k = _headnorm_rope(k_ref[...], kg_ref[...], seg, cos_t, sup, sdn, 1.0)
    kf_ref[...] = k
    kb_ref[...] = k.astype(BF16)
    vb_ref[...] = v_ref[...].astype(BF16)


def _qk_prep(z, tabs, qg, kg, seg_ones, tm=512):
    n = z.shape[0]
    tm = min(tm, n)
    cos_t, sup_t, sdn_t = tabs
    ntab = cos_t.shape[0] // tm
    tab = pl.BlockSpec((tm, LANES), lambda i: (i % ntab, 0))
    const = lambda shape: pl.BlockSpec(shape, lambda i: (0, 0))
    return pl.pallas_call(
        _qk_prep_kernel,
        out_shape=(jax.ShapeDtypeStruct((n, D_Q), BF16), jax.ShapeDtypeStruct((n, D_KV), F32),
                   jax.ShapeDtypeStruct((n, D_KV), BF16), jax.ShapeDtypeStruct((n, D_KV), BF16)),
        grid=(n // tm,),
        in_specs=[
            pl.BlockSpec((tm, D_Q), lambda i: (i, OFF_Q // D_Q)),
            pl.BlockSpec((tm, D_KV), lambda i: (i, OFF_K // D_KV)),
            pl.BlockSpec((tm, D_KV), lambda i: (i, OFF_V // D_KV)),
            tab, tab, tab,
            const((1, LANES)), const((1, LANES)), const((LANES, LANES)),
        ],
        out_specs=(pl.BlockSpec((tm, D_Q), lambda i: (i, 0)), pl.BlockSpec((tm, D_KV), lambda i: (i, 0)),
                   pl.BlockSpec((tm, D_KV), lambda i: (i, 0)), pl.BlockSpec((tm, D_KV), lambda i: (i, 0))),
        compiler_params=_params("parallel"),
        name="qk_prep",
    )(z, z, z, cos_t, sup_t, sdn_t, qg, kg, seg_ones)


def _sink_attention(q, k, v, valid, sink_ref, tq):
    lane_head = lax.broadcasted_iota(jnp.int32, (tq, D_KV), 1) // HEAD_DIM
    blocks = [(j, g) for j in range(N_KV_HEADS) for g in range(GROUP)]
    rowblk = lax.broadcasted_iota(jnp.int32, (N_Q_HEADS * tq, 1), 0) // tq
    keep = [jnp.where(lane_head == j, 1.0, 0.0).astype(BF16) for j in range(N_KV_HEADS)]
    qs = jnp.concatenate([q[:, g * D_KV:(g + 1) * D_KV] * keep[j] for j, g in blocks], axis=0)
    s = jnp.where(jnp.concatenate([valid] * N_Q_HEADS, axis=0), _dot_nt(qs, k), NEG_INF)
    sk = jnp.zeros((N_Q_HEADS * tq, 1), F32)
    for b, (j, g) in enumerate(blocks):
        sk = jnp.where(rowblk == b, sink_ref[j * GROUP + g], sk)
    m = jnp.maximum(jnp.max(s, axis=-1, keepdims=True), sk)
    p = jnp.exp(s - m)
    denom = jnp.sum(p, axis=-1, keepdims=True) + jnp.exp(sk - m)
    pv = _dot(p.astype(BF16), v) / denom
    out = [jnp.zeros((tq, D_KV), F32) for _ in range(GROUP)]
    for b, (j, g) in enumerate(blocks):
        out[g] = jnp.where(lane_head == j, pv[b * tq:(b + 1) * tq], out[g])
    return jnp.concatenate(out, axis=1)


def _attn_prompt_kernel(sink_ref, q_ref, kp_ref, kc_ref, vp_ref, vc_ref, o_ref):
    nb = pl.program_id(1)
    k = jnp.concatenate([kp_ref[...], kc_ref[...]], axis=0)
    v = jnp.concatenate([vp_ref[...], vc_ref[...]], axis=0)
    i = lax.broadcasted_iota(jnp.int32, (WINDOW, 2 * WINDOW), 0)
    j = lax.broadcasted_iota(jnp.int32, (WINDOW, 2 * WINDOW), 1)
    d = WINDOW + i - j
    valid = (d >= 0) & (d < WINDOW) & ((j >= WINDOW) | (nb > 0))
    o_ref[...] = _sink_attention(q_ref[...], k, v, valid, sink_ref, WINDOW).astype(BF16)


def _attn_prompt(sinks, q, kb, vb, batch, seq):
    nblk = seq // WINDOW
    cur = lambda w: pl.BlockSpec((WINDOW, w), lambda b, t: (b * nblk + t, 0))
    prev = lambda w: pl.BlockSpec((WINDOW, w), lambda b, t: (b * nblk + jnp.maximum(t - 1, 0), 0))
    return pl.pallas_call(
        _attn_prompt_kernel,
        out_shape=jax.ShapeDtypeStruct((batch * seq, D_Q), BF16),
        grid=(batch, nblk),
        in_specs=[pl.BlockSpec(memory_space=pltpu.SMEM), cur(D_Q), prev(D_KV), cur(D_KV),
                  prev(D_KV), cur(D_KV)],
        out_specs=cur(D_Q),
        compiler_params=_params("parallel", "arbitrary"),
        name="attn_prompt",
    )(sinks, q, kb, kb, vb, vb)


def _shift_in(cache, new8, steps):
    wbuf = cache.shape[0]
    rolled = pltpu.roll(cache, wbuf - steps, axis=0)
    row8 = lax.broadcasted_iota(jnp.int32, new8.shape, 0)
    tail = jnp.where(row8 >= SUBLANES - steps, pltpu.roll(new8, SUBLANES - steps, axis=0),
                     rolled[wbuf - SUBLANES:])
    return jnp.concatenate([rolled[:wbuf - SUBLANES], tail], axis=0)


def _attn_sample_kernel(sink_ref, q_ref, kn_ref, vn_ref, knf_ref, vnf_ref, ck_ref, cv_ref, o_ref, ok_ref, ov_ref,
                        *, bs, tq, wbuf, steps):
    t = lax.broadcasted_iota(jnp.int32, (tq, wbuf + tq), 0)
    c = lax.broadcasted_iota(jnp.int32, (tq, wbuf + tq), 1)
    d = wbuf + t - c
    valid = (d >= 0) & (d < WINDOW)
    for s in range(bs):
        ck, cv = ck_ref[s], cv_ref[s]
        k = jnp.concatenate([ck.astype(BF16), kn_ref[s]], axis=0)
        v = jnp.concatenate([cv.astype(BF16), vn_ref[s]], axis=0)
        o_ref[s] = _sink_attention(q_ref[s], k, v, valid, sink_ref, tq).astype(BF16)
        ok_ref[s] = _shift_in(ck, knf_ref[s], steps)
        ov_ref[s] = _shift_in(cv, vnf_ref[s], steps)


def _attn_sample(sinks, q3, kn3, vn3, knf3, vnf3, cache_k, cache_v, steps, bs=8):
    db, tq, _ = q3.shape
    wbuf = cache_k.shape[1]
    assert steps <= SUBLANES and knf3.shape[1] == SUBLANES and db % bs == 0
    blk = lambda r, w: pl.BlockSpec((bs, r, w), lambda b: (b, 0, 0))
    cache = jax.ShapeDtypeStruct((db, wbuf, D_KV), F32)
    return pl.pallas_call(
        functools.partial(_attn_sample_kernel, bs=bs, tq=tq, wbuf=wbuf, steps=steps),
        out_shape=(jax.ShapeDtypeStruct((db, tq, D_Q), BF16), cache, cache),
        grid=(db // bs,),
        in_specs=[pl.BlockSpec(memory_space=pltpu.SMEM), blk(tq, D_Q), blk(tq, D_KV), blk(tq, D_KV),
                  blk(SUBLANES, D_KV), blk(SUBLANES, D_KV), blk(wbuf, D_KV), blk(wbuf, D_KV)],
        out_specs=(blk(tq, D_Q), blk(wbuf, D_KV), blk(wbuf, D_KV)),
        compiler_params=_params("parallel"),
        name="attn_sample",
    )(sinks, q3, kn3, vn3, knf3, vnf3, cache_k, cache_v)


def _merge_out_kernel(x_ref, yr_ref, o_ref, ga_ref, gb_ref, wr_ref, wa_ref, wo_ref, g_ref, x1_ref, n2_ref):
    a = _dot(yr_ref[...], wr_ref[...])
    b = _dot(o_ref[...], wa_ref[...])
    merged = (jax.nn.sigmoid(ga_ref[...]) * a + jax.nn.sigmoid(gb_ref[...]) * b).astype(BF16)
    x1 = x_ref[...] + _dot(merged, wo_ref[...])
    x1_ref[...] = x1
    n2_ref[...] = _rmsnorm_rows(x1, g_ref[...]).astype(BF16)


def _merge_out(x, yr, o, z, wr, wa, wo, g, tm=256):
    n = x.shape[0]
    tm = min(tm, n)
    row = lambda w, c=0: pl.BlockSpec((tm, w), lambda i: (i, c))
    const = lambda r, c: pl.BlockSpec((r, c), lambda i: (0, 0), pipeline_mode=pl.Buffered(1))
    return pl.pallas_call(
        _merge_out_kernel,
        out_shape=(jax.ShapeDtypeStruct((n, D_MODEL), F32), jax.ShapeDtypeStruct((n, D_MODEL), BF16)),
        grid=(n // tm,),
        in_specs=[row(D_MODEL), row(D_RNN), row(D_Q), row(D_MODEL, OFF_GA // D_MODEL), row(D_MODEL, OFF_GB // D_MODEL),
                  const(D_RNN, D_MODEL), const(D_Q, D_MODEL), const(D_MODEL, D_MODEL), const(1, D_MODEL)],
        out_specs=(row(D_MODEL), row(D_MODEL)),
        compiler_params=_params("parallel"),
        name="merge_out",
    )(x, yr, o, z, z, wr, wa, wo, g)


def _topk_slabs(slabs, keys, k):
    t = slabs[0].shape[1]
    slot = lax.broadcasted_iota(jnp.int32, (k, t), 0)
    vals = jnp.zeros((k, t), F32)
    ids = jnp.zeros((k, t), F32)
    for r in range(k):
        best, bkey = slabs[0], keys[0]
        for sl, ky in zip(slabs[1:], keys[1:]):
            take = sl > best
            best = jnp.where(take, sl, best)
            bkey = jnp.where(take, ky, bkey)
        m = jnp.max(best, axis=0, keepdims=True)
        i = jnp.min(jnp.where(best == m, bkey, jnp.inf), axis=0, keepdims=True)
        vals = jnp.where(slot == r, m, vals)
        ids = jnp.where(slot == r, i, ids)
        slabs = [jnp.where(ky == i, -jnp.inf, sl) for sl, ky in zip(slabs, keys)]
    return vals, ids


def _peer_route_kernel(n2_ref, wq_ref, sk_ref, idx_ref, gate_ref):
    q = _dot(n2_ref[...], wq_ref[...]).astype(BF16)
    gates, experts = [], []
    tcol = q.shape[0]
    row8 = lax.broadcasted_iota(jnp.int32, (SUBLANES, tcol), 0).astype(F32)
    key_slabs = [row8 + float(a) for a in range(0, N_KEYS, SUBLANES)]
    assert PEER_TOPK == 2 * SUBLANES
    for h in range(PEER_HEADS):
        top = []
        for p in range(2):
            c = (2 * h + p) * D_HALF_KEY
            s = _dot_nt(sk_ref[2 * h + p], q[:, c:c + D_HALF_KEY])
            top.append(_topk_slabs([s[a:a + SUBLANES] for a in range(0, N_KEYS, SUBLANES)], key_slabs, PEER_TOPK))
        (s1, i1), (s2, i2) = top
        hk = SUBLANES
        lo, hi = slice(0, hk), slice(hk, 2 * hk)
        one = lambda a: slice(a, a + 1)
        groups = [(one(0), lo, row8), (one(0), hi, row8 + float(hk))]
        groups += [(one(a), lo, row8 + float(a * PEER_TOPK)) for a in range(1, hk)]
        groups += [(hi, one(0), (row8 + float(hk)) * float(PEER_TOPK))]
        cand = [s1[ra] + s2[rb] for ra, rb, _ in groups]
        cidx = [i1[ra] * float(N_KEYS) + i2[rb] for ra, rb, _ in groups]
        rid = [key for _, _, key in groups]
        top_s, pos = _topk_slabs(cand, rid, PEER_TOPK)
        slot = lax.broadcasted_iota(jnp.int32, top_s.shape, 0)
        ids = jnp.zeros(top_s.shape, F32)
        for r in range(PEER_TOPK):
            hit = [jnp.where(ky == pos[r:r + 1, :], ci, -1.0) for ky, ci in zip(rid, cidx)]
            e = jnp.max(functools.reduce(jnp.maximum, hit), axis=0, keepdims=True)
            ids = jnp.where(slot == r, e, ids)
        w = jnp.exp(top_s - top_s[0:1, :])
        gates.append(w / jnp.sum(w, axis=0, keepdims=True))
        experts.append(ids)
    gate_ref[...] = jnp.concatenate(gates, axis=0).T
    idx_ref[...] = jnp.concatenate(experts, axis=0).T.astype(jnp.int32)


def _peer_route(n2, wq, sk, tm=512):
    n = n2.shape[0]
    tm = min(tm, n)
    return pl.pallas_call(
        _peer_route_kernel,
        out_shape=(jax.ShapeDtypeStruct((n, N_PICKS), jnp.int32), jax.ShapeDtypeStruct((n, N_PICKS), F32)),
        grid=(n // tm,),
        in_specs=[pl.BlockSpec((tm, D_MODEL), lambda i: (i, 0)),
                  pl.BlockSpec((D_MODEL, PEER_HEADS * D_KEY), lambda i: (0, 0), pipeline_mode=pl.Buffered(1)),
                  pl.BlockSpec((2 * PEER_HEADS, N_KEYS, D_HALF_KEY), lambda i: (0, 0, 0))],
        out_specs=(pl.BlockSpec((tm, N_PICKS), lambda i: (i, 0)), pl.BlockSpec((tm, N_PICKS), lambda i: (i, 0))),
        compiler_params=_params("parallel"),
        name="peer_route",
    )(n2, wq, sk)


D_TILES = D_MODEL // LANES
assert D_TILES == BF16_TILE_ROWS
PICK_GROUP = 16
WAIT_GROUP = 4


def _peer_mix_kernel(idx_ref, idxn_ref, x_ref, gate_ref, exp_ref, sel_ref, tab_ref, o_ref, buf_a, buf_b,
                     sem_ref, *, tb, nsteps):
    i = pl.program_id(0)
    bufs = (buf_a, buf_b)

    group_rows = WAIT_GROUP * N_PICKS

    def start_token(ids_ref, row, half, t, picks=(0, N_PICKS)):
        for j in range(*picks):
            pltpu.make_async_copy(tab_ref.at[ids_ref[row, j]], bufs[half].at[t * N_PICKS + j],
                                  sem_ref.at[half, t // WAIT_GROUP]).start(priority=j % 2)

    def wait_group(half, g):
        pltpu.make_async_copy(tab_ref.at[pl.ds(0, group_rows)], bufs[half].at[pl.ds(g * group_rows, group_rows)],
                              sem_ref.at[half, g]).wait()

    @pl.when(i == 0)
    def _():
        for t in range(tb):
            start_token(idx_ref, t, 0, t)

    rows = N_PICKS * D_TILES
    diag = (lax.broadcasted_iota(jnp.int32, (D_TILES, rows), 1) % D_TILES
            == lax.broadcasted_iota(jnp.int32, (D_TILES, rows), 0))

    sel = sel_ref[...]
    lane = lax.broadcasted_iota(jnp.int32, (N_PICKS, LANES), 1)

    first_picks, second_picks = (0, N_PICKS // 2), (N_PICKS // 2, N_PICKS)

    def run_tile(half, start_other):
        buf = bufs[half]
        hcols = jnp.zeros((N_PICKS, LANES), F32)
        for t in range(tb):
            if t % WAIT_GROUP == 0:
                wait_group(half, t // WAIT_GROUP)
            start_other(t, first_picks)
            u3 = buf[t * N_PICKS:(t + 1) * N_PICKS, 0:D_TILES, :]
            p = (u3 * x_ref[half * tb + t][None, :, :]).reshape(rows, LANES)
            grp = PICK_GROUP * D_TILES
            r = jnp.concatenate([_dot(sel, p[c * grp:(c + 1) * grp]) for c in range(N_PICKS // PICK_GROUP)],
                                axis=0)
            hcols = jnp.where(lane == t, jnp.sum(r, axis=1, keepdims=True), hcols)
        h = hcols.T[0:tb, :]
        w = (jax.nn.gelu(h) * gate_ref[half * tb:(half + 1) * tb, :]).astype(BF16)
        wrow = _dot(w, exp_ref[...])
        for t in range(tb):
            start_other(t, second_picks)
            v = buf[t * N_PICKS:(t + 1) * N_PICKS, D_TILES:2 * D_TILES, :].reshape(rows, LANES)
            wm = jnp.where(diag, wrow[t:t + 1, :], 0.0).astype(BF16)
            o_ref[half * tb + t] = _dot(wm, v)

    run_tile(0, lambda t, picks: start_token(idx_ref, tb + t, 1, t, picks))
    run_tile(1, lambda t, picks: start_token(idxn_ref, t, 0, t, picks))

    @pl.when(i == nsteps - 1)
    def _():
        for g in range(tb // WAIT_GROUP):
            wait_group(0, g)


def _peer_mix(idx, gates, x3, table, expand, sel, tb=BF16_TILE_ROWS):
    n = idx.shape[0]
    nsteps = n // (2 * tb)
    rows = N_PICKS * D_TILES
    out = pl.pallas_call(
        functools.partial(_peer_mix_kernel, tb=tb, nsteps=nsteps),
        out_shape=jax.ShapeDtypeStruct((n, D_TILES, LANES), F32),
        grid=(nsteps,),
        in_specs=[
            pl.BlockSpec((2 * tb, N_PICKS), lambda i: (i, 0), memory_space=pltpu.SMEM),
            pl.BlockSpec((tb, N_PICKS), lambda i: (jnp.minimum(2 * i + 2, 2 * nsteps - 2), 0),
                         memory_space=pltpu.SMEM),
            pl.BlockSpec((2 * tb, D_TILES, LANES), lambda i: (i, 0, 0)),
            pl.BlockSpec((2 * tb, N_PICKS), lambda i: (i, 0)),
            pl.BlockSpec((N_PICKS, rows), lambda i: (0, 0)),
            pl.BlockSpec((PICK_GROUP, PICK_GROUP * D_TILES), lambda i: (0, 0)),
            pl.BlockSpec(memory_space=pl.ANY),
        ],
        out_specs=pl.BlockSpec((2 * tb, D_TILES, LANES), lambda i: (i, 0, 0)),
        scratch_shapes=[pltpu.VMEM((tb * N_PICKS, 2 * D_TILES, LANES), BF16),
                        pltpu.VMEM((tb * N_PICKS, 2 * D_TILES, LANES), BF16),
                        pltpu.SemaphoreType.DMA((2, tb // WAIT_GROUP))],
        compiler_params=_params("arbitrary"),
        name="peer_mix",
    )(idx, idx, x3, gates, expand, sel, table)
    return out.reshape(n, D_MODEL)


def _pack_table_kernel(u_ref, v_ref, o_ref):
    o_ref[:, 0:D_TILES, :] = u_ref[...].astype(BF16)
    o_ref[:, D_TILES:2 * D_TILES, :] = v_ref[...].astype(BF16)


def _pack_table(u3, v3, te=256):
    e = u3.shape[0]
    te = min(te, e)
    blk = pl.BlockSpec((te, D_TILES, LANES), lambda i: (i, 0, 0))
    return pl.pallas_call(
        _pack_table_kernel,
        out_shape=jax.ShapeDtypeStruct((e, 2 * D_TILES, LANES), BF16),
        grid=(e // te,),
        in_specs=[blk, blk],
        out_specs=pl.BlockSpec((te, 2 * D_TILES, LANES), lambda i: (i, 0, 0)),
        compiler_params=_params("parallel"),
        name="pack_table",
    )(u3, v3)


def _ple_kernel(x1_ref, po_ref, ple_ref, g_ref, wp_ref, wg_ref, y_ref):
    x2 = x1_ref[...] + po_ref[...]
    n3 = _rmsnorm_rows(x2, g_ref[...]).astype(BF16)
    emb = _dot(ple_ref[...].astype(BF16), wp_ref[...])
    y_ref[...] = x2 + emb * jax.nn.sigmoid(_dot(n3, wg_ref[...]))


def _ple(x1, po, ple, g, wp, wg, tm=512):
    n = x1.shape[0]
    tm = min(tm, n)
    row = pl.BlockSpec((tm, D_MODEL), lambda i: (i, 0))
    return pl.pallas_call(
        _ple_kernel,
        out_shape=jax.ShapeDtypeStruct((n, D_MODEL), F32),
        grid=(n // tm,),
        in_specs=[row, row, pl.BlockSpec((tm, D_PLE), lambda i: (i, 0)),
                  pl.BlockSpec((1, D_MODEL), lambda i: (0, 0)),
                  pl.BlockSpec((D_PLE, D_MODEL), lambda i: (0, 0)),
                  pl.BlockSpec((D_MODEL, D_MODEL), lambda i: (0, 0))],
        out_specs=row,
        compiler_params=_params("parallel"),
        name="ple",
    )(x1, po, ple, g, wp, wg)


def _rope_tables(pos):
    half = ROT_DIM // 2
    inv = ROPE_THETA ** (-jnp.arange(0, ROT_DIM, 2, dtype=F32) / ROT_DIM)
    ang = pos.astype(F32)[:, None] * inv[None, :]
    cos, sin = jnp.cos(ang), jnp.sin(ang)
    n = pos.shape[0]
    pad = jnp.zeros((n, HEAD_DIM - ROT_DIM), F32)
    zh = jnp.zeros((n, half), F32)
    cos_h = jnp.concatenate([cos, cos, pad + 1.0], axis=1)
    up_h = jnp.concatenate([-sin, zh, pad], axis=1)
    dn_h = jnp.concatenate([zh, sin, pad], axis=1)
    rep = LANES // HEAD_DIM
    return tuple(jnp.tile(a, (1, rep)) for a in (cos_h, up_h, dn_h))


def _block_diag(w):
    eye = jnp.eye(RNN_BLOCKS, dtype=w.dtype)
    return jnp.einsum("ncd,nm->ncmd", w, eye).reshape(D_RNN, D_RNN)


def _head_perm():
    return np.array([j * GROUP + g for g in range(GROUP) for j in range(N_KV_HEADS)])


def _token_pipeline_tail(x, z, yr, o, w, ple):
    x1, n2 = _merge_out(x, yr, o, z, w["proj_rnn"], w["proj_attn"], w["out"], w["norm_ffn"])
    idx, gate = _peer_route(n2, w["peer_q"], w["sub_keys"])
    po = _peer_mix(idx, gate, n2.reshape(-1, D_TILES, LANES), w["peer_table"], w["expand"], w["pick_sum"])
    return _ple(x1, po, ple, w["norm_ple"], w["ple"], w["ple_gate"])


def kernel(x_prompt, x_sample, p_prompt, p_sample, state_conv, state_rglru, cache_k, cache_v, norm_mix, w_in, conv_w, conv_b, w_rgate, b_rgate, w_igate, b_igate, lru_lambda, w_proj_rnn, q_norm, k_norm, attn_sinks, w_proj_attn, w_out, norm_ffn, w_peer_q, peer_sub_keys, peer_u, peer_v, w_ple, norm_ple, w_ple_gate):
    depth = w_in.shape[0]
    assert depth == 1
    l = 0
    B, S, _ = x_prompt.shape
    DB, DS, _ = x_sample.shape
    wbuf = cache_k.shape[2]

    hp = _head_perm()
    offs = np.cumsum([0, D_RNN, D_RNN, D_Q, D_KV, D_KV, D_MODEL, D_MODEL])
    xr_c, gr_c, q_c, k_c, v_c, ga_c, gb_c = [np.arange(offs[i], offs[i + 1]) for i in range(7)]
    cols = np.concatenate([ga_c, gb_c, xr_c, gr_c, q_c, k_c, v_c])
    row2 = lambda a: a[l].reshape(1, -1)
    w = {
        "proj_rnn": w_proj_rnn[l].astype(BF16),
        "proj_attn": w_proj_attn[l].reshape(N_Q_HEADS, HEAD_DIM, D_MODEL)[hp].reshape(D_Q, D_MODEL).astype(BF16),
        "out": w_out[l].astype(BF16),
        "norm_ffn": row2(norm_ffn),
        "peer_q": w_peer_q[l].astype(BF16),
        "sub_keys": peer_sub_keys[l].reshape(2 * PEER_HEADS, N_KEYS, D_HALF_KEY).astype(BF16),
        "peer_table": _pack_table(peer_u[l].reshape(N_EXPERTS, D_TILES, LANES),
                                  peer_v[l].reshape(N_EXPERTS, D_TILES, LANES)),
        "norm_ple": row2(norm_ple),
        "ple": w_ple[l].astype(BF16),
        "ple_gate": w_ple_gate[l].astype(BF16),
    }
    expand = np.repeat(np.eye(N_PICKS, dtype=np.float32), D_TILES, axis=1)
    w["expand"] = jnp.asarray(expand, BF16)
    w["pick_sum"] = jnp.asarray(expand[:PICK_GROUP, :PICK_GROUP * D_TILES], BF16)
    runs = np.split(cols, np.flatnonzero(np.diff(cols) != 1) + 1)
    w_in_b = jnp.concatenate([w_in[l][:, r[0]:r[-1] + 1] for r in runs], axis=1).astype(BF16)
    g_mix = row2(norm_mix)
    cw, cb = conv_w[l], row2(conv_b)
    wr, br = _block_diag(w_rgate[l]).astype(BF16), row2(b_rgate)
    wi, bi = _block_diag(w_igate[l]).astype(BF16), row2(b_igate)
    lam = row2(lru_lambda)
    rep = LANES // HEAD_DIM
    qg = jnp.tile(q_norm[l], rep).reshape(1, LANES)
    kg = jnp.tile(k_norm[l], rep).reshape(1, LANES)
    seg_ones = jnp.asarray(
        np.kron(np.eye(LANES // HEAD_DIM, dtype=np.float32), np.ones((HEAD_DIM, HEAD_DIM), np.float32)), BF16)
    sinks = attn_sinks[l]

    xp = x_prompt.reshape(B * S, D_MODEL)
    zp = _in_proj(xp, g_mix, w_in_b)
    yr_p, h_p = _rnn_prompt(zp, B, S, cw, cb, wr, br, wi, bi, lam)
    tabs_p = _rope_tables(jnp.arange(S, dtype=jnp.int32))
    q_p, kf_p, kb_p, vb_p = _qk_prep(zp, tabs_p, qg, kg, seg_ones)
    o_p = _attn_prompt(sinks, q_p, kb_p, vb_p, B, S)
    y_p = _token_pipeline_tail(xp, zp, yr_p, o_p, w, p_prompt[l].reshape(B * S, D_PLE))

    zp3 = zp.reshape(B, S, D_IN)
    keep = min(WINDOW, S)
    prompt_conv = zp3[:, S - (CONV_W - 1):, OFF_XR:OFF_XR + D_RNN]
    prompt_k = kf_p.reshape(B, S, D_KV)[:, S - keep:].reshape(B, keep, N_KV_HEADS, HEAD_DIM)
    prompt_v = zp3[:, S - keep:, OFF_V:OFF_V + D_KV].reshape(B, keep, N_KV_HEADS, HEAD_DIM)

    ns = DB * DS
    xs = x_sample.reshape(ns, D_MODEL)
    zs = _in_proj(xs, g_mix, w_in_b)
    zs3 = zs.reshape(DB, DS, D_IN)
    yr_s, h_s = _rnn_sample(zs.reshape(DB, DS * D_IN), state_conv[l].reshape(DB, (CONV_W - 1) * D_RNN),
                            state_rglru[l], cw, cb, wr, br, wi, bi, lam)
    yr_s = yr_s.reshape(ns, D_RNN)
    pos_s = PAST_LEN + jnp.arange(DS, dtype=jnp.int32)
    tabs_s = tuple(jnp.tile(a, (DB, 1)) for a in _rope_tables(pos_s))
    q_s, kf_s, kb_s, vb_s = _qk_prep(zs, tabs_s, qg, kg, seg_ones)
    tq = BF16_TILE_ROWS
    pad_t = lambda a, rows: jnp.pad(a.reshape(DB, DS, -1), ((0, 0), (0, rows - DS), (0, 0)))
    ck = cache_k[l].reshape(DB, wbuf, D_KV)
    cv = cache_v[l].reshape(DB, wbuf, D_KV)
    o_s, sample_k, sample_v = _attn_sample(
        sinks, pad_t(q_s, tq), pad_t(kb_s, tq), pad_t(vb_s, tq), pad_t(kf_s, SUBLANES),
        pad_t(zs3[:, :, OFF_V:OFF_V + D_KV], SUBLANES), ck, cv, DS)
    o_s = o_s[:, :DS].reshape(ns, D_Q)
    y_s = _token_pipeline_tail(xs, zs, yr_s, o_s, w, p_sample[l].reshape(ns, D_PLE))

    sample_conv = jnp.concatenate([state_conv[l], zs3[:, :, OFF_XR:OFF_XR + D_RNN]], axis=1)[:, DS:]
    sample_k = sample_k.reshape(DB, wbuf, N_KV_HEADS, HEAD_DIM)
    sample_v = sample_v.reshape(DB, wbuf, N_KV_HEADS, HEAD_DIM)

    return (y_p.reshape(B, S, D_MODEL), y_s.reshape(DB, DS, D_MODEL),
            prompt_conv[None], h_p.reshape(1, B, D_RNN), prompt_k[None], prompt_v[None],
            sample_conv[None], h_s[None], sample_k[None], sample_v[None])
```
